```python
import math
import jax, jax.numpy as jnp
from jax import lax
import numpy as np

D_MODEL = 2048
BATCH = 4
SEQ = 2048
DEPTH = 2
DEC_BATCH = 128
DEC_SEQ = 1
PAST_LEN = 16384
PAGE_SIZE = 128

DN_HEADS = 8
DN_DK = 128
DN_DV = 128
DN_QK = DN_HEADS * DN_DK
DN_V = DN_HEADS * DN_DV
CONV_W = 4
CONV_CH = 2 * DN_QK + DN_V
CHUNK = 64
S5_CH = D_MODEL // 2
S5_GROUP = 16
S5_GROUPS = S5_CH // S5_GROUP
S5_STATE = 64
FFN_HIDDEN = -((-8 * D_MODEL) // (3 * 256)) * 256
IN_SIZES = (DN_QK, DN_QK, DN_V, DN_V, DN_HEADS, DN_HEADS, S5_CH, D_MODEL, D_MODEL)
IN_DIM = DN_QK + DN_QK + DN_V + DN_V + DN_HEADS + DN_HEADS + S5_CH + D_MODEL + D_MODEL
NORM_EPS = 1e-6
L2_EPS = 1e-6

kernel_name = "hybrid_gdn_s5_decoder_step"


def rms_norm(x, w):
    xf = x.astype(jnp.float32)
    y = xf * lax.rsqrt(jnp.mean(xf * xf, axis=-1, keepdims=True) + NORM_EPS)
    return (y * w.astype(jnp.float32)).astype(x.dtype)


def l2norm(x):
    return x * lax.rsqrt(jnp.sum(x * x, axis=-1, keepdims=True) + L2_EPS)


def causal_conv(x, buf, w):
    L = x.shape[1]
    xp = jnp.concatenate([buf.astype(x.dtype), x], axis=1)
    y = sum(xp[:, i:i + L] * w[i] for i in range(CONV_W))
    return jax.nn.silu(y), xp[:, L:]


def gated_delta_chunked(q, k, v, beta, g, s0):
    b, L, h, _ = q.shape
    dv = v.shape[-1]
    n = L // CHUNK

    def blk(t):
        t = t.reshape((b, n, CHUNK, h) + t.shape[3:])
        return jnp.moveaxis(jnp.moveaxis(t, 1, 0), 2, 3)

    q, k, v, beta, g = (blk(t) for t in (q, k, v, beta, g))
    gc = jnp.cumsum(g, axis=-1)
    idx = jnp.arange(CHUNK)
    causal = idx[:, None] >= idx[None, :]
    strict = idx[:, None] > idx[None, :]
    gamma = jnp.exp(jnp.where(causal, gc[..., :, None] - gc[..., None, :], -jnp.inf))
    kb = k * beta[..., None]
    vb = v * beta[..., None]
    a_mat = jnp.where(strict, jnp.einsum('nbhid,nbhjd->nbhij', kb, k) * gamma, 0.0)
    eye = jnp.eye(CHUNK, dtype=q.dtype)
    t_inv = lax.linalg.triangular_solve(eye + a_mat, jnp.broadcast_to(eye, a_mat.shape),
                                        left_side=True, lower=True)
    u = t_inv @ vb
    w = t_inv @ (kb * jnp.exp(gc)[..., None])
    qk = jnp.einsum('nbhid,nbhjd->nbhij', q, k) * gamma
    q_dec = q * jnp.exp(gc)[..., None]
    g_last = gc[..., -1]
    k_dec = k * jnp.exp(g_last[..., None] - gc)[..., None]

    def step(S, inp):
        u_c, w_c, qk_c, qd_c, kd_c, gl_c = inp
        v_new = u_c - w_c @ S
        o = qd_c @ S + qk_c @ v_new
        S = S * jnp.exp(gl_c)[..., None, None] + jnp.swapaxes(kd_c, -1, -2) @ v_new
        return S, o

    S, o = lax.scan(step, s0, (u, w, qk, q_dec, k_dec, g_last))
    o = jnp.swapaxes(jnp.moveaxis(o, 0, 1), 2, 3).reshape(b, L, h, dv)
    return o, S


def gated_delta_recurrent(q, k, v, beta, g, s0):
    def step(S, inp):
        q_t, k_t, v_t, b_t, g_t = inp
        S = S * jnp.exp(g_t)[..., None, None]
        v_new = (v_t - jnp.einsum('bhk,bhkv->bhv', k_t, S)) * b_t[..., None]
        S = S + jnp.einsum('bhk,bhv->bhkv', k_t, v_new)
        return S, jnp.einsum('bhk,bhkv->bhv', q_t, S)

    xs = tuple(jnp.moveaxis(t, 1, 0) for t in (q, k, v, beta, g))
    S, o = lax.scan(step, s0, xs)
    return jnp.moveaxis(o, 0, 1), S


def s5_scan(u, x0_re, x0_im, lam_re, lam_im, log_dt, b_re, b_im, c_re, c_im, d_skip):
    f32 = jnp.float32
    lam_re, lam_im, log_dt = lam_re.astype(f32), lam_im.astype(f32), log_dt.astype(f32)
    b_re, b_im, c_re, c_im = b_re.astype(f32), b_im.astype(f32), c_re.astype(f32), c_im.astype(f32)
    bsz, L, _ = u.shape
    ug = u.reshape(bsz, L, S5_GROUPS, S5_GROUP)
    dt = jnp.exp(log_dt)[:, None]
    mag = jnp.exp(lam_re * dt)
    ar = mag * jnp.cos(lam_im * dt)
    ai = mag * jnp.sin(lam_im * dt)
    nr = ar - 1.0
    den = lam_re * lam_re + lam_im * lam_im
    fr = (nr * lam_re + ai * lam_im) / den
    fi = (ai * lam_re - nr * lam_im) / den
    bbar_re = fr[..., None] * b_re - fi[..., None] * b_im
    bbar_im = fr[..., None] * b_im + fi[..., None] * b_re
    bu_re = jnp.einsum('gpc,blgc->blgp', bbar_re, ug)
    bu_im = jnp.einsum('gpc,blgc->blgp', bbar_im, ug)
    a_re = jnp.broadcast_to(ar, bu_re.shape)
    a_im = jnp.broadcast_to(ai, bu_im.shape)

    def combine(e1, e2):
        a1r, a1i, b1r, b1i = e1
        a2r, a2i, b2r, b2i = e2
        return (a2r * a1r - a2i * a1i, a2r * a1i + a2i * a1r,
                a2r * b1r - a2i * b1i + b2r, a2r * b1i + a2i * b1r + b2i)

    pr, pim, sr, si = lax.associative_scan(combine, (a_re, a_im, bu_re, bu_im), axis=1)
    x0r = x0_re.astype(f32)[:, None]
    x0i = x0_im.astype(f32)[:, None]
    x_re = pr * x0r - pim * x0i + sr
    x_im = pr * x0i + pim * x0r + si
    y = jnp.einsum('gcp,blgp->blgc', c_re, x_re) - jnp.einsum('gcp,blgp->blgc', c_im, x_im)
    y = y.reshape(bsz, L, S5_CH) + d_skip.astype(f32) * u
    return y, x_re[:, -1], x_im[:, -1]


def decoder_layer(x, conv_buf, dn_s, s5r, s5i, lw, chunked):
    (n1, w_in, conv_w, a_log, dt_bias, dn_nw, w_br_dn, lam_re, lam_im, log_dt,
     b_re, b_im, c_re, c_im, d_s, w_glu, w_br_s5, w_out, n2, wg, wu, wd) = lw
    f32 = jnp.float32
    bsz, L, _ = x.shape
    h = rms_norm(x, n1)
    proj = h @ w_in
    offs = np.cumsum(np.array(IN_SIZES))[:-1].tolist()
    q, k, v, z, bt, a, s5u, gd, gs = jnp.split(proj, offs, axis=-1)

    qkv, conv_new = causal_conv(jnp.concatenate([q, k, v], axis=-1), conv_buf, conv_w)
    qkv = qkv.astype(f32)
    q = l2norm(qkv[..., :DN_QK].reshape(bsz, L, DN_HEADS, DN_DK)) * (DN_DK ** -0.5)
    k = l2norm(qkv[..., DN_QK:2 * DN_QK].reshape(bsz, L, DN_HEADS, DN_DK))
    v = qkv[..., 2 * DN_QK:].reshape(bsz, L, DN_HEADS, DN_DV)
    beta = jax.nn.sigmoid(bt.astype(f32))
    g = -jnp.exp(a_log.astype(f32)) * jax.nn.softplus(a.astype(f32) + dt_bias.astype(f32))
    s0 = dn_s.astype(f32)
    if chunked:
        o, dn_new = gated_delta_chunked(q, k, v, beta, g, s0)
    else:
        o, dn_new = gated_delta_recurrent(q, k, v, beta, g, s0)
    zf = jax.nn.silu(z.astype(f32)).reshape(bsz, L, DN_HEADS, DN_DV)
    o = o * lax.rsqrt(jnp.mean(o * o, axis=-1, keepdims=True) + NORM_EPS) * dn_nw.astype(f32) * zf
    br_dn = o.reshape(bsz, L, DN_V).astype(x.dtype) @ w_br_dn

    y5, s5r_new, s5i_new = s5_scan(s5u.astype(f32), s5r, s5i, lam_re, lam_im, log_dt,
                                   b_re, b_im, c_re, c_im, d_s)
    g5 = jax.nn.gelu(y5)
    g5 = g5 * jax.nn.sigmoid(g5 @ w_glu.astype(f32))
    br_s5 = g5.astype(x.dtype) @ w_br_s5

    merged = jax.nn.sigmoid(gd) * br_dn + jax.nn.sigmoid(gs) * br_s5
    x = x + merged @ w_out

    h2 = rms_norm(x, n2)
    x = x + (jax.nn.silu(h2 @ wg) * (h2 @ wu)) @ wd
    return x, conv_new, dn_new, s5r_new, s5i_new


def setup_inputs(seed: int = 0) -> dict:
    key = jax.random.key(seed)
    ks = iter(jax.random.split(key, 40))
    f32 = jnp.float32

    def nrm(shape, scale):
        return jax.random.normal(next(ks), shape, f32) * scale

    def unif(shape, lo, hi):
        return jax.random.uniform(next(ks), shape, f32, lo, hi)

    x_prompt = nrm((BATCH, SEQ, D_MODEL), 1.0)
    x_sample = nrm((DEC_BATCH, DEC_SEQ, D_MODEL), 1.0)
    state_dn_conv = nrm((DEPTH, DEC_BATCH, CONV_W - 1, CONV_CH), 1.0)
    state_dn_ssm = nrm((DEPTH, DEC_BATCH, DN_HEADS, DN_DK, DN_DV), 0.1)
    state_s5_re = nrm((DEPTH, DEC_BATCH, S5_GROUPS, S5_STATE), 0.3)
    state_s5_im = nrm((DEPTH, DEC_BATCH, S5_GROUPS, S5_STATE), 0.3)

    norm1 = 1.0 + nrm((DEPTH, D_MODEL), 0.01)
    w_in = nrm((DEPTH, D_MODEL, IN_DIM), D_MODEL ** -0.5)
    dn_conv_w = nrm((DEPTH, CONV_W, CONV_CH), CONV_W ** -0.5)
    dn_a_log = jnp.log(unif((DEPTH, DN_HEADS), 1.0, 16.0))
    dt = jnp.exp(unif((DEPTH, DN_HEADS), math.log(1e-3), math.log(1e-1)))
    dn_dt_bias = dt + jnp.log(-jnp.expm1(-dt))
    dn_norm_w = 1.0 + nrm((DEPTH, DN_DV), 0.01)
    w_br_dn = nrm((DEPTH, DN_V, D_MODEL), DN_V ** -0.5)
    s5_lam_re = -0.5 + nrm((DEPTH, S5_GROUPS, S5_STATE), 0.01)
    s5_lam_im = jnp.broadcast_to(math.pi * jnp.arange(S5_STATE, dtype=f32), (DEPTH, S5_GROUPS, S5_STATE)) + 0.0
    s5_log_dt = unif((DEPTH, S5_GROUPS), math.log(1e-3), math.log(1e-1))
    s5_b_re = nrm((DEPTH, S5_GROUPS, S5_STATE, S5_GROUP), (2 * S5_GROUP) ** -0.5)
    s5_b_im = nrm((DEPTH, S5_GROUPS, S5_STATE, S5_GROUP), (2 * S5_GROUP) ** -0.5)
    s5_c_re = nrm((DEPTH, S5_GROUPS, S5_GROUP, S5_STATE), S5_STATE ** -0.5)
    s5_c_im = nrm((DEPTH, S5_GROUPS, S5_GROUP, S5_STATE), S5_STATE ** -0.5)
    s5_d = nrm((DEPTH, S5_CH), 1.0)
    w_glu = nrm((DEPTH, S5_CH, S5_CH), S5_CH ** -0.5)
    w_br_s5 = nrm((DEPTH, S5_CH, D_MODEL), S5_CH ** -0.5)
    w_out = nrm((DEPTH, D_MODEL, D_MODEL), D_MODEL ** -0.5)
    norm2 = 1.0 + nrm((DEPTH, D_MODEL), 0.01)
    w_ffn_gate = nrm((DEPTH, D_MODEL, FFN_HIDDEN), D_MODEL ** -0.5)
    w_ffn_up = nrm((DEPTH, D_MODEL, FFN_HIDDEN), D_MODEL ** -0.5)
    w_ffn_down = nrm((DEPTH, FFN_HIDDEN, D_MODEL), FFN_HIDDEN ** -0.5)
    norm_f = 1.0 + nrm((D_MODEL,), 0.01)
    return {
        "x_prompt": x_prompt, "x_sample": x_sample,
        "state_dn_conv": state_dn_conv, "state_dn_ssm": state_dn_ssm,
        "state_s5_re": state_s5_re, "state_s5_im": state_s5_im,
        "norm1": norm1, "w_in": w_in, "dn_conv_w": dn_conv_w, "dn_a_log": dn_a_log,
        "dn_dt_bias": dn_dt_bias, "dn_norm_w": dn_norm_w, "w_br_dn": w_br_dn,
        "s5_lam_re": s5_lam_re, "s5_lam_im": s5_lam_im, "s5_log_dt": s5_log_dt,
        "s5_b_re": s5_b_re, "s5_b_im": s5_b_im, "s5_c_re": s5_c_re, "s5_c_im": s5_c_im,
        "s5_d": s5_d, "w_glu": w_glu, "w_br_s5": w_br_s5, "w_out": w_out, "norm2": norm2,
        "w_ffn_gate": w_ffn_gate, "w_ffn_up": w_ffn_up, "w_ffn_down": w_ffn_down,
        "norm_f": norm_f,
    }


def reference(x_prompt, x_sample, state_dn_conv, state_dn_ssm, state_s5_re, state_s5_im,
              norm1, w_in, dn_conv_w, dn_a_log, dn_dt_bias, dn_norm_w, w_br_dn,
              s5_lam_re, s5_lam_im, s5_log_dt, s5_b_re, s5_b_im, s5_c_re, s5_c_im, s5_d,
              w_glu, w_br_s5, w_out, norm2, w_ffn_gate, w_ffn_up, w_ffn_down, norm_f):
    weights = (norm1, w_in, dn_conv_w, dn_a_log, dn_dt_bias, dn_norm_w, w_br_dn,
               s5_lam_re, s5_lam_im, s5_log_dt, s5_b_re, s5_b_im, s5_c_re, s5_c_im, s5_d,
               w_glu, w_br_s5, w_out, norm2, w_ffn_gate, w_ffn_up, w_ffn_down)

    def run(x, conv, ssm, sre, sim, chunked):
        conv_l, ssm_l, sre_l, sim_l = [], [], [], []
        for i in range(DEPTH):
            lw = tuple(w[i] for w in weights)
            x, c_new, d_new, r_new, m_new = decoder_layer(x, conv[i], ssm[i], sre[i], sim[i], lw, chunked)
            conv_l.append(c_new)
            ssm_l.append(d_new)
            sre_l.append(r_new)
            sim_l.append(m_new)
        return (rms_norm(x, norm_f), jnp.stack(conv_l), jnp.stack(ssm_l),
                jnp.stack(sre_l), jnp.stack(sim_l))

    f32 = jnp.float32
    p_conv0 = jnp.zeros((DEPTH, BATCH, CONV_W - 1, CONV_CH), x_prompt.dtype)
    p_ssm0 = jnp.zeros((DEPTH, BATCH, DN_HEADS, DN_DK, DN_DV), f32)
    p_s50 = jnp.zeros((DEPTH, BATCH, S5_GROUPS, S5_STATE), f32)
    y_prompt, p_dn_conv, p_dn_ssm, p_s5_re, p_s5_im = run(x_prompt, p_conv0, p_ssm0, p_s50, p_s50, True)
    y_sample, s_dn_conv, s_dn_ssm, s_s5_re, s_s5_im = run(
        x_sample, state_dn_conv, state_dn_ssm, state_s5_re, state_s5_im, False)
    return (y_prompt, y_sample, p_dn_conv, p_dn_ssm, p_s5_re, p_s5_im,
            s_dn_conv, s_dn_ssm, s_s5_re, s_s5_im)
```

```python
import functools
import math

import jax
import jax.numpy as jnp
from jax import lax
from jax.experimental import pallas as pl
from jax.experimental.pallas import tpu as pltpu

F32 = jnp.float32
BF16 = jnp.bfloat16

NORM_EPS = 1e-6
L2_EPS = 1e-6
LANES = 128
SUBLANES = 8
VMEM_CAP_BYTES = 56 * 1024 * 1024
GDN_CHUNK = 128
S5_SEGMENTS = SUBLANES


def _vmem_limit(nbytes):
    return int(min(VMEM_CAP_BYTES, nbytes * 5 // 4 + (4 << 20)))


def _pick_tile(n, target, mult):
    best = None
    for t in range(mult, min(n, target) + 1, mult):
        if n % t == 0:
            best = t
    return best if best is not None else n


def _sigmoid(x):
    return 1.0 / (1.0 + jnp.exp(-x))


def _silu(x):
    return x * _sigmoid(x)


def _softplus(x):
    return jnp.maximum(x, 0.0) + jnp.log1p(jnp.exp(-jnp.abs(x)))


def _gelu_tanh(x):
    c = math.sqrt(2.0 / math.pi)
    return 0.5 * x * (1.0 + jnp.tanh(c * (x + 0.044715 * (x * x * x))))


def _bdot(a, b):
    return jnp.dot(a.astype(BF16), b.astype(BF16), preferred_element_type=F32)


def _bdot_nt(a, b):
    return lax.dot_general(a.astype(BF16), b.astype(BF16), (((1,), (1,)), ((), ())),
                           preferred_element_type=F32)


def _rmsnorm_kernel(x_ref, w_ref, o_ref):
    x = x_ref[...]
    y = x * lax.rsqrt(jnp.mean(x * x, axis=-1, keepdims=True) + NORM_EPS)
    o_ref[...] = (y * w_ref[...]).astype(o_ref.dtype)


def _rmsnorm(x, w_row, out_dtype):
    m, d = x.shape
    tr = _pick_tile(m, 640, 16)
    nbytes = 2 * tr * d * 4 + 2 * tr * d * jnp.dtype(out_dtype).itemsize + 2 * tr * d * 4
    return pl.pallas_call(
        _rmsnorm_kernel,
        out_shape=jax.ShapeDtypeStruct((m, d), out_dtype),
        grid=(m // tr,),
        in_specs=[pl.BlockSpec((tr, d), lambda i: (i, 0)),
                  pl.BlockSpec((1, d), lambda i: (0, 0))],
        out_specs=pl.BlockSpec((tr, d), lambda i: (i, 0)),
        compiler_params=pltpu.CompilerParams(
            dimension_semantics=("parallel",), vmem_limit_bytes=_vmem_limit(nbytes)),
        name="rmsnorm",
    )(x, w_row)


def _mm_kernel(*refs, a_idx, n_a, n_extra, nk, epilogue):
    n_w = len(a_idx)
    a_refs = refs[:n_a]
    w_refs = refs[n_a:n_a + n_w]
    e_refs = refs[n_a + n_w:n_a + n_w + n_extra]
    o_ref = refs[n_a + n_w + n_extra]
    acc_refs = refs[n_a + n_w + n_extra + 1:]
    a_vals = [r[...] for r in a_refs]
    parts = [jnp.dot(a_vals[ai], w[...].astype(BF16), preferred_element_type=F32)
             for ai, w in zip(a_idx, w_refs)]
    if nk == 1:
        o_ref[...] = epilogue(parts, [e[...] for e in e_refs]).astype(o_ref.dtype)
        return
    k = pl.program_id(2)

    @pl.when(k == 0)
    def _():
        for acc, p in zip(acc_refs, parts):
            acc[...] = p

    @pl.when(k > 0)
    def _():
        for acc, p in zip(acc_refs, parts):
            acc[...] += p

    @pl.when(k == nk - 1)
    def _():
        o_ref[...] = epilogue([acc[...] for acc in acc_refs],
                              [e[...] for e in e_refs]).astype(o_ref.dtype)


def _fused_matmul(a_list, w_list, extras, epilogue, n_out, out_dtype, *, tm, tn, tk=None, name):
    m, kdim = a_list[0].shape
    tk = kdim if tk is None else tk
    nk = kdim // tk
    assert m % tm == 0 and n_out % tn == 0 and kdim % tk == 0
    assert nk == 1 or all(a.shape[1] == kdim for a in a_list)
    in_specs, args = [], []
    for a in a_list:
        assert a.shape[0] == m
        in_specs.append(pl.BlockSpec((tm, tk if nk > 1 else a.shape[1]), lambda i, j, k: (i, k)))
        args.append(a)
    for ai, w, layer, col0 in w_list:
        assert col0 % tn == 0 and w.shape[1] == a_list[ai].shape[1]
        in_specs.append(pl.BlockSpec((None, tk if nk > 1 else w.shape[1], tn),
                                     lambda i, j, k, layer=layer, off=col0 // tn: (layer, k, j + off)))
        args.append(w)
    for e, col0 in extras:
        assert col0 % tn == 0 and e.shape[0] == m
        in_specs.append(pl.BlockSpec((tm, tn), lambda i, j, k, off=col0 // tn: (i, j + off)))
        args.append(e)
    n_w = len(w_list)
    scratch = [pltpu.VMEM((tm, tn), F32) for _ in range(n_w)] if nk > 1 else []
    osz = jnp.dtype(out_dtype).itemsize
    kb = lambda a: tk if nk > 1 else a.shape[1]
    nbytes = (sum(2 * tm * kb(a) * 2 for a in a_list)
              + sum(kb(a_list[ai]) * tn * (2 * w.dtype.itemsize + 2) for ai, w, _, _ in w_list)
              + sum(2 * tm * tn * e.dtype.itemsize for e, _ in extras)
              + 2 * tm * tn * osz + (2 + n_w) * tm * tn * 4)
    kern = functools.partial(_mm_kernel, a_idx=tuple(ai for ai, _, _, _ in w_list),
                             n_a=len(a_list), n_extra=len(extras), nk=nk, epilogue=epilogue)
    return pl.pallas_call(
        kern,
        out_shape=jax.ShapeDtypeStruct((m, n_out), out_dtype),
        grid=(m // tm, n_out // tn, nk),
        in_specs=in_specs,
        out_specs=pl.BlockSpec((tm, tn), lambda i, j, k: (i, j)),
        scratch_shapes=scratch,
        compiler_params=pltpu.CompilerParams(
            dimension_semantics=("parallel", "parallel", "arbitrary"),
            vmem_limit_bytes=_vmem_limit(nbytes)),
        name=name,
    )(*args)


def _ep_identity(accs, extras):
    return accs[0]


def _ep_residual(accs, extras):
    return extras[0] + accs[0]


def _ep_swiglu(accs, extras):
    return _silu(accs[0]) * accs[1]


def _ep_glu_self(accs, extras):
    g5 = extras[0]
    return g5 * _sigmoid(accs[0])


def _ep_gated_merge(accs, extras):
    return _sigmoid(extras[0]) * accs[0] + _sigmoid(extras[1]) * accs[1]


def _head_columns(ba, alog_row, dtb_row, head, n_heads):
    lane = lax.broadcasted_iota(jnp.int32, ba.shape, 1)
    beta_all = _sigmoid(ba)
    g_all = -jnp.exp(alog_row) * _softplus(ba + dtb_row)
    beta = jnp.sum(jnp.where(lane == head, beta_all, 0.0), axis=-1, keepdims=True)
    g = jnp.sum(jnp.where(lane == head + n_heads, g_all, 0.0), axis=-1, keepdims=True)
    return beta, g


def _l2norm_rows(x):
    return x * lax.rsqrt(jnp.sum(x * x, axis=-1, keepdims=True) + L2_EPS)


def _gated_out_norm(o, z, nw_row):
    y = o * lax.rsqrt(jnp.mean(o * o, axis=-1, keepdims=True) + NORM_EPS)
    return y * nw_row * _silu(z)


def _gdn_prompt_kernel(q_ref, k_ref, v_ref, z_ref, ba_ref, cwq_ref, cwk_ref, cwv_ref,
                       alog_ref, dtb_ref, nw_ref, o_ref, s_ref,
                       qn, kn, vn, gb, bb, us, ws, qks, qds, kdt, gl, osc, *, n_heads):
    L, dk = q_ref.shape
    C = GDN_CHUNK
    n_chunks = L // C
    head = pl.program_id(1)

    row = lax.broadcasted_iota(jnp.int32, (L, dk), 0)

    def conv_silu(x_ref, cw_ref):
        x = x_ref[...]
        cw = cw_ref[...]
        n_taps = cw.shape[0]
        y = x * cw[n_taps - 1:n_taps, :]
        for j in range(1, n_taps):
            xs = jnp.where(row >= j, pltpu.roll(x, j, 0), 0.0)
            y = y + xs * cw[n_taps - 1 - j:n_taps - j, :]
        return _silu(y)

    qn[...] = _l2norm_rows(conv_silu(q_ref, cwq_ref)) * (dk ** -0.5)
    kn[...] = _l2norm_rows(conv_silu(k_ref, cwk_ref))
    vn[...] = conv_silu(v_ref, cwv_ref)
    beta, g = _head_columns(ba_ref[...], alog_ref[...], dtb_ref[...], head, n_heads)
    bb[...] = jnp.broadcast_to(beta, (L, dk))
    gb[...] = jnp.broadcast_to(g, (L, dk))

    ri = lax.broadcasted_iota(jnp.int32, (C, C), 0)
    ci = lax.broadcasted_iota(jnp.int32, (C, C), 1)
    causal = ri >= ci
    strict = ri > ci
    tri_incl = jnp.where(causal, 1.0, 0.0).astype(F32)
    eye = jnp.where(ri == ci, 1.0, 0.0).astype(F32)
    level_masks = []
    n = 1
    while n < C:
        sh = n.bit_length() - 1
        same_2n = (ri >> (sh + 1)) == (ci >> (sh + 1))
        diff_n = (ri >> sh) != (ci >> sh)
        level_masks.append(jnp.where(same_2n & diff_n & strict, 1.0, 0.0).astype(F32))
        n *= 2

    def intra(c, carry):
        r0 = pl.multiple_of(c * C, C)
        rows = pl.ds(r0, C)
        q = qn[rows, :]
        k = kn[rows, :]
        v = vn[rows, :]
        bet = bb[rows, :]
        gcb = jnp.dot(tri_incl, gb[rows, :], precision=lax.Precision.HIGHEST,
                      preferred_element_type=F32)
        diff = gcb - gcb.T
        gamma = jnp.where(causal, jnp.exp(jnp.minimum(diff, 0.0)), 0.0)
        kb = k * bet
        a_mat = jnp.where(strict, _bdot_nt(kb, k) * gamma, 0.0)
        qks[rows, :] = _bdot_nt(q, k) * gamma
        t = eye - a_mat * level_masks[0]
        for m in level_masks[1:]:
            t = t - _bdot(t, _bdot(a_mat * m, t))
        eg = jnp.exp(gcb)
        us[rows, :] = _bdot(t, v * bet)
        ws[rows, :] = _bdot(t, kb * eg)
        qds[rows, :] = q * eg
        g_last = gcb[C - 1:C, :]
        kdt[rows, :] = (k * jnp.exp(g_last - gcb)).T
        gl[pl.ds(pl.multiple_of(c * SUBLANES, SUBLANES), SUBLANES), :] = jnp.broadcast_to(
            jnp.exp(g_last), (SUBLANES, dk))
        return carry

    lax.fori_loop(0, n_chunks, intra, 0)

    def inter(c, s):
        r0 = pl.multiple_of(c * C, C)
        rows = pl.ds(r0, C)
        v_new = us[rows, :] - _bdot(ws[rows, :], s)
        osc[rows, :] = _bdot(qds[rows, :], s) + _bdot(qks[rows, :], v_new)
        decay = gl[pl.ds(pl.multiple_of(c * SUBLANES, SUBLANES), 1), :]
        return s * decay + _bdot(kdt[rows, :], v_new)

    s_fin = lax.fori_loop(0, n_chunks, inter, jnp.zeros((dk, dk), F32))
    s_ref[...] = s_fin
    o_ref[...] = _gated_out_norm(osc[...], z_ref[...], nw_ref[...]).astype(o_ref.dtype)


def _gdn_prompt(qkvz, ba, conv_w, alog_pad, dtb_pad, norm_w, layer, batch, seq, n_heads, dk):
    H = n_heads
    L = seq

    def col(off):
        return pl.BlockSpec((L, dk), lambda b, h, off=off: (b, h + off))

    def cw(off):
        return pl.BlockSpec((None, conv_w.shape[1], dk), lambda b, h, off=off: (layer, 0, h + off))

    def prow(a):
        return pl.BlockSpec((None, 1, a.shape[2]), lambda b, h: (layer, 0, 0))

    scr_l = lambda: pltpu.VMEM((L, dk), F32)
    nbytes = 2 * 5 * L * dk * 4 + 12 * L * dk * 4 + 2 * L * dk * 2 + 40 * GDN_CHUNK * GDN_CHUNK * 4
    return pl.pallas_call(
        functools.partial(_gdn_prompt_kernel, n_heads=H),
        out_shape=(jax.ShapeDtypeStruct((batch * L, H * dk), BF16),
                   jax.ShapeDtypeStruct((batch, H, dk, dk), F32)),
        grid=(batch, H),
        in_specs=[col(0), col(H), col(2 * H), col(3 * H),
                  pl.BlockSpec((L, LANES), lambda b, h: (b, 0)),
                  cw(0), cw(H), cw(2 * H),
                  prow(alog_pad), prow(dtb_pad), prow(norm_w)],
        out_specs=(pl.BlockSpec((L, dk), lambda b, h: (b, h)),
                   pl.BlockSpec((None, None, dk, dk), lambda b, h: (b, h, 0, 0))),
        scratch_shapes=[scr_l(), scr_l(), scr_l(), scr_l(), scr_l(),
                        scr_l(), scr_l(), scr_l(), scr_l(), scr_l(),
                        pltpu.VMEM((L // GDN_CHUNK * SUBLANES, dk), F32),
                        scr_l()],
        compiler_params=pltpu.CompilerParams(
            dimension_semantics=("parallel", "parallel"),
            vmem_limit_bytes=_vmem_limit(nbytes)),
        name="gdn_prompt",
    )(qkvz, qkvz, qkvz, qkvz, ba, conv_w, conv_w, conv_w, alog_pad, dtb_pad, norm_w)


def _gdn_sample_kernel(q_ref, k_ref, v_ref, z_ref, ba_ref, bq_ref, bk_ref, bv_ref,
                       cwq_ref, cwk_ref, cwv_ref, alog_ref, dtb_ref, nw_ref, s_in_ref,
                       o_ref, s_out_ref, osc, *, n_heads):
    nb, dk = q_ref.shape
    head = pl.program_id(0)

    def conv_silu(x_ref, buf_ref, cw_ref):
        cw = cw_ref[...]
        n_taps = cw.shape[0]
        y = x_ref[...] * cw[n_taps - 1:n_taps, :]
        for i in range(n_taps - 1):
            y = y + buf_ref[i] * cw[i:i + 1, :]
        return _silu(y)

    q = _l2norm_rows(conv_silu(q_ref, bq_ref, cwq_ref)) * (dk ** -0.5)
    k = _l2norm_rows(conv_silu(k_ref, bk_ref, cwk_ref))
    v = conv_silu(v_ref, bv_ref, cwv_ref)
    beta, g = _head_columns(ba_ref[...], alog_ref[...], dtb_ref[...], head, n_heads)
    decay = jnp.exp(g)
    kt = jnp.concatenate([k, jnp.zeros((LANES - nb, dk), F32)], axis=0).T if nb < LANES else k.T
    qt = jnp.concatenate([q, jnp.zeros((LANES - nb, dk), F32)], axis=0).T if nb < LANES else q.T
    for b in range(nb):
        s = s_in_ref[b] * decay[b:b + 1, :]
        kcol = kt[:, b:b + 1]
        v_new = (v[b:b + 1, :] - jnp.sum(s * kcol, axis=0, keepdims=True)) * beta[b:b + 1, :]
        s = s + kcol * v_new
        s_out_ref[b] = s
        osc[b:b + 1, :] = jnp.sum(s * qt[:, b:b + 1], axis=0, keepdims=True)
    o_ref[...] = _gated_out_norm(osc[...], z_ref[...], nw_ref[...]).astype(o_ref.dtype)


def _gdn_sample(qkvz_s, ba_s, conv_buf_t, conv_w, alog_pad, dtb_pad, norm_w, state, layer,
                n_heads, dk):
    H = n_heads
    nb_total = qkvz_s.shape[0]
    nb = 16
    assert nb_total % nb == 0
    n_hist = conv_buf_t.shape[1]

    def col(off):
        return pl.BlockSpec((nb, dk), lambda h, i, off=off: (i, h + off))

    def buf(off):
        return pl.BlockSpec((None, n_hist, nb, dk), lambda h, i, off=off: (layer, 0, i, h + off))

    def cw(off):
        return pl.BlockSpec((None, conv_w.shape[1], dk), lambda h, i, off=off: (layer, 0, h + off))

    def prow(a):
        return pl.BlockSpec((None, 1, a.shape[2]), lambda h, i: (layer, 0, 0))

    nbytes = 4 * nb * dk * dk * 4 + 64 * nb * dk * 4 + 64 * dk * dk * 4
    return pl.pallas_call(
        functools.partial(_gdn_sample_kernel, n_heads=H),
        out_shape=(jax.ShapeDtypeStruct((nb_total, H * dk), BF16),
                   jax.ShapeDtypeStruct((nb_total, H, dk, dk), F32)),
        grid=(H, nb_total // nb),
        in_specs=[col(0), col(H), col(2 * H), col(3 * H),
                  pl.BlockSpec((nb, LANES), lambda h, i: (i, 0)),
                  buf(0), buf(H), buf(2 * H),
                  cw(0), cw(H), cw(2 * H),
                  prow(alog_pad), prow(dtb_pad), prow(norm_w),
                  pl.BlockSpec((None, nb, None, dk, dk), lambda h, i: (layer, i, h, 0, 0))],
        out_specs=(pl.BlockSpec((nb, dk), lambda h, i: (i, h)),
                   pl.BlockSpec((nb, None, dk, dk), lambda h, i: (i, h, 0, 0))),
        scratch_shapes=[pltpu.VMEM((nb, dk), F32)],
        compiler_params=pltpu.CompilerParams(
            dimension_semantics=("parallel", "parallel"),
            vmem_limit_bytes=_vmem_limit(nbytes)),
        name="gdn_sample",
    )(qkvz_s, qkvz_s, qkvz_s, qkvz_s, ba_s, conv_buf_t, conv_buf_t, conv_buf_t,
      conv_w, conv_w, conv_w, alog_pad, dtb_pad, norm_w, state)


def _s5_tables(lam_re, lam_im, log_dt, b_re, b_im, c_re, c_im):
    G, P = lam_re.shape
    gc = b_re.shape[-1]
    gpb = LANES // gc
    nblk = G // gpb
    dt = jnp.exp(log_dt)[:, None]
    mag = jnp.exp(lam_re * dt)
    ar = mag * jnp.cos(lam_im * dt)
    ai = mag * jnp.sin(lam_im * dt)
    nr = ar - 1.0
    den = lam_re * lam_re + lam_im * lam_im
    fr = (nr * lam_re + ai * lam_im) / den
    fi = (ai * lam_re - nr * lam_im) / den
    bbar_re = fr[..., None] * b_re - fi[..., None] * b_im
    bbar_im = fr[..., None] * b_im + fi[..., None] * b_re
    eye = jnp.eye(gpb, dtype=F32)

    def bmat(bb):
        t = bb.reshape(nblk, gpb, P, gc)
        return jnp.einsum('jgpc,gh->jgchp', t, eye).reshape(nblk, gpb * gc, gpb * P)

    def cmat(cc):
        t = cc.reshape(nblk, gpb, gc, P)
        return jnp.einsum('jgcp,gh->jhpgc', t, eye).reshape(nblk, gpb * P, gpb * gc)

    b_blk = jnp.concatenate([bmat(bbar_re), bmat(bbar_im)], axis=2).astype(BF16)
    c_blk = jnp.concatenate([cmat(c_re), -cmat(c_im)], axis=1).astype(BF16)
    return (b_blk, c_blk, ar.reshape(nblk, 1, gpb * P), ai.reshape(nblk, 1, gpb * P))


def _s5_prompt_kernel(u_ref, bblk_ref, cblk_ref, ar_ref, ai_ref, d_ref,
                      g5_ref, g5b_ref, xre_ref, xim_ref, xs):
    L = u_ref.shape[0]
    ns = ar_ref.shape[-1]
    nseg = S5_SEGMENTS
    seg = L // nseg
    n_tiles = xs.shape[0]
    u = u_ref[...]
    bu = jnp.dot(u.astype(BF16), bblk_ref[...], preferred_element_type=F32)
    for j in range(n_tiles):
        xs[j] = bu[:, j * LANES:(j + 1) * LANES]
    ar = jnp.broadcast_to(ar_ref[...], (nseg, ns))
    ai = jnp.broadcast_to(ai_ref[...], (nseg, ns))

    def step(x, t):
        xr, xi = x
        r = jnp.concatenate([xs[j, pl.ds(t, nseg, stride=seg), :] for j in range(n_tiles)], axis=1)
        return (ar * xr - ai * xi + r[:, :ns], ar * xi + ai * xr + r[:, ns:])

    zero = jnp.zeros((nseg, ns), F32)
    er, ei = lax.fori_loop(0, seg, lambda t, x: step(x, t), (zero, zero))
    pr, pi = ar_ref[...], ai_ref[...]
    for _ in range(seg.bit_length() - 1):
        pr, pi = pr * pr - pi * pi, 2.0 * pr * pi
    assert seg == 1 << (seg.bit_length() - 1)
    cr = [jnp.zeros((1, ns), F32)]
    ci = [jnp.zeros((1, ns), F32)]
    for s in range(nseg - 1):
        cr.append(er[s:s + 1] + pr * cr[s] - pi * ci[s])
        ci.append(ei[s:s + 1] + pr * ci[s] + pi * cr[s])
    x0 = (jnp.concatenate(cr, axis=0), jnp.concatenate(ci, axis=0))

    def step_store(t, x):
        xr, xi = step(x, t)
        xn = jnp.concatenate([xr, xi], axis=1)
        for j in range(n_tiles):
            xs[j, pl.ds(t, nseg, stride=seg), :] = xn[:, j * LANES:(j + 1) * LANES]
        return (xr, xi)

    xr, xi = lax.fori_loop(0, seg, step_store, x0)
    xre_ref[...] = xr[nseg - 1:nseg]
    xim_ref[...] = xi[nseg - 1:nseg]
    x_all = jnp.concatenate([xs[j] for j in range(n_tiles)], axis=1).astype(BF16)
    y = jnp.dot(x_all, cblk_ref[...], preferred_element_type=F32) + d_ref[...] * u
    g5 = _gelu_tanh(y)
    g5_ref[...] = g5
    g5b_ref[...] = g5.astype(BF16)


def _s5_prompt(rest, tables, d_skip, layer, batch, seq, n_ch):
    b_blk, c_blk, ar, ai = tables
    nblk, _, ns2 = b_blk.shape
    ns = ns2 // 2
    L = seq
    nbytes = (2 * L * LANES * 4 + 2 * L * LANES * 6 + 3 * L * ns2 * 4 + L * ns2 * 2
              + 8 * LANES * ns2 * 2)
    blkp = lambda r, c: pl.BlockSpec((None, r, c), lambda b, j: (j, 0, 0))
    return pl.pallas_call(
        _s5_prompt_kernel,
        out_shape=(jax.ShapeDtypeStruct((batch * L, n_ch), F32),
                   jax.ShapeDtypeStruct((batch * L, n_ch), BF16),
                   jax.ShapeDtypeStruct((batch, 1, nblk * ns), F32),
                   jax.ShapeDtypeStruct((batch, 1, nblk * ns), F32)),
        grid=(batch, nblk),
        in_specs=[pl.BlockSpec((L, LANES), lambda b, j: (b, j)),
                  blkp(LANES, ns2), blkp(ns2, LANES), blkp(1, ns), blkp(1, ns),
                  pl.BlockSpec((None, 1, LANES), lambda b, j: (layer, 0, j))],
        out_specs=(pl.BlockSpec((L, LANES), lambda b, j: (b, j)),
                   pl.BlockSpec((L, LANES), lambda b, j: (b, j)),
                   pl.BlockSpec((None, 1, ns), lambda b, j: (b, 0, j)),
                   pl.BlockSpec((None, 1, ns), lambda b, j: (b, 0, j))),
        scratch_shapes=[pltpu.VMEM((ns2 // LANES, L, LANES), F32)],
        compiler_params=pltpu.CompilerParams(
            dimension_semantics=("parallel", "parallel"),
            vmem_limit_bytes=_vmem_limit(nbytes)),
        name="s5_prompt",
    )(rest, b_blk, c_blk, ar, ai, d_skip)


def _s5_sample_kernel(u_ref, bblk_ref, cblk_ref, ar_ref, ai_ref, d_ref, x0r_ref, x0i_ref,
                      g5_ref, g5b_ref, xre_ref, xim_ref):
    ns = ar_ref.shape[-1]
    u = u_ref[...]
    bu = jnp.dot(u.astype(BF16), bblk_ref[...], preferred_element_type=F32)
    ar, ai = ar_ref[...], ai_ref[...]
    x0r, x0i = x0r_ref[...], x0i_ref[...]
    xr = ar * x0r - ai * x0i + bu[:, :ns]
    xi = ar * x0i + ai * x0r + bu[:, ns:]
    xre_ref[...] = xr
    xim_ref[...] = xi
    x = jnp.concatenate([xr, xi], axis=1)
    y = jnp.dot(x.astype(BF16), cblk_ref[...], preferred_element_type=F32) + d_ref[...] * u
    g5 = _gelu_tanh(y)
    g5_ref[...] = g5
    g5b_ref[...] = g5.astype(BF16)


def _s5_sample(rest_s, tables, d_skip, x0_re, x0_im, layer, n_ch):
    b_blk, c_blk, ar, ai = tables
    nblk, _, ns2 = b_blk.shape
    ns = ns2 // 2
    nb = rest_s.shape[0]
    blkp = lambda r, c: pl.BlockSpec((None, r, c), lambda j: (j, 0, 0))
    nbytes = 16 * nb * ns2 * 4 + 8 * LANES * ns2 * 2
    return pl.pallas_call(
        _s5_sample_kernel,
        out_shape=(jax.ShapeDtypeStruct((nb, n_ch), F32),
                   jax.ShapeDtypeStruct((nb, n_ch), BF16),
                   jax.ShapeDtypeStruct((nb, nblk * ns), F32),
                   jax.ShapeDtypeStruct((nb, nblk * ns), F32)),
        grid=(nblk,),
        in_specs=[pl.BlockSpec((nb, LANES), lambda j: (0, j)),
                  blkp(LANES, ns2), blkp(ns2, LANES), blkp(1, ns), blkp(1, ns),
                  pl.BlockSpec((None, 1, LANES), lambda j: (layer, 0, j)),
                  pl.BlockSpec((None, nb, ns), lambda j: (layer, 0, j)),
                  pl.BlockSpec((None, nb, ns), lambda j: (layer, 0, j))],
        out_specs=(pl.BlockSpec((nb, LANES), lambda j: (0, j)),
                   pl.BlockSpec((nb, LANES), lambda j: (0, j)),
                   pl.BlockSpec((nb, ns), lambda j: (0, j)),
                   pl.BlockSpec((nb, ns), lambda j: (0, j))),
        compiler_params=pltpu.CompilerParams(
            dimension_semantics=("parallel",), vmem_limit_bytes=_vmem_limit(nbytes)),
        name="s5_sample",
    )(rest_s, b_blk, c_blk, ar, ai, d_skip, x0_re, x0_im)


def kernel(x_prompt, x_sample, state_dn_conv, state_dn_ssm, state_s5_re, state_s5_im, norm1, w_in, dn_conv_w, dn_a_log, dn_dt_bias, dn_norm_w, w_br_dn, s5_lam_re, s5_lam_im, s5_log_dt, s5_b_re, s5_b_im, s5_c_re, s5_c_im, s5_d, w_glu, w_br_s5, w_out, norm2, w_ffn_gate, w_ffn_up, w_ffn_down, norm_f):
    batch, seq, d_model = x_prompt.shape
    nb, dec_seq, _ = x_sample.shape
    assert dec_seq == 1
    depth, _, n_heads, dk, dv = state_dn_ssm.shape
    assert dk == LANES and dv == LANES and seq % GDN_CHUNK == 0 and seq % S5_SEGMENTS == 0
    qk_dim = n_heads * dk
    conv_ch = dn_conv_w.shape[2]
    assert conv_ch == 3 * qk_dim
    n_ch = s5_d.shape[1]
    n_groups, n_state = s5_lam_re.shape[1:]
    ffn = w_ffn_gate.shape[2]
    mp = batch * seq
    m = mp + nb
    z_end = 4 * qk_dim
    rest0 = z_end + 2 * n_heads
    assert w_in.shape[2] == rest0 + n_ch + 2 * d_model and 2 * n_heads <= LANES

    x = jnp.concatenate([x_prompt.reshape(mp, d_model), x_sample.reshape(nb, d_model)], axis=0)

    w_ba = jnp.pad(w_in[:, :, z_end:rest0], ((0, 0), (0, 0), (0, LANES - 2 * n_heads)))
    w_rest = w_in[:, :, rest0:].astype(BF16)
    pad_heads = lambda a: jnp.pad(a, ((0, 0), (n_heads, LANES - 2 * n_heads)))[:, None, :]
    alog_pad = pad_heads(dn_a_log)
    dtb_pad = pad_heads(dn_dt_bias)
    conv_buf_t = jnp.swapaxes(state_dn_conv, 1, 2)
    x0_re = state_s5_re.reshape(depth, nb, n_groups * n_state)
    x0_im = state_s5_im.reshape(depth, nb, n_groups * n_state)

    tm = _pick_tile(m, 1664, 64)
    outs = {k: [] for k in ("p_conv", "p_ssm", "p_re", "p_im", "s_conv", "s_ssm", "s_re", "s_im")}
    for l in range(depth):
        h = _rmsnorm(x, norm1[l][None, :], BF16)
        qkvz = _fused_matmul([h], [(0, w_in, l, 0)], [], _ep_identity, z_end, F32,
                             tm=tm, tn=512, name="in_qkvz")
        ba = _fused_matmul([h], [(0, w_ba, l, 0)], [], _ep_identity, LANES, F32,
                           tm=tm, tn=LANES, name="in_ba")
        rest = _fused_matmul([h], [(0, w_rest, l, 0)], [], _ep_identity, n_ch + 2 * d_model, F32,
                             tm=tm, tn=512, name="in_rest")

        o_p, ssm_p = _gdn_prompt(qkvz, ba, dn_conv_w, alog_pad, dtb_pad, dn_norm_w[:, None, :],
                                 l, batch, seq, n_heads, dk)
        qkvz_s, ba_s = qkvz[mp:], ba[mp:]
        o_s, ssm_s = _gdn_sample(qkvz_s, ba_s, conv_buf_t, dn_conv_w, alog_pad, dtb_pad,
                                 dn_norm_w[:, None, :], state_dn_ssm, l, n_heads, dk)
        o_all = jnp.concatenate([o_p, o_s], axis=0)
        n_hist = state_dn_conv.shape[2]
        pre = qkvz[:, :conv_ch]
        outs["p_conv"].append(pre[:mp].reshape(batch, seq, conv_ch)[:, seq - n_hist:])
        outs["s_conv"].append(jnp.concatenate(
            [state_dn_conv[l][:, 1:], pre[mp:][:, None, :]], axis=1))
        outs["p_ssm"].append(ssm_p)
        outs["s_ssm"].append(ssm_s)

        tables = _s5_tables(s5_lam_re[l], s5_lam_im[l], s5_log_dt[l], s5_b_re[l], s5_b_im[l],
                            s5_c_re[l], s5_c_im[l])
        g5_p, g5b_p, re_p, im_p = _s5_prompt(rest, tables, s5_d[:, None, :], l, batch, seq, n_ch)
        g5_s, g5b_s, re_s, im_s = _s5_sample(rest[mp:], tables, s5_d[:, None, :], x0_re, x0_im,
                                             l, n_ch)
        g5 = jnp.concatenate([g5_p, g5_s], axis=0)
        g5b = jnp.concatenate([g5b_p, g5b_s], axis=0)
        outs["p_re"].append(re_p.reshape(batch, n_groups, n_state))
        outs["p_im"].append(im_p.reshape(batch, n_groups, n_state))
        outs["s_re"].append(re_s.reshape(nb, n_groups, n_state))
        outs["s_im"].append(im_s.reshape(nb, n_groups, n_state))
        g5g = _fused_matmul([g5b], [(0, w_glu, l, 0)], [(g5, 0)], _ep_glu_self, n_ch, BF16,
                            tm=tm, tn=512, name="s5_glu")

        merged = _fused_matmul([o_all, g5g], [(0, w_br_dn, l, 0), (1, w_br_s5, l, 0)],
                               [(rest, n_ch), (rest, n_ch + d_model)], _ep_gated_merge,
                               d_model, BF16, tm=tm, tn=512, name="branch_merge")
        x = _fused_matmul([merged], [(0, w_out, l, 0)], [(x, 0)], _ep_residual, d_model, F32,
                          tm=tm, tn=512, name="out_proj")

        h2 = _rmsnorm(x, norm2[l][None, :], BF16)
        hmid = _fused_matmul([h2], [(0, w_ffn_gate, l, 0), (0, w_ffn_up, l, 0)], [], _ep_swiglu,
                             ffn, BF16, tm=tm, tn=256, name="ffn_up")
        x = _fused_matmul([hmid], [(0, w_ffn_down, l, 0)], [(x, 0)], _ep_residual, d_model, F32,
                          tm=tm, tn=512, tk=_pick_tile(ffn, 1024, LANES), name="ffn_down")

    y = _rmsnorm(x, norm_f[None, :], F32)
    st = lambda k: jnp.stack(outs[k])
    return (y[:mp].reshape(batch, seq, d_model), y[mp:].reshape(nb, 1, d_model),
            st("p_conv"), st("p_ssm"), st("p_re"), st("p_im"),
            st("s_conv"), st("s_ssm"), st("s_re"), st("s_im"))
```

```python
import functools
import math

import jax
import jax.numpy as jnp
from jax import lax
from jax.experimental import pallas as pl
from jax.experimental.pallas import tpu as pltpu

F32 = jnp.float32
BF16 = jnp.bfloat16

NORM_EPS = 1e-6
L2_EPS = 1e-6
LANES = 128
SUBLANES = 8
VMEM_CAP_BYTES = 56 * 1024 * 1024
GDN_CHUNK = 128
S5_SEGMENTS = SUBLANES


def _vmem_limit(nbytes):
    return int(min(VMEM_CAP_BYTES, nbytes * 5 // 4 + (4 << 20)))


def _pick_tile(n, target, mult):
    best = None
    for t in range(mult, min(n, target) + 1, mult):
        if n % t == 0:
            best = t
    return best if best is not None else n


def _sigmoid(x):
    return 1.0 / (1.0 + jnp.exp(-x))


def _silu(x):
    return x * _sigmoid(x)


def _softplus(x):
    return jnp.maximum(x, 0.0) + jnp.log1p(jnp.exp(-jnp.abs(x)))


def _gelu_tanh(x):
    c = math.sqrt(2.0 / math.pi)
    return 0.5 * x * (1.0 + jnp.tanh(c * (x + 0.044715 * (x * x * x))))


def _bdot(a, b):
    return jnp.dot(a.astype(BF16), b.astype(BF16), preferred_element_type=F32)


def _bdot_nt(a, b):
    return lax.dot_general(a.astype(BF16), b.astype(BF16), (((1,), (1,)), ((), ())),
                           preferred_element_type=F32)


def _rows_of(op):
    return op[0].shape[0] + op[1].shape[0] if isinstance(op, tuple) else op.shape[0]


def _row_specs(op, tm, ncols, index_map):
    if not isinstance(op, tuple):
        return [pl.BlockSpec((tm, ncols), index_map)], [op], None
    p, s = op
    tail = p.shape[0] % tm
    assert tail + s.shape[0] == tm and tail % 16 == 0, (p.shape, s.shape, tm)

    def s_map(*idx):
        return (0,) + tuple(index_map(*idx)[1:])

    return ([pl.BlockSpec((tm, ncols), index_map), pl.BlockSpec((s.shape[0], ncols), s_map)],
            [p, s], tail)


def _load_rows(refs, tail, last):
    if tail is None or not last:
        return refs[0][...]
    return jnp.concatenate([refs[0][:tail, :], refs[1][...]], axis=0)


def _on_row_tiles(i, n_tiles, any_split, body):
    if not any_split:
        body(False)
        return
    if n_tiles > 1:
        pl.when(i < n_tiles - 1)(lambda: body(False))
    pl.when(i == n_tiles - 1)(lambda: body(True))


def _rmsnorm_kernel(*refs, tail, n_tiles):
    x_refs, w_ref, o_ref = refs[:-2], refs[-2], refs[-1]

    def body(last):
        x = _load_rows(x_refs, tail, last)
        y = x * lax.rsqrt(jnp.mean(x * x, axis=-1, keepdims=True) + NORM_EPS)
        o_ref[...] = (y * w_ref[...]).astype(o_ref.dtype)

    _on_row_tiles(pl.program_id(0), n_tiles, tail is not None, body)


def _rmsnorm(x, w_row, out_dtype):
    m = _rows_of(x)
    d = w_row.shape[1]
    tr = _pick_tile(m, 832, 64)
    specs, arrs, tail = _row_specs(x, tr, d, lambda i: (i, 0))
    nbytes = 2 * tr * d * 4 + 2 * tr * d * jnp.dtype(out_dtype).itemsize + 3 * tr * d * 4
    return pl.pallas_call(
        functools.partial(_rmsnorm_kernel, tail=tail, n_tiles=m // tr),
        out_shape=jax.ShapeDtypeStruct((m, d), out_dtype),
        grid=(m // tr,),
        in_specs=specs + [pl.BlockSpec((1, d), lambda i: (0, 0))],
        out_specs=pl.BlockSpec((tr, d), lambda i: (i, 0)),
        compiler_params=pltpu.CompilerParams(
            dimension_semantics=("parallel",), vmem_limit_bytes=_vmem_limit(nbytes)),
        name="rmsnorm",
    )(*arrs, w_row)


def _mm_kernel(*refs, a_idx, a_groups, e_groups, n_tiles, epilogue):
    pos = 0
    a_refs = []
    for n, _ in a_groups:
        a_refs.append(refs[pos:pos + n])
        pos += n
    w_refs = refs[pos:pos + len(a_idx)]
    pos += len(a_idx)
    e_refs = []
    for n, _ in e_groups:
        e_refs.append(refs[pos:pos + n])
        pos += n
    o_ref = refs[pos]
    any_split = any(t is not None for _, t in a_groups + e_groups)

    def body(last):
        a_vals = [_load_rows(r, t, last) for r, (_, t) in zip(a_refs, a_groups)]
        parts = [jnp.dot(a_vals[ai], w[...].astype(BF16), preferred_element_type=F32)
                 for ai, w in zip(a_idx, w_refs)]
        e_vals = [_load_rows(r, t, last) for r, (_, t) in zip(e_refs, e_groups)]
        o_ref[...] = epilogue(parts, e_vals).astype(o_ref.dtype)

    _on_row_tiles(pl.program_id(0), n_tiles, any_split, body)


def _fused_matmul(a_list, w_list, extras, epilogue, n_out, out_dtype, *, tm, tn, name):
    m = _rows_of(a_list[0])
    assert m % tm == 0 and n_out % tn == 0
    in_specs, args, a_groups, e_groups = [], [], [], []
    kdims = []
    for a in a_list:
        kd = (a[0] if isinstance(a, tuple) else a).shape[1]
        specs, arrs, tail = _row_specs(a, tm, kd, lambda i, j: (i, 0))
        assert _rows_of(a) == m
        in_specs += specs
        args += arrs
        a_groups.append((len(arrs), tail))
        kdims.append(kd)
    for ai, w, layer, col0 in w_list:
        assert col0 % tn == 0 and w.shape[1] == kdims[ai]
        in_specs.append(pl.BlockSpec((None, w.shape[1], tn),
                                     lambda i, j, layer=layer, off=col0 // tn: (layer, 0, j + off)))
        args.append(w)
    for e, col0 in extras:
        assert col0 % tn == 0 and _rows_of(e) == m
        specs, arrs, tail = _row_specs(e, tm, tn, lambda i, j, off=col0 // tn: (i, j + off))
        in_specs += specs
        args += arrs
        e_groups.append((len(arrs), tail))
    osz = jnp.dtype(out_dtype).itemsize
    nbytes = (sum(3 * tm * kd * 2 for kd in kdims)
              + sum(kdims[ai] * tn * (2 * w.dtype.itemsize + 2) for ai, w, _, _ in w_list)
              + sum(3 * tm * tn * 4 for _ in extras)
              + 2 * tm * tn * osz + (2 + len(w_list)) * tm * tn * 4)
    kern = functools.partial(_mm_kernel, a_idx=tuple(ai for ai, _, _, _ in w_list),
                             a_groups=tuple(a_groups), e_groups=tuple(e_groups),
                             n_tiles=m // tm, epilogue=epilogue)
    return pl.pallas_call(
        kern,
        out_shape=jax.ShapeDtypeStruct((m, n_out), out_dtype),
        grid=(m // tm, n_out // tn),
        in_specs=in_specs,
        out_specs=pl.BlockSpec((tm, tn), lambda i, j: (i, j)),
        compiler_params=pltpu.CompilerParams(
            dimension_semantics=("parallel", "parallel"),
            vmem_limit_bytes=_vmem_limit(nbytes)),
        name=name,
    )(*args)


def _mm_residual_acc_kernel(a_ref, w_ref, x_ref, o_ref):
    part = jnp.dot(a_ref[...], w_ref[...].astype(BF16), preferred_element_type=F32)
    k = pl.program_id(2)

    @pl.when(k == 0)
    def _():
        o_ref[...] = x_ref[...] + part

    @pl.when(k > 0)
    def _():
        o_ref[...] += part


def _matmul_residual_ktiled(a, w, layer, x, *, tm, tn, tk, name):
    m, kdim = a.shape
    n_out = w.shape[2]
    assert m % tm == 0 and n_out % tn == 0 and kdim % tk == 0
    nbytes = 2 * tm * tk * 2 + tk * tn * (2 * w.dtype.itemsize + 2) + 5 * tm * tn * 4
    return pl.pallas_call(
        _mm_residual_acc_kernel,
        out_shape=jax.ShapeDtypeStruct((m, n_out), F32),
        grid=(m // tm, n_out // tn, kdim // tk),
        in_specs=[pl.BlockSpec((tm, tk), lambda i, j, k: (i, k)),
                  pl.BlockSpec((None, tk, tn), lambda i, j, k: (layer, k, j)),
                  pl.BlockSpec((tm, tn), lambda i, j, k: (i, j))],
        out_specs=pl.BlockSpec((tm, tn), lambda i, j, k: (i, j)),
        compiler_params=pltpu.CompilerParams(
            dimension_semantics=("parallel", "parallel", "arbitrary"),
            vmem_limit_bytes=_vmem_limit(nbytes)),
        name=name,
    )(a, w, x)


def _ep_identity(accs, extras):
    return accs[0]


def _ep_residual(accs, extras):
    return extras[0] + accs[0]


def _ep_swiglu(accs, extras):
    return _silu(accs[0]) * accs[1]


def _ep_glu_self(accs, extras):
    g5 = extras[0]
    return g5 * _sigmoid(accs[0])


def _ep_gated_merge(accs, extras):
    return _sigmoid(extras[0]) * accs[0] + _sigmoid(extras[1]) * accs[1]


def _head_columns(ba, alog_row, dtb_row, head, n_heads):
    lane = lax.broadcasted_iota(jnp.int32, ba.shape, 1)
    beta_all = _sigmoid(ba)
    g_all = -jnp.exp(alog_row) * _softplus(ba + dtb_row)
    beta = jnp.sum(jnp.where(lane == head, beta_all, 0.0), axis=-1, keepdims=True)
    g = jnp.sum(jnp.where(lane == head + n_heads, g_all, 0.0), axis=-1, keepdims=True)
    return beta, g


def _l2norm_rows(x):
    return x * lax.rsqrt(jnp.sum(x * x, axis=-1, keepdims=True) + L2_EPS)


def _gated_out_norm(o, z, nw_row):
    y = o * lax.rsqrt(jnp.mean(o * o, axis=-1, keepdims=True) + NORM_EPS)
    return y * nw_row * _silu(z)


def _gdn_prompt_kernel(q_ref, k_ref, v_ref, z_ref, ba_ref, cwq_ref, cwk_ref, cwv_ref,
                       alog_ref, dtb_ref, nw_ref, o_ref, s_ref,
                       qn, kn, vn, gb, bb, us, ws, qks, qds, kdt, gl, osc, *, n_heads):
    L, dk = q_ref.shape
    C = GDN_CHUNK
    n_chunks = L // C
    head = pl.program_id(1)

    row = lax.broadcasted_iota(jnp.int32, (L, dk), 0)

    def conv_silu(x_ref, cw_ref):
        x = x_ref[...]
        cw = cw_ref[...]
        n_taps = cw.shape[0]
        y = x * cw[n_taps - 1:n_taps, :]
        for j in range(1, n_taps):
            xs = jnp.where(row >= j, pltpu.roll(x, j, 0), 0.0)
            y = y + xs * cw[n_taps - 1 - j:n_taps - j, :]
        return _silu(y)

    qn[...] = _l2norm_rows(conv_silu(q_ref, cwq_ref)) * (dk ** -0.5)
    kn[...] = _l2norm_rows(conv_silu(k_ref, cwk_ref))
    vn[...] = conv_silu(v_ref, cwv_ref)
    beta, g = _head_columns(ba_ref[...], alog_ref[...], dtb_ref[...], head, n_heads)
    bb[...] = jnp.broadcast_to(beta, (L, dk))
    gb[...] = jnp.broadcast_to(g, (L, dk))

    ri = lax.broadcasted_iota(jnp.int32, (C, C), 0)
    ci = lax.broadcasted_iota(jnp.int32, (C, C), 1)
    causal = ri >= ci
    strict = ri > ci
    tri_incl = jnp.where(causal, 1.0, 0.0).astype(F32)
    eye = jnp.where(ri == ci, 1.0, 0.0).astype(F32)
    level_masks = []
    n = 1
    while n < C:
        sh = n.bit_length() - 1
        same_2n = (ri >> (sh + 1)) == (ci >> (sh + 1))
        diff_n = (ri >> sh) != (ci >> sh)
        level_masks.append(jnp.where(same_2n & diff_n & strict, 1.0, 0.0).astype(F32))
        n *= 2

    group = 4 if n_chunks % 4 == 0 else (2 if n_chunks % 2 == 0 else 1)

    def intra_group(i, carry):
        G = range(group)
        rows = [pl.ds(pl.multiple_of((i * group + j) * C, C), C) for j in G]
        k = [kn[r, :] for r in rows]
        bet = [bb[r, :] for r in rows]
        gcb = [jnp.dot(tri_incl, gb[r, :], precision=lax.Precision.HIGHEST,
                       preferred_element_type=F32) for r in rows]
        gamma = [jnp.where(causal, jnp.exp(jnp.minimum(g - g.T, 0.0)), 0.0) for g in gcb]
        kb = [k[j] * bet[j] for j in G]
        a_mat = [jnp.where(strict, _bdot_nt(kb[j], k[j]) * gamma[j], 0.0) for j in G]
        q = [qn[r, :] for r in rows]
        for j in G:
            qks[rows[j], :] = _bdot_nt(q[j], k[j]) * gamma[j]
        t = [eye - a * level_masks[0] for a in a_mat]
        for m in level_masks[1:]:
            x = [_bdot(a_mat[j] * m, t[j]) for j in G]
            t = [t[j] - _bdot(t[j], x[j]) for j in G]
        eg = [jnp.exp(g) for g in gcb]
        for j in G:
            us[rows[j], :] = _bdot(t[j], vn[rows[j], :] * bet[j])
        for j in G:
            ws[rows[j], :] = _bdot(t[j], kb[j] * eg[j])
        for j in G:
            qds[rows[j], :] = q[j] * eg[j]
            g_last = gcb[j][C - 1:C, :]
            kdt[rows[j], :] = (k[j] * jnp.exp(g_last - gcb[j])).T
            gl[pl.ds(pl.multiple_of((i * group + j) * SUBLANES, SUBLANES), SUBLANES), :] = (
                jnp.broadcast_to(jnp.exp(g_last), (SUBLANES, dk)))
        return carry

    lax.fori_loop(0, n_chunks // group, intra_group, 0)

    def inter(c, s):
        r0 = pl.multiple_of(c * C, C)
        rows = pl.ds(r0, C)
        v_new = us[rows, :] - _bdot(ws[rows, :], s)
        osc[rows, :] = _bdot(qds[rows, :], s) + _bdot(qks[rows, :], v_new)
        decay = gl[pl.ds(pl.multiple_of(c * SUBLANES, SUBLANES), 1), :]
        return s * decay + _bdot(kdt[rows, :], v_new)

    s_fin = lax.fori_loop(0, n_chunks, inter, jnp.zeros((dk, dk), F32))
    s_ref[...] = s_fin
    o_ref[...] = _gated_out_norm(osc[...], z_ref[...], nw_ref[...]).astype(o_ref.dtype)


def _gdn_prompt(qkvz, ba, conv_w, alog_pad, dtb_pad, norm_w, layer, batch, seq, n_heads, dk):
    H = n_heads
    L = seq

    def col(off):
        return pl.BlockSpec((L, dk), lambda b, h, off=off: (b, h + off))

    def cw(off):
        return pl.BlockSpec((None, conv_w.shape[1], dk), lambda b, h, off=off: (layer, 0, h + off))

    def prow(a):
        return pl.BlockSpec((None, 1, a.shape[2]), lambda b, h: (layer, 0, 0))

    scr_l = lambda: pltpu.VMEM((L, dk), F32)
    nbytes = 2 * 5 * L * dk * 4 + 12 * L * dk * 4 + 2 * L * dk * 2 + 40 * GDN_CHUNK * GDN_CHUNK * 4
    return pl.pallas_call(
        functools.partial(_gdn_prompt_kernel, n_heads=H),
        out_shape=(jax.ShapeDtypeStruct((batch * L, H * dk), BF16),
                   jax.ShapeDtypeStruct((batch, H, dk, dk), F32)),
        grid=(batch, H),
        in_specs=[col(0), col(H), col(2 * H), col(3 * H),
                  pl.BlockSpec((L, LANES), lambda b, h: (b, 0)),
                  cw(0), cw(H), cw(2 * H),
                  prow(alog_pad), prow(dtb_pad), prow(norm_w)],
        out_specs=(pl.BlockSpec((L, dk), lambda b, h: (b, h)),
                   pl.BlockSpec((None, None, dk, dk), lambda b, h: (b, h, 0, 0))),
        scratch_shapes=[scr_l(), scr_l(), scr_l(), scr_l(), scr_l(),
                        scr_l(), scr_l(), scr_l(), scr_l(), scr_l(),
                        pltpu.VMEM((L // GDN_CHUNK * SUBLANES, dk), F32),
                        scr_l()],
        compiler_params=pltpu.CompilerParams(
            dimension_semantics=("parallel", "parallel"),
            vmem_limit_bytes=_vmem_limit(nbytes)),
        name="gdn_prompt",
    )(qkvz, qkvz, qkvz, qkvz, ba, conv_w, conv_w, conv_w, alog_pad, dtb_pad, norm_w)


def _gdn_sample_kernel(q_ref, k_ref, v_ref, z_ref, ba_ref, bq_ref, bk_ref, bv_ref,
                       cwq_ref, cwk_ref, cwv_ref, alog_ref, dtb_ref, nw_ref, s_in_ref,
                       o_ref, s_out_ref, osc, *, n_heads):
    nb, dk = q_ref.shape
    head = pl.program_id(0)

    def conv_silu(x_ref, buf_ref, cw_ref):
        cw = cw_ref[...]
        n_taps = cw.shape[0]
        y = x_ref[...] * cw[n_taps - 1:n_taps, :]
        for i in range(n_taps - 1):
            y = y + buf_ref[i] * cw[i:i + 1, :]
        return _silu(y)

    q = _l2norm_rows(conv_silu(q_ref, bq_ref, cwq_ref)) * (dk ** -0.5)
    k = _l2norm_rows(conv_silu(k_ref, bk_ref, cwk_ref))
    v = conv_silu(v_ref, bv_ref, cwv_ref)
    beta, g = _head_columns(ba_ref[...], alog_ref[...], dtb_ref[...], head, n_heads)
    decay = jnp.exp(g)
    kt = jnp.concatenate([k, jnp.zeros((LANES - nb, dk), F32)], axis=0).T if nb < LANES else k.T
    qt = jnp.concatenate([q, jnp.zeros((LANES - nb, dk), F32)], axis=0).T if nb < LANES else q.T
    for b in range(nb):
        s = s_in_ref[b] * decay[b:b + 1, :]
        kcol = kt[:, b:b + 1]
        v_new = (v[b:b + 1, :] - jnp.sum(s * kcol, axis=0, keepdims=True)) * beta[b:b + 1, :]
        s = s + kcol * v_new
        s_out_ref[b] = s
        osc[b:b + 1, :] = jnp.sum(s * qt[:, b:b + 1], axis=0, keepdims=True)
    o_ref[...] = _gated_out_norm(osc[...], z_ref[...], nw_ref[...]).astype(o_ref.dtype)


def _gdn_sample(qkvz, ba, row0, conv_buf_t, conv_w, alog_pad, dtb_pad, norm_w, state, layer,
                n_heads, dk):
    H = n_heads
    nb_total = state.shape[1]
    nb = 16
    assert nb_total % nb == 0 and row0 % nb == 0
    r0 = row0 // nb
    n_hist = conv_buf_t.shape[1]

    def col(off):
        return pl.BlockSpec((nb, dk), lambda h, i, off=off: (i + r0, h + off))

    def buf(off):
        return pl.BlockSpec((None, n_hist, nb, dk), lambda h, i, off=off: (layer, 0, i, h + off))

    def cw(off):
        return pl.BlockSpec((None, conv_w.shape[1], dk), lambda h, i, off=off: (layer, 0, h + off))

    def prow(a):
        return pl.BlockSpec((None, 1, a.shape[2]), lambda h, i: (layer, 0, 0))

    nbytes = 4 * nb * dk * dk * 4 + 64 * nb * dk * 4 + 64 * dk * dk * 4
    return pl.pallas_call(
        functools.partial(_gdn_sample_kernel, n_heads=H),
        out_shape=(jax.ShapeDtypeStruct((nb_total, H * dk), BF16),
                   jax.ShapeDtypeStruct((nb_total, H, dk, dk), F32)),
        grid=(H, nb_total // nb),
        in_specs=[col(0), col(H), col(2 * H), col(3 * H),
                  pl.BlockSpec((nb, LANES), lambda h, i: (i + r0, 0)),
                  buf(0), buf(H), buf(2 * H),
                  cw(0), cw(H), cw(2 * H),
                  prow(alog_pad), prow(dtb_pad), prow(norm_w),
                  pl.BlockSpec((None, nb, None, dk, dk), lambda h, i: (layer, i, h, 0, 0))],
        out_specs=(pl.BlockSpec((nb, dk), lambda h, i: (i, h)),
                   pl.BlockSpec((nb, None, dk, dk), lambda h, i: (i, h, 0, 0))),
        scratch_shapes=[pltpu.VMEM((nb, dk), F32)],
        compiler_params=pltpu.CompilerParams(
            dimension_semantics=("parallel", "parallel"),
            vmem_limit_bytes=_vmem_limit(nbytes)),
        name="gdn_sample",
    )(qkvz, qkvz, qkvz, qkvz, ba, conv_buf_t, conv_buf_t, conv_buf_t,
      conv_w, conv_w, conv_w, alog_pad, dtb_pad, norm_w, state)


def _s5_tables(lam_re, lam_im, log_dt, b_re, b_im, c_re, c_im):
    G, P = lam_re.shape
    gc = b_re.shape[-1]
    gpb = LANES // gc
    nblk = G // gpb
    dt = jnp.exp(log_dt)[:, None]
    mag = jnp.exp(lam_re * dt)
    ar = mag * jnp.cos(lam_im * dt)
    ai = mag * jnp.sin(lam_im * dt)
    nr = ar - 1.0
    den = lam_re * lam_re + lam_im * lam_im
    fr = (nr * lam_re + ai * lam_im) / den
    fi = (ai * lam_re - nr * lam_im) / den
    bbar_re = fr[..., None] * b_re - fi[..., None] * b_im
    bbar_im = fr[..., None] * b_im + fi[..., None] * b_re
    eye = jnp.eye(gpb, dtype=F32)

    def bmat(bb):
        t = bb.reshape(nblk, gpb, P, gc)
        return jnp.einsum('jgpc,gh->jgchp', t, eye).reshape(nblk, gpb * gc, gpb * P)

    def cmat(cc):
        t = cc.reshape(nblk, gpb, gc, P)
        return jnp.einsum('jgcp,gh->jhpgc', t, eye).reshape(nblk, gpb * P, gpb * gc)

    b_blk = jnp.concatenate([bmat(bbar_re), bmat(bbar_im)], axis=2).astype(BF16)
    c_blk = jnp.concatenate([cmat(c_re), -cmat(c_im)], axis=1).astype(BF16)
    return (b_blk, c_blk, ar.reshape(nblk, 1, gpb * P), ai.reshape(nblk, 1, gpb * P))


def _s5_prompt_kernel(u_ref, bblk_ref, cblk_ref, ar_ref, ai_ref, d_ref,
                      g5_ref, g5b_ref, xre_ref, xim_ref, up, xs, ys):
    L = u_ref.shape[0]
    ns = ar_ref.shape[-1]
    nseg = S5_SEGMENTS
    seg = L // nseg

    def slab(t):
        return pl.ds(pl.multiple_of(t * nseg, nseg), nseg)

    def gather(t, carry):
        up[slab(t), :] = u_ref[pl.ds(t, nseg, stride=seg), :]
        return carry

    lax.fori_loop(0, seg, gather, 0, unroll=8)
    u = up[...]
    xs[...] = jnp.dot(u.astype(BF16), bblk_ref[...], preferred_element_type=F32)
    ar = jnp.broadcast_to(ar_ref[...], (nseg, ns))
    ai = jnp.broadcast_to(ai_ref[...], (nseg, ns))

    def step(x, t):
        xr, xi = x
        r = xs[slab(t), :]
        return (ar * xr - ai * xi + r[:, :ns], ar * xi + ai * xr + r[:, ns:])

    zero = jnp.zeros((nseg, ns), F32)
    er, ei = lax.fori_loop(0, seg, lambda t, x: step(x, t), (zero, zero), unroll=8)
    pr, pi = ar_ref[...], ai_ref[...]
    for _ in range(seg.bit_length() - 1):
        pr, pi = pr * pr - pi * pi, 2.0 * pr * pi
    assert seg == 1 << (seg.bit_length() - 1)
    cr = [jnp.zeros((1, ns), F32)]
    ci = [jnp.zeros((1, ns), F32)]
    for s in range(nseg - 1):
        cr.append(er[s:s + 1] + pr * cr[s] - pi * ci[s])
        ci.append(ei[s:s + 1] + pr * ci[s] + pi * cr[s])
    x0 = (jnp.concatenate(cr, axis=0), jnp.concatenate(ci, axis=0))

    def step_store(t, x):
        xr, xi = step(x, t)
        xs[slab(t), :] = jnp.concatenate([xr, xi], axis=1)
        return (xr, xi)

    xr, xi = lax.fori_loop(0, seg, step_store, x0, unroll=8)
    xre_ref[...] = xr[nseg - 1:nseg]
    xim_ref[...] = xi[nseg - 1:nseg]
    y = jnp.dot(xs[...].astype(BF16), cblk_ref[...], preferred_element_type=F32) + d_ref[...] * u
    ys[...] = _gelu_tanh(y)
    for s in range(nseg):
        g5 = ys[pl.ds(s, seg, stride=nseg), :]
        g5_ref[s * seg:(s + 1) * seg, :] = g5
        g5b_ref[s * seg:(s + 1) * seg, :] = g5.astype(BF16)


def _s5_prompt(rest, tables, d_skip, layer, batch, seq, n_ch):
    b_blk, c_blk, ar, ai = tables
    nblk, _, ns2 = b_blk.shape
    ns = ns2 // 2
    L = seq
    nbytes = (2 * L * LANES * 4 + 2 * L * LANES * 6 + 3 * L * ns2 * 4 + L * ns2 * 2
              + 8 * LANES * ns2 * 2)
    blkp = lambda r, c: pl.BlockSpec((None, r, c), lambda b, j: (j, 0, 0))
    return pl.pallas_call(
        _s5_prompt_kernel,
        out_shape=(jax.ShapeDtypeStruct((batch * L, n_ch), F32),
                   jax.ShapeDtypeStruct((batch * L, n_ch), BF16),
                   jax.ShapeDtypeStruct((batch, 1, nblk * ns), F32),
                   jax.ShapeDtypeStruct((batch, 1, nblk * ns), F32)),
        grid=(batch, nblk),
        in_specs=[pl.BlockSpec((L, LANES), lambda b, j: (b, j)),
                  blkp(LANES, ns2), blkp(ns2, LANES), blkp(1, ns), blkp(1, ns),
                  pl.BlockSpec((None, 1, LANES), lambda b, j: (layer, 0, j))],
        out_specs=(pl.BlockSpec((L, LANES), lambda b, j: (b, j)),
                   pl.BlockSpec((L, LANES), lambda b, j: (b, j)),
                   pl.BlockSpec((None, 1, ns), lambda b, j: (b, 0, j)),
                   pl.BlockSpec((None, 1, ns), lambda b, j: (b, 0, j))),
        scratch_shapes=[pltpu.VMEM((L, LANES), F32),
                        pltpu.VMEM((L, ns2), F32),
                        pltpu.VMEM((L, LANES), F32)],
        compiler_params=pltpu.CompilerParams(
            dimension_semantics=("parallel", "parallel"),
            vmem_limit_bytes=_vmem_limit(nbytes)),
        name="s5_prompt",
    )(rest, b_blk, c_blk, ar, ai, d_skip)


def _s5_sample_kernel(u_ref, bblk_ref, cblk_ref, ar_ref, ai_ref, d_ref, x0r_ref, x0i_ref,
                      g5_ref, g5b_ref, xre_ref, xim_ref):
    ns = ar_ref.shape[-1]
    u = u_ref[...]
    bu = jnp.dot(u.astype(BF16), bblk_ref[...], preferred_element_type=F32)
    ar, ai = ar_ref[...], ai_ref[...]
    x0r, x0i = x0r_ref[...], x0i_ref[...]
    xr = ar * x0r - ai * x0i + bu[:, :ns]
    xi = ar * x0i + ai * x0r + bu[:, ns:]
    xre_ref[...] = xr
    xim_ref[...] = xi
    x = jnp.concatenate([xr, xi], axis=1)
    y = jnp.dot(x.astype(BF16), cblk_ref[...], preferred_element_type=F32) + d_ref[...] * u
    g5 = _gelu_tanh(y)
    g5_ref[...] = g5
    g5b_ref[...] = g5.astype(BF16)


def _s5_sample(rest, row0, tables, d_skip, x0_re, x0_im, layer, n_ch):
    b_blk, c_blk, ar, ai = tables
    nblk, _, ns2 = b_blk.shape
    ns = ns2 // 2
    nb = x0_re.shape[1]
    assert row0 % nb == 0
    r0 = row0 // nb
    blkp = lambda r, c: pl.BlockSpec((None, r, c), lambda j: (j, 0, 0))
    nbytes = 16 * nb * ns2 * 4 + 8 * LANES * ns2 * 2
    return pl.pallas_call(
        _s5_sample_kernel,
        out_shape=(jax.ShapeDtypeStruct((nb, n_ch), F32),
                   jax.ShapeDtypeStruct((nb, n_ch), BF16),
                   jax.ShapeDtypeStruct((nb, nblk * ns), F32),
                   jax.ShapeDtypeStruct((nb, nblk * ns), F32)),
        grid=(nblk,),
        in_specs=[pl.BlockSpec((nb, LANES), lambda j: (r0, j)),
                  blkp(LANES, ns2), blkp(ns2, LANES), blkp(1, ns), blkp(1, ns),
                  pl.BlockSpec((None, 1, LANES), lambda j: (layer, 0, j)),
                  pl.BlockSpec((None, nb, ns), lambda j: (layer, 0, j)),
                  pl.BlockSpec((None, nb, ns), lambda j: (layer, 0, j))],
        out_specs=(pl.BlockSpec((nb, LANES), lambda j: (0, j)),
                   pl.BlockSpec((nb, LANES), lambda j: (0, j)),
                   pl.BlockSpec((nb, ns), lambda j: (0, j)),
                   pl.BlockSpec((nb, ns), lambda j: (0, j))),
        compiler_params=pltpu.CompilerParams(
            dimension_semantics=("parallel",), vmem_limit_bytes=_vmem_limit(nbytes)),
        name="s5_sample",
    )(rest, b_blk, c_blk, ar, ai, d_skip, x0_re, x0_im)


def kernel(x_prompt, x_sample, state_dn_conv, state_dn_ssm, state_s5_re, state_s5_im, norm1, w_in, dn_conv_w, dn_a_log, dn_dt_bias, dn_norm_w, w_br_dn, s5_lam_re, s5_lam_im, s5_log_dt, s5_b_re, s5_b_im, s5_c_re, s5_c_im, s5_d, w_glu, w_br_s5, w_out, norm2, w_ffn_gate, w_ffn_up, w_ffn_down, norm_f):
    batch, seq, d_model = x_prompt.shape
    nb, dec_seq, _ = x_sample.shape
    assert dec_seq == 1
    depth, _, n_heads, dk, dv = state_dn_ssm.shape
    assert dk == LANES and dv == LANES and seq % GDN_CHUNK == 0 and seq % S5_SEGMENTS == 0
    qk_dim = n_heads * dk
    conv_ch = dn_conv_w.shape[2]
    assert conv_ch == 3 * qk_dim
    n_ch = s5_d.shape[1]
    n_groups, n_state = s5_lam_re.shape[1:]
    ffn = w_ffn_gate.shape[2]
    mp = batch * seq
    m = mp + nb
    z_end = 4 * qk_dim
    rest0 = z_end + 2 * n_heads
    assert w_in.shape[2] == rest0 + n_ch + 2 * d_model and 2 * n_heads <= LANES

    x = (x_prompt.reshape(mp, d_model), x_sample.reshape(nb, d_model))

    w_ba = jnp.pad(w_in[:, :, z_end:rest0], ((0, 0), (0, 0), (0, LANES - 2 * n_heads)))
    w_rest = w_in[:, :, rest0:].astype(BF16)
    pad_heads = lambda a: jnp.pad(a, ((0, 0), (n_heads, LANES - 2 * n_heads)))[:, None, :]
    alog_pad = pad_heads(dn_a_log)
    dtb_pad = pad_heads(dn_dt_bias)
    conv_buf_t = jnp.swapaxes(state_dn_conv, 1, 2)
    x0_re = state_s5_re.reshape(depth, nb, n_groups * n_state)
    x0_im = state_s5_im.reshape(depth, nb, n_groups * n_state)

    tm = _pick_tile(m, 1664, 64)
    n_hist = state_dn_conv.shape[2]
    norm_w3 = dn_norm_w[:, None, :]
    d_skip3 = s5_d[:, None, :]
    outs = {k: [] for k in ("p_conv", "p_ssm", "p_re", "p_im", "s_conv", "s_ssm", "s_re", "s_im")}
    for l in range(depth):
        h = _rmsnorm(x, norm1[l][None, :], BF16)
        qkvz = _fused_matmul([h], [(0, w_in, l, 0)], [], _ep_identity, z_end, F32,
                             tm=tm, tn=512, name="in_qkvz")
        ba = _fused_matmul([h], [(0, w_ba, l, 0)], [], _ep_identity, LANES, F32,
                           tm=tm, tn=LANES, name="in_ba")
        rest = _fused_matmul([h], [(0, w_rest, l, 0)], [], _ep_identity, n_ch + 2 * d_model, F32,
                             tm=tm, tn=512, name="in_rest")

        o_p, ssm_p = _gdn_prompt(qkvz, ba, dn_conv_w, alog_pad, dtb_pad, norm_w3,
                                 l, batch, seq, n_heads, dk)
        o_s, ssm_s = _gdn_sample(qkvz, ba, mp, conv_buf_t, dn_conv_w, alog_pad, dtb_pad,
                                 norm_w3, state_dn_ssm, l, n_heads, dk)
        outs["p_conv"].append(
            qkvz[:mp].reshape(batch, seq, z_end)[:, seq - n_hist:, :conv_ch])
        outs["s_conv"].append(jnp.concatenate(
            [state_dn_conv[l][:, 1:], qkvz[mp:, None, :conv_ch]], axis=1))
        outs["p_ssm"].append(ssm_p)
        outs["s_ssm"].append(ssm_s)

        tables = _s5_tables(s5_lam_re[l], s5_lam_im[l], s5_log_dt[l], s5_b_re[l], s5_b_im[l],
                            s5_c_re[l], s5_c_im[l])
        g5_p, g5b_p, re_p, im_p = _s5_prompt(rest, tables, d_skip3, l, batch, seq, n_ch)
        g5_s, g5b_s, re_s, im_s = _s5_sample(rest, mp, tables, d_skip3, x0_re, x0_im, l, n_ch)
        outs["p_re"].append(re_p.reshape(batch, n_groups, n_state))
        outs["p_im"].append(im_p.reshape(batch, n_groups, n_state))
        outs["s_re"].append(re_s.reshape(nb, n_groups, n_state))
        outs["s_im"].append(im_s.reshape(nb, n_groups, n_state))
        g5g = _fused_matmul([(g5b_p, g5b_s)], [(0, w_glu, l, 0)], [((g5_p, g5_s), 0)],
                            _ep_glu_self, n_ch, BF16, tm=tm, tn=512, name="s5_glu")

        merged = _fused_matmul([(o_p, o_s), g5g], [(0, w_br_dn, l, 0), (1, w_br_s5, l, 0)],
                               [(rest, n_ch), (rest, n_ch + d_model)], _ep_gated_merge,
                               d_model, BF16, tm=tm, tn=512, name="branch_merge")
        x = _fused_matmul([merged], [(0, w_out, l, 0)], [(x, 0)], _ep_residual, d_model, F32,
                          tm=tm, tn=512, name="out_proj")

        h2 = _rmsnorm(x, norm2[l][None, :], BF16)
        hmid = _fused_matmul([h2], [(0, w_ffn_gate, l, 0), (0, w_ffn_up, l, 0)], [], _ep_swiglu,
                             ffn, BF16, tm=tm, tn=256, name="ffn_up")
        x = _matmul_residual_ktiled(hmid, w_ffn_down, l, x, tm=tm, tn=min(d_model, 1024),
                                    tk=_pick_tile(ffn, 512, LANES), name="ffn_down")

    y = _rmsnorm(x, norm_f[None, :], F32)
    st = lambda k: jnp.stack(outs[k])
    return (y[:mp].reshape(batch, seq, d_model), y[mp:].reshape(nb, 1, d_model),
            st("p_conv"), st("p_ssm"), st("p_re"), st("p_im"),
            st("s_conv"), st("s_ssm"), st("s_re"), st("s_im"))
```

```python
import functools
import math

import jax
import jax.numpy as jnp
from jax import lax
from jax.experimental import pallas as pl
from jax.experimental.pallas import tpu as pltpu

F32 = jnp.float32
BF16 = jnp.bfloat16

NORM_EPS = 1e-6
L2_EPS = 1e-6
LANES = 128
SUBLANES = 8
VMEM_CAP_BYTES = 56 * 1024 * 1024
GDN_CHUNK = 128
S5_SEGMENTS = SUBLANES
GDN_HEADS_PER_STEP = 2
GDN_CHUNKS_PER_GROUP = 16


def _vmem_limit(nbytes):
    return int(min(VMEM_CAP_BYTES, nbytes * 5 // 4 + (4 << 20)))


def _pick_tile(n, target, mult):
    best = None
    for t in range(mult, min(n, target) + 1, mult):
        if n % t == 0:
            best = t
    return best if best is not None else n


def _sigmoid(x):
    return 1.0 / (1.0 + jnp.exp(-x))


def _silu(x):
    return x * _sigmoid(x)


def _softplus(x):
    return jnp.maximum(x, 0.0) + jnp.log1p(jnp.exp(-jnp.abs(x)))


def _gelu_tanh(x):
    c = math.sqrt(2.0 / math.pi)
    return 0.5 * x * (1.0 + jnp.tanh(c * (x + 0.044715 * (x * x * x))))


def _bdot(a, b):
    return jnp.dot(a.astype(BF16), b.astype(BF16), preferred_element_type=F32)


def _bdot_nt(a, b):
    return lax.dot_general(a.astype(BF16), b.astype(BF16), (((1,), (1,)), ((), ())),
                           preferred_element_type=F32)


def _rows_of(op):
    return op[0].shape[0] + op[1].shape[0] if isinstance(op, tuple) else op.shape[0]


def _row_specs(op, tm, ncols, index_map):
    if not isinstance(op, tuple):
        return [pl.BlockSpec((tm, ncols), index_map)], [op], None
    p, s = op
    tail = p.shape[0] % tm
    assert tail + s.shape[0] == tm and tail % 16 == 0, (p.shape, s.shape, tm)

    def s_map(*idx):
        return (0,) + tuple(index_map(*idx)[1:])

    return ([pl.BlockSpec((tm, ncols), index_map), pl.BlockSpec((s.shape[0], ncols), s_map)],
            [p, s], tail)


def _load_rows(refs, tail, last):
    if tail is None or not last:
        return refs[0][...]
    return jnp.concatenate([refs[0][:tail, :], refs[1][...]], axis=0)


def _on_row_tiles(i, n_tiles, any_split, body):
    if not any_split:
        body(False)
        return
    if n_tiles > 1:
        pl.when(i < n_tiles - 1)(lambda: body(False))
    pl.when(i == n_tiles - 1)(lambda: body(True))


def _rmsnorm_kernel(*refs, tail, n_tiles, out_tail):
    n_out = 1 if out_tail is None else 2
    x_refs, w_ref, o_refs = refs[:-1 - n_out], refs[-1 - n_out], refs[-n_out:]

    def body(last):
        x = _load_rows(x_refs, tail, last)
        y = x * lax.rsqrt(jnp.mean(x * x, axis=-1, keepdims=True) + NORM_EPS)
        y = (y * w_ref[...]).astype(o_refs[0].dtype)
        if out_tail is None:
            o_refs[0][...] = y
        elif not last:
            o_refs[0][...] = y
        else:
            o_refs[0][:out_tail, :] = y[:out_tail]
            o_refs[1][...] = y[out_tail:]

    _on_row_tiles(pl.program_id(0), n_tiles, tail is not None or out_tail is not None, body)


def _rmsnorm(x, w_row, out_dtype, split_rows=None):
    m = _rows_of(x)
    d = w_row.shape[1]
    tr = _pick_tile(m, 832, 64)
    specs, arrs, tail = _row_specs(x, tr, d, lambda i: (i, 0))
    nbytes = 2 * tr * d * 4 + 2 * tr * d * jnp.dtype(out_dtype).itemsize + 3 * tr * d * 4
    if split_rows is None:
        out_shape = jax.ShapeDtypeStruct((m, d), out_dtype)
        out_specs = pl.BlockSpec((tr, d), lambda i: (i, 0))
        out_tail = None
    else:
        mp, nb = split_rows
        out_tail = mp % tr
        assert mp + nb == m and out_tail + nb == tr
        out_shape = (jax.ShapeDtypeStruct((mp, d), out_dtype),
                     jax.ShapeDtypeStruct((nb, d), out_dtype))
        out_specs = (pl.BlockSpec((tr, d), lambda i: (i, 0)),
                     pl.BlockSpec((nb, d), lambda i: (0, 0)))
    return pl.pallas_call(
        functools.partial(_rmsnorm_kernel, tail=tail, n_tiles=m // tr, out_tail=out_tail),
        out_shape=out_shape,
        grid=(m // tr,),
        in_specs=specs + [pl.BlockSpec((1, d), lambda i: (0, 0))],
        out_specs=out_specs,
        compiler_params=pltpu.CompilerParams(
            dimension_semantics=("arbitrary",), vmem_limit_bytes=_vmem_limit(nbytes)),
        name="rmsnorm",
    )(*arrs, w_row)


def _mm_kernel(*refs, a_idx, a_groups, e_groups, n_tiles, epilogue):
    pos = 0
    a_refs = []
    for n, _ in a_groups:
        a_refs.append(refs[pos:pos + n])
        pos += n
    w_refs = refs[pos:pos + len(a_idx)]
    pos += len(a_idx)
    e_refs = []
    for n, _ in e_groups:
        e_refs.append(refs[pos:pos + n])
        pos += n
    o_ref = refs[pos]
    any_split = any(t is not None for _, t in a_groups + e_groups)

    def body(last):
        a_vals = [_load_rows(r, t, last) for r, (_, t) in zip(a_refs, a_groups)]
        parts = [jnp.dot(a_vals[ai], w[...].astype(BF16), preferred_element_type=F32)
                 for ai, w in zip(a_idx, w_refs)]
        e_vals = [_load_rows(r, t, last) for r, (_, t) in zip(e_refs, e_groups)]
        o_ref[...] = epilogue(parts, e_vals).astype(o_ref.dtype)

    _on_row_tiles(pl.program_id(0), n_tiles, any_split, body)


def _fused_matmul(a_list, w_list, extras, epilogue, n_out, out_dtype, *, tm, tn, name,
                  col_params=()):
    m = _rows_of(a_list[0])
    assert m % tm == 0 and n_out % tn == 0
    in_specs, args, a_groups, e_groups = [], [], [], []
    kdims = []
    for a in a_list:
        kd = (a[0] if isinstance(a, tuple) else a).shape[1]
        specs, arrs, tail = _row_specs(a, tm, kd, lambda i, j: (i, 0))
        assert _rows_of(a) == m
        in_specs += specs
        args += arrs
        a_groups.append((len(arrs), tail))
        kdims.append(kd)
    for ai, w, layer, col0 in w_list:
        assert col0 % tn == 0 and w.shape[1] == kdims[ai]
        in_specs.append(pl.BlockSpec((None, w.shape[1], tn),
                                     lambda i, j, layer=layer, off=col0 // tn: (layer, 0, j + off)))
        args.append(w)
    for e, col0 in extras:
        assert col0 % tn == 0 and _rows_of(e) == m
        specs, arrs, tail = _row_specs(e, tm, tn, lambda i, j, off=col0 // tn: (i, j + off))
        in_specs += specs
        args += arrs
        e_groups.append((len(arrs), tail))
    for p, layer in col_params:
        in_specs.append(pl.BlockSpec((None, 1, tn), lambda i, j, layer=layer: (layer, 0, j)))
        args.append(p)
        e_groups.append((1, None))
    osz = jnp.dtype(out_dtype).itemsize
    nbytes = (sum(3 * tm * kd * 2 for kd in kdims)
              + sum(kdims[ai] * tn * (2 * w.dtype.itemsize + 2) for ai, w, _, _ in w_list)
              + sum(3 * tm * tn * 4 for _ in extras)
              + 2 * tm * tn * osz + (2 + len(w_list)) * tm * tn * 4)
    kern = functools.partial(_mm_kernel, a_idx=tuple(ai for ai, _, _, _ in w_list),
                             a_groups=tuple(a_groups), e_groups=tuple(e_groups),
                             n_tiles=m // tm, epilogue=epilogue)
    return pl.pallas_call(
        kern,
        out_shape=jax.ShapeDtypeStruct((m, n_out), out_dtype),
        grid=(m // tm, n_out // tn),
        in_specs=in_specs,
        out_specs=pl.BlockSpec((tm, tn), lambda i, j: (i, j)),
        compiler_params=pltpu.CompilerParams(
            dimension_semantics=("parallel", "parallel"),
            vmem_limit_bytes=_vmem_limit(nbytes)),
        name=name,
    )(*args)


def _mm_residual_wstat_kernel(a_ref, w_ref, x_ref, o_ref, wb):
    @pl.when(pl.program_id(1) == 0)
    def _():
        wb[...] = w_ref[...].astype(BF16)

    o_ref[...] = x_ref[...] + jnp.dot(a_ref[...], wb[...], preferred_element_type=F32)


def _matmul_residual_wstat(a, w, layer, x, *, tm, tn, name):
    m, kdim = a.shape
    n_out = w.shape[2]
    assert m % tm == 0 and n_out % tn == 0 and w.shape[1] == kdim
    nbytes = 2 * tm * kdim * 2 + kdim * tn * (2 * w.dtype.itemsize + 2) + 6 * tm * tn * 4
    return pl.pallas_call(
        _mm_residual_wstat_kernel,
        out_shape=jax.ShapeDtypeStruct((m, n_out), F32),
        grid=(n_out // tn, m // tm),
        in_specs=[pl.BlockSpec((tm, kdim), lambda j, i: (i, 0)),
                  pl.BlockSpec((None, kdim, tn), lambda j, i: (layer, 0, j)),
                  pl.BlockSpec((tm, tn), lambda j, i: (i, j))],
        out_specs=pl.BlockSpec((tm, tn), lambda j, i: (i, j)),
        scratch_shapes=[pltpu.VMEM((kdim, tn), BF16)],
        compiler_params=pltpu.CompilerParams(
            dimension_semantics=("parallel", "arbitrary"),
            vmem_limit_bytes=_vmem_limit(nbytes)),
        name=name,
    )(a, w, x)


def _ep_identity(accs, extras):
    return accs[0]


def _ep_residual(accs, extras):
    return extras[0] + accs[0]


def _ep_swiglu(accs, extras):
    return _silu(accs[0]) * accs[1]


def _ep_glu_self(accs, extras):
    g5 = extras[0]
    return g5 * _sigmoid(accs[0])


def _ep_gated_merge(accs, extras):
    return _sigmoid(extras[0]) * accs[0] + _sigmoid(extras[1]) * accs[1]


def _ep_beta_decay(n_heads):
    def ep(accs, extras):
        acc = accs[0]
        alog_row, dtb_row = extras
        lane = lax.broadcasted_iota(jnp.int32, acc.shape, 1)
        return jnp.where(lane < n_heads, _sigmoid(acc),
                         -jnp.exp(alog_row) * _softplus(acc + dtb_row))
    return ep


def _head_columns(bg, head, n_heads):
    lane = lax.broadcasted_iota(jnp.int32, bg.shape, 1)
    beta = jnp.sum(jnp.where(lane == head, bg, 0.0), axis=-1, keepdims=True)
    g = jnp.sum(jnp.where(lane == head + n_heads, bg, 0.0), axis=-1, keepdims=True)
    return beta, g


def _l2norm_rows(x):
    return x * lax.rsqrt(jnp.sum(x * x, axis=-1, keepdims=True) + L2_EPS)


def _gated_out_norm(o, z, nw_row):
    y = o * lax.rsqrt(jnp.mean(o * o, axis=-1, keepdims=True) + NORM_EPS)
    return y * nw_row * _silu(z)


def _gdn_prompt_kernel(q_ref, k_ref, v_ref, z_ref, bg_ref, cwq_ref, cwk_ref, cwv_ref, nw_ref,
                       o_ref, s_ref,
                       qn, kn, vn, gb, bb, us, ws, qks, qds, kdt, gl, osc, *, n_heads):
    hb, L, dk = qn.shape
    C = GDN_CHUNK
    n_chunks = L // C
    H8 = range(hb)

    row8 = lax.broadcasted_iota(jnp.int32, (SUBLANES, dk), 0)

    def conv_silu(x_ref, cw_ref, cols):
        cw = cw_ref[:, cols]
        n_taps = cw.shape[0]
        assert n_taps - 1 <= SUBLANES
        tap = lambda j: cw[n_taps - 1 - j:n_taps - j, :]
        head8 = x_ref[0:SUBLANES, cols]
        lo = head8 * tap(0)
        hi = x_ref[SUBLANES:L, cols] * tap(0)
        for j in range(1, n_taps):
            lo = lo + jnp.where(row8 >= j, pltpu.roll(head8, j, 0), 0.0) * tap(j)
            hi = hi + x_ref[pl.ds(SUBLANES - j, L - SUBLANES), cols] * tap(j)
        return _silu(jnp.concatenate([lo, hi], axis=0))

    bg = bg_ref[...]
    for hh in H8:
        cols = slice(hh * dk, (hh + 1) * dk)
        qn[hh] = _l2norm_rows(conv_silu(q_ref, cwq_ref, cols)) * (dk ** -0.5)
        kn[hh] = _l2norm_rows(conv_silu(k_ref, cwk_ref, cols))
        vn[hh] = conv_silu(v_ref, cwv_ref, cols)
        beta, g = _head_columns(bg, pl.program_id(1) * hb + hh, n_heads)
        bb[hh] = jnp.broadcast_to(beta, (L, dk))
        gb[hh] = jnp.broadcast_to(g, (L, dk))

    ri = lax.broadcasted_iota(jnp.int32, (C, C), 0)
    ci = lax.broadcasted_iota(jnp.int32, (C, C), 1)
    causal = ri >= ci
    strict = ri > ci
    tri_incl = jnp.where(causal, 1.0, 0.0).astype(F32)
    eye = jnp.where(ri == ci, 1.0, 0.0).astype(F32)
    level_masks = []
    n = 1
    while n < C:
        sh = n.bit_length() - 1
        same_2n = (ri >> (sh + 1)) == (ci >> (sh + 1))
        diff_n = (ri >> sh) != (ci >> sh)
        level_masks.append(jnp.where(same_2n & diff_n & strict, 1.0, 0.0).astype(F32))
        n *= 2

    group = math.gcd(n_chunks, GDN_CHUNKS_PER_GROUP)

    def intra_group(hh, i):
        G = range(group)
        rows = [pl.ds(pl.multiple_of((i * group + j) * C, C), C) for j in G]
        k = [kn[hh, r, :] for r in rows]
        bet = [bb[hh, r, :] for r in rows]
        gcb = [jnp.dot(tri_incl, gb[hh, r, :], precision=lax.Precision.HIGHEST,
                       preferred_element_type=F32) for r in rows]
        gamma = [jnp.where(causal, jnp.exp(jnp.minimum(g - g.T, 0.0)), 0.0) for g in gcb]
        kb = [k[j] * bet[j] for j in G]
        a_mat = [jnp.where(strict, _bdot_nt(kb[j], k[j]) * gamma[j], 0.0) for j in G]
        q = [qn[hh, r, :] for r in rows]
        for j in G:
            qks[hh, rows[j], :] = _bdot_nt(q[j], k[j]) * gamma[j]
        t = [eye - a * level_masks[0] for a in a_mat]
        for m in level_masks[1:]:
            x = [_bdot(a_mat[j] * m, t[j]) for j in G]
            t = [t[j] - _bdot(t[j], x[j]) for j in G]
        eg = [jnp.exp(g) for g in gcb]
        for j in G:
            us[hh, rows[j], :] = _bdot(t[j], vn[hh, rows[j], :] * bet[j])
        for j in G:
            ws[hh, rows[j], :] = _bdot(t[j], kb[j] * eg[j])
        for j in G:
            qds[hh, rows[j], :] = q[j] * eg[j]
            g_last = gcb[j][C - 1:C, :]
            kdt[hh, rows[j], :] = (k[j] * jnp.exp(g_last - gcb[j])).T
            gl[hh, pl.ds(pl.multiple_of((i * group + j) * SUBLANES, SUBLANES), SUBLANES), :] = (
                jnp.broadcast_to(jnp.exp(g_last), (SUBLANES, dk)))

    for hh in H8:
        def body(i, carry, hh=hh):
            intra_group(hh, i)
            return carry
        lax.fori_loop(0, n_chunks // group, body, 0)

    def inter(c, states):
        rows = pl.ds(pl.multiple_of(c * C, C), C)
        ws_s = [_bdot(ws[hh, rows, :], states[hh]) for hh in H8]
        qd_s = [_bdot(qds[hh, rows, :], states[hh]) for hh in H8]
        v_new = [us[hh, rows, :] - ws_s[hh] for hh in H8]
        for hh in H8:
            osc[hh, rows, :] = qd_s[hh] + _bdot(qks[hh, rows, :], v_new[hh])
        decay = [gl[hh, pl.ds(pl.multiple_of(c * SUBLANES, SUBLANES), 1), :] for hh in H8]
        return tuple(states[hh] * decay[hh] + _bdot(kdt[hh, rows, :], v_new[hh]) for hh in H8)

    s_fin = lax.fori_loop(0, n_chunks, inter, tuple(jnp.zeros((dk, dk), F32) for _ in H8))
    for hh in H8:
        cols = slice(hh * dk, (hh + 1) * dk)
        s_ref[hh] = s_fin[hh]
        o_ref[:, cols] = _gated_out_norm(osc[hh], z_ref[:, cols], nw_ref[...]).astype(o_ref.dtype)


def _gdn_prompt(qkvz, bg, conv_w, norm_w, layer, batch, seq, n_heads, dk):
    H = n_heads
    L = seq
    hb = GDN_HEADS_PER_STEP if H % GDN_HEADS_PER_STEP == 0 else 1
    hg = H // hb

    def col(sec):
        return pl.BlockSpec((L, hb * dk), lambda b, h, sec=sec: (b, h + sec * hg))

    def cw(sec):
        return pl.BlockSpec((None, conv_w.shape[1], hb * dk),
                            lambda b, h, sec=sec: (layer, 0, h + sec * hg))

    scr = lambda: pltpu.VMEM((hb, L, dk), F32)
    nbytes = hb * (2 * 4 * L * dk * 4 + 11 * L * dk * 4 + 2 * L * dk * 2) + 2 * L * LANES * 4 \
        + 60 * GDN_CHUNK * GDN_CHUNK * 4
    return pl.pallas_call(
        functools.partial(_gdn_prompt_kernel, n_heads=H),
        out_shape=(jax.ShapeDtypeStruct((batch * L, H * dk), BF16),
                   jax.ShapeDtypeStruct((batch, H, dk, dk), F32)),
        grid=(batch, hg),
        in_specs=[col(0), col(1), col(2), col(3),
                  pl.BlockSpec((L, LANES), lambda b, h: (b, 0)),
                  cw(0), cw(1), cw(2),
                  pl.BlockSpec((None, 1, dk), lambda b, h: (layer, 0, 0))],
        out_specs=(pl.BlockSpec((L, hb * dk), lambda b, h: (b, h)),
                   pl.BlockSpec((None, hb, dk, dk), lambda b, h: (b, h, 0, 0))),
        scratch_shapes=[scr(), scr(), scr(), scr(), scr(),
                        scr(), scr(), scr(), scr(), scr(),
                        pltpu.VMEM((hb, L // GDN_CHUNK * SUBLANES, dk), F32),
                        scr()],
        compiler_params=pltpu.CompilerParams(
            dimension_semantics=("parallel", "parallel"),
            vmem_limit_bytes=_vmem_limit(nbytes)),
        name="gdn_prompt",
    )(qkvz, qkvz, qkvz, qkvz, bg, conv_w, conv_w, conv_w, norm_w)


def _gdn_sample_kernel(q_ref, k_ref, v_ref, z_ref, bg_ref, bq_ref, bk_ref, bv_ref,
                       cwq_ref, cwk_ref, cwv_ref, nw_ref, s_in_ref, *rest, n_heads):
    o_ref, s_out_ref, osc = rest[-3:]
    nb, dk = q_ref.shape
    head = pl.program_id(0)

    def conv_silu(x_ref, buf_ref, cw_ref):
        cw = cw_ref[...]
        n_taps = cw.shape[0]
        y = x_ref[...] * cw[n_taps - 1:n_taps, :]
        for i in range(n_taps - 1):
            y = y + buf_ref[i] * cw[i:i + 1, :]
        return _silu(y)

    q = _l2norm_rows(conv_silu(q_ref, bq_ref, cwq_ref)) * (dk ** -0.5)
    k = _l2norm_rows(conv_silu(k_ref, bk_ref, cwk_ref))
    v = conv_silu(v_ref, bv_ref, cwv_ref)
    beta, g = _head_columns(bg_ref[...], head, n_heads)
    decay = jnp.exp(g)
    kt = jnp.concatenate([k, jnp.zeros((LANES - nb, dk), F32)], axis=0).T if nb < LANES else k.T
    qt = jnp.concatenate([q, jnp.zeros((LANES - nb, dk), F32)], axis=0).T if nb < LANES else q.T
    for b in range(nb):
        s = s_in_ref[b] * decay[b:b + 1, :]
        kcol = kt[:, b:b + 1]
        v_new = (v[b:b + 1, :] - jnp.sum(s * kcol, axis=0, keepdims=True)) * beta[b:b + 1, :]
        s = s + kcol * v_new
        s_out_ref[b] = s
        osc[b:b + 1, :] = jnp.sum(s * qt[:, b:b + 1], axis=0, keepdims=True)
    o_ref[...] = _gated_out_norm(osc[...], z_ref[...], nw_ref[...]).astype(o_ref.dtype)


def _gdn_sample(qkvz, bg, row0, conv_buf_t, conv_w, norm_w, state, layer, n_heads, dk,
                state_out=None):
    H = n_heads
    nb_total = state.shape[1]
    nb = 16
    assert nb_total % nb == 0 and row0 % nb == 0
    r0 = row0 // nb
    n_hist = conv_buf_t.shape[1]

    def col(off):
        return pl.BlockSpec((nb, dk), lambda h, i, off=off: (i + r0, h + off))

    def buf(off):
        return pl.BlockSpec((None, n_hist, nb, dk), lambda h, i, off=off: (layer, 0, i, h + off))

    def cw(off):
        return pl.BlockSpec((None, conv_w.shape[1], dk), lambda h, i, off=off: (layer, 0, h + off))

    in_specs = [col(0), col(H), col(2 * H), col(3 * H),
                pl.BlockSpec((nb, LANES), lambda h, i: (i + r0, 0)),
                buf(0), buf(H), buf(2 * H),
                cw(0), cw(H), cw(2 * H),
                pl.BlockSpec((None, 1, dk), lambda h, i: (layer, 0, 0)),
                pl.BlockSpec((None, nb, None, dk, dk), lambda h, i: (layer, i, h, 0, 0))]
    args = [qkvz, qkvz, qkvz, qkvz, bg, conv_buf_t, conv_buf_t, conv_buf_t,
            conv_w, conv_w, conv_w, norm_w, state]
    aliases = {}
    if state_out is not None:
        in_specs.append(pl.BlockSpec(memory_space=pl.ANY))
        args.append(state_out)
        aliases = {len(args) - 1: 1}
    nbytes = 4 * nb * dk * dk * 4 + 64 * nb * dk * 4 + 64 * dk * dk * 4
    return pl.pallas_call(
        functools.partial(_gdn_sample_kernel, n_heads=H),
        out_shape=(jax.ShapeDtypeStruct((nb_total, H * dk), BF16),
                   jax.ShapeDtypeStruct(state.shape, F32)),
        grid=(H, nb_total // nb),
        in_specs=in_specs,
        out_specs=(pl.BlockSpec((nb, dk), lambda h, i: (i, h)),
                   pl.BlockSpec((None, nb, None, dk, dk), lambda h, i: (layer, i, h, 0, 0))),
        scratch_shapes=[pltpu.VMEM((nb, dk), F32)],
        input_output_aliases=aliases,
        compiler_params=pltpu.CompilerParams(
            dimension_semantics=("parallel", "parallel"),
            vmem_limit_bytes=_vmem_limit(nbytes)),
        name="gdn_sample",
    )(*args)


def _s5_tables(lam_re, lam_im, log_dt, b_re, b_im, c_re, c_im):
    G, P = lam_re.shape
    gc = b_re.shape[-1]
    gpb = LANES // gc
    nblk = G // gpb
    dt = jnp.exp(log_dt)[:, None]
    mag = jnp.exp(lam_re * dt)
    ar = mag * jnp.cos(lam_im * dt)
    ai = mag * jnp.sin(lam_im * dt)
    nr = ar - 1.0
    den = lam_re * lam_re + lam_im * lam_im
    fr = (nr * lam_re + ai * lam_im) / den
    fi = (ai * lam_re - nr * lam_im) / den
    bbar_re = fr[..., None] * b_re - fi[..., None] * b_im
    bbar_im = fr[..., None] * b_im + fi[..., None] * b_re
    eye = jnp.eye(gpb, dtype=F32)

    def bmat(bb):
        t = bb.reshape(nblk, gpb, P, gc)
        return jnp.einsum('jgpc,gh->jgchp', t, eye).reshape(nblk, gpb * gc, gpb * P)

    def cmat(cc):
        t = cc.reshape(nblk, gpb, gc, P)
        return jnp.einsum('jgcp,gh->jhpgc', t, eye).reshape(nblk, gpb * P, gpb * gc)

    b_blk = jnp.concatenate([bmat(bbar_re), bmat(bbar_im)], axis=2).astype(BF16)
    c_blk = jnp.concatenate([cmat(c_re), -cmat(c_im)], axis=1).astype(BF16)
    return (b_blk, c_blk, ar.reshape(nblk, 1, gpb * P), ai.reshape(nblk, 1, gpb * P))


def _s5_prompt_kernel(u_ref, bblk_ref, cblk_ref, ar_ref, ai_ref, d_ref,
                      g5_ref, g5b_ref, xre_ref, xim_ref, up, xs, ys):
    L = u_ref.shape[0]
    ns = ar_ref.shape[-1]
    nseg = S5_SEGMENTS
    seg = L // nseg

    def slab(t):
        return pl.ds(pl.multiple_of(t * nseg, nseg), nseg)

    def gather(t, carry):
        up[slab(t), :] = u_ref[pl.ds(t, nseg, stride=seg), :]
        return carry

    lax.fori_loop(0, seg, gather, 0, unroll=8)
    u = up[...]
    xs[...] = jnp.dot(u.astype(BF16), bblk_ref[...], preferred_element_type=F32)
    ar = jnp.broadcast_to(ar_ref[...], (nseg, ns))
    ai = jnp.broadcast_to(ai_ref[...], (nseg, ns))

    def step(x, t):
        xr, xi = x
        r = xs[slab(t), :]
        return (ar * xr - ai * xi + r[:, :ns], ar * xi + ai * xr + r[:, ns:])

    zero = jnp.zeros((nseg, ns), F32)
    er, ei = lax.fori_loop(0, seg, lambda t, x: step(x, t), (zero, zero), unroll=8)
    pr, pi = ar_ref[...], ai_ref[...]
    for _ in range(seg.bit_length() - 1):
        pr, pi = pr * pr - pi * pi, 2.0 * pr * pi
    assert seg == 1 << (seg.bit_length() - 1)
    cr = [jnp.zeros((1, ns), F32)]
    ci = [jnp.zeros((1, ns), F32)]
    for s in range(nseg - 1):
        cr.append(er[s:s + 1] + pr * cr[s] - pi * ci[s])
        ci.append(ei[s:s + 1] + pr * ci[s] + pi * cr[s])
    x0 = (jnp.concatenate(cr, axis=0), jnp.concatenate(ci, axis=0))

    def step_store(t, x):
        xr, xi = step(x, t)
        xs[slab(t), :] = jnp.concatenate([xr, xi], axis=1)
        return (xr, xi)

    xr, xi = lax.fori_loop(0, seg, step_store, x0, unroll=8)
    xre_ref[...] = xr[nseg - 1:nseg]
    xim_ref[...] = xi[nseg - 1:nseg]
    y = jnp.dot(xs[...].astype(BF16), cblk_ref[...], preferred_element_type=F32) + d_ref[...] * u
    ys[...] = _gelu_tanh(y)
    for s in range(nseg):
        g5 = ys[pl.ds(s, seg, stride=nseg), :]
        g5_ref[s * seg:(s + 1) * seg, :] = g5
        g5b_ref[s * seg:(s + 1) * seg, :] = g5.astype(BF16)


def _s5_prompt(rest, tables, d_skip, layer, batch, seq, n_ch):
    b_blk, c_blk, ar, ai = tables
    nblk, _, ns2 = b_blk.shape
    ns = ns2 // 2
    L = seq
    nbytes = (2 * L * LANES * 4 + 2 * L * LANES * 6 + 3 * L * ns2 * 4 + L * ns2 * 2
              + 8 * LANES * ns2 * 2)
    blkp = lambda r, c: pl.BlockSpec((None, r, c), lambda b, j: (j, 0, 0))
    return pl.pallas_call(
        _s5_prompt_kernel,
        out_shape=(jax.ShapeDtypeStruct((batch * L, n_ch), F32),
                   jax.ShapeDtypeStruct((batch * L, n_ch), BF16),
                   jax.ShapeDtypeStruct((batch, 1, nblk * ns), F32),
                   jax.ShapeDtypeStruct((batch, 1, nblk * ns), F32)),
        grid=(batch, nblk),
        in_specs=[pl.BlockSpec((L, LANES), lambda b, j: (b, j)),
                  blkp(LANES, ns2), blkp(ns2, LANES), blkp(1, ns), blkp(1, ns),
                  pl.BlockSpec((None, 1, LANES), lambda b, j: (layer, 0, j))],
        out_specs=(pl.BlockSpec((L, LANES), lambda b, j: (b, j)),
                   pl.BlockSpec((L, LANES), lambda b, j: (b, j)),
                   pl.BlockSpec((None, 1, ns), lambda b, j: (b, 0, j)),
                   pl.BlockSpec((None, 1, ns), lambda b, j: (b, 0, j))),
        scratch_shapes=[pltpu.VMEM((L, LANES), F32),
                        pltpu.VMEM((L, ns2), F32),
                        pltpu.VMEM((L, LANES), F32)],
        compiler_params=pltpu.CompilerParams(
            dimension_semantics=("parallel", "parallel"),
            vmem_limit_bytes=_vmem_limit(nbytes)),
        name="s5_prompt",
    )(rest, b_blk, c_blk, ar, ai, d_skip)


def _s5_sample_kernel(u_ref, bblk_ref, cblk_ref, ar_ref, ai_ref, d_ref, x0r_ref, x0i_ref,
                      g5_ref, g5b_ref, xre_ref, xim_ref):
    ns = ar_ref.shape[-1]
    u = u_ref[...]
    bu = jnp.dot(u.astype(BF16), bblk_ref[...], preferred_element_type=F32)
    ar, ai = ar_ref[...], ai_ref[...]
    x0r, x0i = x0r_ref[...], x0i_ref[...]
    xr = ar * x0r - ai * x0i + bu[:, :ns]
    xi = ar * x0i + ai * x0r + bu[:, ns:]
    xre_ref[...] = xr
    xim_ref[...] = xi
    x = jnp.concatenate([xr, xi], axis=1)
    y = jnp.dot(x.astype(BF16), cblk_ref[...], preferred_element_type=F32) + d_ref[...] * u
    g5 = _gelu_tanh(y)
    g5_ref[...] = g5
    g5b_ref[...] = g5.astype(BF16)


def _s5_sample(rest, row0, tables, d_skip, x0_re, x0_im, layer, n_ch):
    b_blk, c_blk, ar, ai = tables
    nblk, _, ns2 = b_blk.shape
    ns = ns2 // 2
    nb = x0_re.shape[1]
    assert row0 % nb == 0
    r0 = row0 // nb
    blkp = lambda r, c: pl.BlockSpec((None, r, c), lambda j: (j, 0, 0))
    nbytes = 16 * nb * ns2 * 4 + 8 * LANES * ns2 * 2
    return pl.pallas_call(
        _s5_sample_kernel,
        out_shape=(jax.ShapeDtypeStruct((nb, n_ch), F32),
                   jax.ShapeDtypeStruct((nb, n_ch), BF16),
                   jax.ShapeDtypeStruct((nb, nblk * ns), F32),
                   jax.ShapeDtypeStruct((nb, nblk * ns), F32)),
        grid=(nblk,),
        in_specs=[pl.BlockSpec((nb, LANES), lambda j: (r0, j)),
                  blkp(LANES, ns2), blkp(ns2, LANES), blkp(1, ns), blkp(1, ns),
                  pl.BlockSpec((None, 1, LANES), lambda j: (layer, 0, j)),
                  pl.BlockSpec((None, nb, ns), lambda j: (layer, 0, j)),
                  pl.BlockSpec((None, nb, ns), lambda j: (layer, 0, j))],
        out_specs=(pl.BlockSpec((nb, LANES), lambda j: (0, j)),
                   pl.BlockSpec((nb, LANES), lambda j: (0, j)),
                   pl.BlockSpec((nb, ns), lambda j: (0, j)),
                   pl.BlockSpec((nb, ns), lambda j: (0, j))),
        compiler_params=pltpu.CompilerParams(
            dimension_semantics=("parallel",), vmem_limit_bytes=_vmem_limit(nbytes)),
        name="s5_sample",
    )(rest, b_blk, c_blk, ar, ai, d_skip, x0_re, x0_im)


def kernel(x_prompt, x_sample, state_dn_conv, state_dn_ssm, state_s5_re, state_s5_im, norm1, w_in, dn_conv_w, dn_a_log, dn_dt_bias, dn_norm_w, w_br_dn, s5_lam_re, s5_lam_im, s5_log_dt, s5_b_re, s5_b_im, s5_c_re, s5_c_im, s5_d, w_glu, w_br_s5, w_out, norm2, w_ffn_gate, w_ffn_up, w_ffn_down, norm_f):
    batch, seq, d_model = x_prompt.shape
    nb, dec_seq, _ = x_sample.shape
    assert dec_seq == 1
    depth, _, n_heads, dk, dv = state_dn_ssm.shape
    assert dk == LANES and dv == LANES and seq % GDN_CHUNK == 0 and seq % S5_SEGMENTS == 0
    qk_dim = n_heads * dk
    conv_ch = dn_conv_w.shape[2]
    assert conv_ch == 3 * qk_dim
    n_ch = s5_d.shape[1]
    n_groups, n_state = s5_lam_re.shape[1:]
    ffn = w_ffn_gate.shape[2]
    mp = batch * seq
    m = mp + nb
    z_end = 4 * qk_dim
    rest0 = z_end + 2 * n_heads
    assert w_in.shape[2] == rest0 + n_ch + 2 * d_model and 2 * n_heads <= LANES

    x = (x_prompt.reshape(mp, d_model), x_sample.reshape(nb, d_model))

    w_bg = jnp.pad(w_in[:, :, z_end:rest0], ((0, 0), (0, 0), (0, LANES - 2 * n_heads)))
    w_rest = w_in[:, :, rest0:].astype(BF16)
    pad_heads = lambda a: jnp.pad(a, ((0, 0), (n_heads, LANES - 2 * n_heads)))[:, None, :]
    alog_pad = pad_heads(dn_a_log)
    dtb_pad = pad_heads(dn_dt_bias)
    conv_buf_t = jnp.swapaxes(state_dn_conv, 1, 2)
    x0_re = state_s5_re.reshape(depth, nb, n_groups * n_state)
    x0_im = state_s5_im.reshape(depth, nb, n_groups * n_state)

    tm = _pick_tile(m, 1664, 64)
    n_hist = state_dn_conv.shape[2]
    norm_w3 = dn_norm_w[:, None, :]
    d_skip3 = s5_d[:, None, :]
    outs = {k: [] for k in ("p_conv", "p_ssm", "p_re", "p_im", "s_conv", "s_re", "s_im")}
    s_ssm = None
    for l in range(depth):
        h = _rmsnorm(x, norm1[l][None, :], BF16)
        qkvz = _fused_matmul([h], [(0, w_in, l, 0)], [], _ep_identity, z_end, F32,
                             tm=tm, tn=512, name="in_qkvz")
        bg = _fused_matmul([h], [(0, w_bg, l, 0)], [], _ep_beta_decay(n_heads), LANES, F32,
                           tm=tm, tn=LANES, name="in_bg",
                           col_params=[(alog_pad, l), (dtb_pad, l)])
        rest = _fused_matmul([h], [(0, w_rest, l, 0)], [], _ep_identity, n_ch + 2 * d_model, F32,
                             tm=tm, tn=512, name="in_rest")

        o_p, ssm_p = _gdn_prompt(qkvz, bg, dn_conv_w, norm_w3, l, batch, seq, n_heads, dk)
        o_s, s_ssm = _gdn_sample(qkvz, bg, mp, conv_buf_t, dn_conv_w, norm_w3, state_dn_ssm, l,
                                 n_heads, dk, state_out=s_ssm)
        outs["p_conv"].append(jnp.stack(
            [qkvz[(b + 1) * seq - n_hist:(b + 1) * seq, :conv_ch] for b in range(batch)]))
        outs["s_conv"].append(jnp.concatenate(
            [state_dn_conv[l][:, 1:], qkvz[mp:, None, :conv_ch]], axis=1))
        outs["p_ssm"].append(ssm_p)

        tables = _s5_tables(s5_lam_re[l], s5_lam_im[l], s5_log_dt[l], s5_b_re[l], s5_b_im[l],
                            s5_c_re[l], s5_c_im[l])
        g5_p, g5b_p, re_p, im_p = _s5_prompt(rest, tables, d_skip3, l, batch, seq, n_ch)
        g5_s, g5b_s, re_s, im_s = _s5_sample(rest, mp, tables, d_skip3, x0_re, x0_im, l, n_ch)
        outs["p_re"].append(re_p.reshape(batch, n_groups, n_state))
        outs["p_im"].append(im_p.reshape(batch, n_groups, n_state))
        outs["s_re"].append(re_s.reshape(nb, n_groups, n_state))
        outs["s_im"].append(im_s.reshape(nb, n_groups, n_state))
        g5g = _fused_matmul([(g5b_p, g5b_s)], [(0, w_glu, l, 0)], [((g5_p, g5_s), 0)],
                            _ep_glu_self, n_ch, BF16, tm=tm, tn=512, name="s5_glu")

        merged = _fused_matmul([(o_p, o_s), g5g], [(0, w_br_dn, l, 0), (1, w_br_s5, l, 0)],
                               [(rest, n_ch), (rest, n_ch + d_model)], _ep_gated_merge,
                               d_model, BF16, tm=tm, tn=512, name="branch_merge")
        x = _fused_matmul([merged], [(0, w_out, l, 0)], [(x, 0)], _ep_residual, d_model, F32,
                          tm=tm, tn=512, name="out_proj")

        h2 = _rmsnorm(x, norm2[l][None, :], BF16)
        hmid = _fused_matmul([h2], [(0, w_ffn_gate, l, 0), (0, w_ffn_up, l, 0)], [], _ep_swiglu,
                             ffn, BF16, tm=tm, tn=256, name="ffn_up")
        x = _matmul_residual_wstat(hmid, w_ffn_down, l, x, tm=_pick_tile(m, 416, 16), tn=512,
                                   name="ffn_down")

    y_p, y_s = _rmsnorm(x, norm_f[None, :], F32, split_rows=(mp, nb))
    st = lambda k: jnp.stack(outs[k])
    return (y_p.reshape(batch, seq, d_model), y_s.reshape(nb, 1, d_model),
            st("p_conv"), st("p_ssm"), st("p_re"), st("p_im"),
            st("s_conv"), s_ssm, st("s_re"), st("s_im"))
```

```python
import functools
import math

import jax
import jax.numpy as jnp
from jax import lax
from jax.experimental import pallas as pl
from jax.experimental.pallas import tpu as pltpu

F32 = jnp.float32
BF16 = jnp.bfloat16

NORM_EPS = 1e-6
L2_EPS = 1e-6
LANES = 128
SUBLANES = 8
VMEM_CAP_BYTES = 56 * 1024 * 1024
GDN_CHUNK = 128
S5_SEGMENTS = SUBLANES
GDN_HEADS_PER_STEP = 2
GDN_CHUNKS_PER_GROUP = 16


def _vmem_limit(nbytes):
    return int(min(VMEM_CAP_BYTES, nbytes * 5 // 4 + (4 << 20)))


def _pick_tile(n, target, mult):
    best = None
    for t in range(mult, min(n, target) + 1, mult):
        if n % t == 0:
            best = t
    return best if best is not None else n


def _sigmoid(x):
    return 1.0 / (1.0 + jnp.exp(-x))


def _silu(x):
    return x * _sigmoid(x)


def _softplus(x):
    return jnp.maximum(x, 0.0) + jnp.log1p(jnp.exp(-jnp.abs(x)))


def _gelu_tanh(x):
    c = math.sqrt(2.0 / math.pi)
    return 0.5 * x * (1.0 + jnp.tanh(c * (x + 0.044715 * (x * x * x))))


def _bdot(a, b):
    return jnp.dot(a.astype(BF16), b.astype(BF16), preferred_element_type=F32)


def _bdot_nt(a, b):
    return lax.dot_general(a.astype(BF16), b.astype(BF16), (((1,), (1,)), ((), ())),
                           preferred_element_type=F32)


def _rows_of(op):
    return op[0].shape[0] + op[1].shape[0] if isinstance(op, tuple) else op.shape[0]


def _row_specs(op, tm, ncols, index_map):
    if not isinstance(op, tuple):
        return [pl.BlockSpec((tm, ncols), index_map)], [op], None
    p, s = op
    tail = p.shape[0] % tm
    assert tail + s.shape[0] == tm and tail % 16 == 0, (p.shape, s.shape, tm)

    def s_map(*idx):
        return (0,) + tuple(index_map(*idx)[1:])

    return ([pl.BlockSpec((tm, ncols), index_map), pl.BlockSpec((s.shape[0], ncols), s_map)],
            [p, s], tail)


def _load_rows(refs, tail, last):
    if tail is None or not last:
        return refs[0][...]
    return jnp.concatenate([refs[0][:tail, :], refs[1][...]], axis=0)


def _lane_window(parts, shift, width):
    x = parts[0] if len(parts) == 1 else jnp.concatenate(parts, axis=1)
    if shift:
        x = pltpu.roll(x, x.shape[1] - shift, 1)
    return x[:, :width]


def _on_row_tiles(i, n_tiles, any_split, body):
    if not any_split:
        body(False)
        return
    if n_tiles > 1:
        pl.when(i < n_tiles - 1)(lambda: body(False))
    pl.when(i == n_tiles - 1)(lambda: body(True))


def _rmsnorm_kernel(*refs, tail, n_tiles, out_tail):
    n_out = 1 if out_tail is None else 2
    x_refs, w_ref, o_refs = refs[:-1 - n_out], refs[-1 - n_out], refs[-n_out:]

    def body(last):
        x = _load_rows(x_refs, tail, last)
        y = x * lax.rsqrt(jnp.mean(x * x, axis=-1, keepdims=True) + NORM_EPS)
        y = (y * w_ref[...]).astype(o_refs[0].dtype)
        if out_tail is None:
            o_refs[0][...] = y
        elif not last:
            o_refs[0][...] = y
        else:
            o_refs[0][:out_tail, :] = y[:out_tail]
            o_refs[1][...] = y[out_tail:]

    _on_row_tiles(pl.program_id(0), n_tiles, tail is not None or out_tail is not None, body)


def _rmsnorm(x, w_row, out_dtype, split_rows=None):
    m = _rows_of(x)
    d = w_row.shape[1]
    tr = _pick_tile(m, 832, 64)
    specs, arrs, tail = _row_specs(x, tr, d, lambda i: (i, 0))
    nbytes = 2 * tr * d * 4 + 2 * tr * d * jnp.dtype(out_dtype).itemsize + 3 * tr * d * 4
    if split_rows is None:
        out_shape = jax.ShapeDtypeStruct((m, d), out_dtype)
        out_specs = pl.BlockSpec((tr, d), lambda i: (i, 0))
        out_tail = None
    else:
        mp, nb = split_rows
        out_tail = mp % tr
        assert mp + nb == m and out_tail + nb == tr
        out_shape = (jax.ShapeDtypeStruct((mp, d), out_dtype),
                     jax.ShapeDtypeStruct((nb, d), out_dtype))
        out_specs = (pl.BlockSpec((tr, d), lambda i: (i, 0)),
                     pl.BlockSpec((nb, d), lambda i: (0, 0)))
    return pl.pallas_call(
        functools.partial(_rmsnorm_kernel, tail=tail, n_tiles=m // tr, out_tail=out_tail),
        out_shape=out_shape,
        grid=(m // tr,),
        in_specs=specs + [pl.BlockSpec((1, d), lambda i: (0, 0))],
        out_specs=out_specs,
        compiler_params=pltpu.CompilerParams(
            dimension_semantics=("arbitrary",), vmem_limit_bytes=_vmem_limit(nbytes)),
        name="rmsnorm",
    )(*arrs, w_row)


def _mm_kernel(*refs, a_idx, a_groups, e_groups, n_tiles, epilogue):
    pos = 0
    a_refs = []
    for n, _ in a_groups:
        a_refs.append(refs[pos:pos + n])
        pos += n
    w_refs = refs[pos:pos + len(a_idx)]
    pos += len(a_idx)
    e_refs = []
    for n, _, _ in e_groups:
        e_refs.append(refs[pos:pos + n])
        pos += n
    o_ref = refs[pos]
    any_split = any(g[1] is not None for g in a_groups + e_groups)

    def load_extra(r, tail, shift, last):
        if shift:
            return _lane_window([r[0][...], r[1][...]], shift, o_ref.shape[1])
        return _load_rows(r, tail, last)

    def body(last):
        a_vals = [_load_rows(r, t, last) for r, (_, t) in zip(a_refs, a_groups)]
        parts = [jnp.dot(a_vals[ai], w[...].astype(BF16), preferred_element_type=F32)
                 for ai, w in zip(a_idx, w_refs)]
        e_vals = [load_extra(r, t, sh, last) for r, (_, t, sh) in zip(e_refs, e_groups)]
        o_ref[...] = epilogue(parts, e_vals).astype(o_ref.dtype)

    _on_row_tiles(pl.program_id(0), n_tiles, any_split, body)


def _fused_matmul(a_list, w_list, extras, epilogue, n_out, out_dtype, *, tm, tn, name,
                  col_params=()):
    m = _rows_of(a_list[0])
    assert m % tm == 0 and n_out % tn == 0
    in_specs, args, a_groups, e_groups = [], [], [], []
    kdims = []
    for a in a_list:
        kd = (a[0] if isinstance(a, tuple) else a).shape[1]
        specs, arrs, tail = _row_specs(a, tm, kd, lambda i, j: (i, 0))
        assert _rows_of(a) == m
        in_specs += specs
        args += arrs
        a_groups.append((len(arrs), tail))
        kdims.append(kd)
    for ai, w, layer, col0 in w_list:
        assert col0 % tn == 0 and w.shape[1] == kdims[ai]
        in_specs.append(pl.BlockSpec((None, w.shape[1], tn),
                                     lambda i, j, layer=layer, off=col0 // tn: (layer, 0, j + off)))
        args.append(w)
    for e, col0 in extras:
        shift = col0 % tn
        assert shift < LANES and _rows_of(e) == m
        specs, arrs, tail = _row_specs(e, tm, tn, lambda i, j, off=col0 // tn: (i, j + off))
        if shift:
            assert not isinstance(e, tuple) and tn % LANES == 0
            specs.append(pl.BlockSpec(
                (tm, LANES), lambda i, j, off=col0 // tn: (i, (j + off + 1) * (tn // LANES))))
            arrs.append(e)
        in_specs += specs
        args += arrs
        e_groups.append((len(arrs), tail, shift))
    for p, layer in col_params:
        in_specs.append(pl.BlockSpec((None, 1, tn), lambda i, j, layer=layer: (layer, 0, j)))
        args.append(p)
        e_groups.append((1, None, 0))
    osz = jnp.dtype(out_dtype).itemsize
    nbytes = (sum(3 * tm * kd * 2 for kd in kdims)
              + sum(kdims[ai] * tn * (2 * w.dtype.itemsize + 2) for ai, w, _, _ in w_list)
              + sum(3 * tm * tn * 4 for _ in extras)
              + 2 * tm * tn * osz + (2 + len(w_list)) * tm * tn * 4)
    kern = functools.partial(_mm_kernel, a_idx=tuple(ai for ai, _, _, _ in w_list),
                             a_groups=tuple(a_groups), e_groups=tuple(e_groups),
                             n_tiles=m // tm, epilogue=epilogue)
    return pl.pallas_call(
        kern,
        out_shape=jax.ShapeDtypeStruct((m, n_out), out_dtype),
        grid=(m // tm, n_out // tn),
        in_specs=in_specs,
        out_specs=pl.BlockSpec((tm, tn), lambda i, j: (i, j)),
        compiler_params=pltpu.CompilerParams(
            dimension_semantics=("parallel", "parallel"),
            vmem_limit_bytes=_vmem_limit(nbytes)),
        name=name,
    )(*args)


def _mm_residual_wstat_kernel(a_ref, w_ref, x_ref, o_ref, wb):
    @pl.when(pl.program_id(1) == 0)
    def _():
        wb[...] = w_ref[...].astype(BF16)

    o_ref[...] = x_ref[...] + jnp.dot(a_ref[...], wb[...], preferred_element_type=F32)


def _matmul_residual_wstat(a, w, layer, x, *, tm, tn, name):
    m, kdim = a.shape
    n_out = w.shape[2]
    assert m % tm == 0 and n_out % tn == 0 and w.shape[1] == kdim
    nbytes = 2 * tm * kdim * 2 + kdim * tn * (2 * w.dtype.itemsize + 2) + 6 * tm * tn * 4
    return pl.pallas_call(
        _mm_residual_wstat_kernel,
        out_shape=jax.ShapeDtypeStruct((m, n_out), F32),
        grid=(n_out // tn, m // tm),
        in_specs=[pl.BlockSpec((tm, kdim), lambda j, i: (i, 0)),
                  pl.BlockSpec((None, kdim, tn), lambda j, i: (layer, 0, j)),
                  pl.BlockSpec((tm, tn), lambda j, i: (i, j))],
        out_specs=pl.BlockSpec((tm, tn), lambda j, i: (i, j)),
        scratch_shapes=[pltpu.VMEM((kdim, tn), BF16)],
        compiler_params=pltpu.CompilerParams(
            dimension_semantics=("parallel", "arbitrary"),
            vmem_limit_bytes=_vmem_limit(nbytes)),
        name=name,
    )(a, w, x)


def _ep_identity(accs, extras):
    return accs[0]


def _ep_residual(accs, extras):
    return extras[0] + accs[0]


def _ep_swiglu(accs, extras):
    return _silu(accs[0]) * accs[1]


def _ep_glu_self(accs, extras):
    g5 = extras[0]
    return g5 * _sigmoid(accs[0])


def _ep_gated_merge(accs, extras):
    return _sigmoid(extras[0]) * accs[0] + _sigmoid(extras[1]) * accs[1]


def _ep_beta_decay(n_heads):
    def ep(accs, extras):
        acc = accs[0]
        alog_row, dtb_row = extras
        lane = lax.broadcasted_iota(jnp.int32, acc.shape, 1)
        return jnp.where(lane < n_heads, _sigmoid(acc),
                         -jnp.exp(alog_row) * _softplus(acc + dtb_row))
    return ep


def _head_columns(bg, head, n_heads):
    lane = lax.broadcasted_iota(jnp.int32, bg.shape, 1)
    beta = jnp.sum(jnp.where(lane == head, bg, 0.0), axis=-1, keepdims=True)
    g = jnp.sum(jnp.where(lane == head + n_heads, bg, 0.0), axis=-1, keepdims=True)
    return beta, g


def _l2norm_rows(x):
    return x * lax.rsqrt(jnp.sum(x * x, axis=-1, keepdims=True) + L2_EPS)


def _sum_rows(x):
    acc = x[0:SUBLANES]
    for i in range(1, x.shape[0] // SUBLANES):
        acc = acc + x[i * SUBLANES:(i + 1) * SUBLANES]
    shift = SUBLANES // 2
    while shift:
        acc = acc + pltpu.roll(acc, shift, 0)
        shift //= 2
    return acc[0:1]


def _gated_out_norm(o, z, nw_row):
    y = o * lax.rsqrt(jnp.mean(o * o, axis=-1, keepdims=True) + NORM_EPS)
    return y * nw_row * _silu(z)


def _gdn_prompt_kernel(q_ref, k_ref, v_ref, z_ref, bg_ref, cwq_ref, cwk_ref, cwv_ref, nw_ref,
                       o_ref, s_ref, pcq_ref, pck_ref, pcv_ref,
                       qn, kn, vn, gb, bb, us, ws, qks, qds, kdt, gl, osc, *, n_heads):
    hb, L, dk = qn.shape
    n_hist = pcq_ref.shape[0]
    for x_ref, pc_ref in ((q_ref, pcq_ref), (k_ref, pck_ref), (v_ref, pcv_ref)):
        pc_ref[...] = x_ref[L - n_hist:L, :]
    C = GDN_CHUNK
    n_chunks = L // C
    H8 = range(hb)

    row8 = lax.broadcasted_iota(jnp.int32, (SUBLANES, dk), 0)

    def conv_silu(x_ref, cw_ref, cols):
        cw = cw_ref[:, cols]
        n_taps = cw.shape[0]
        assert n_taps - 1 <= SUBLANES
        tap = lambda j: cw[n_taps - 1 - j:n_taps - j, :]
        head8 = x_ref[0:SUBLANES, cols]
        lo = head8 * tap(0)
        hi = x_ref[SUBLANES:L, cols] * tap(0)
        for j in range(1, n_taps):
            lo = lo + jnp.where(row8 >= j, pltpu.roll(head8, j, 0), 0.0) * tap(j)
            hi = hi + x_ref[pl.ds(SUBLANES - j, L - SUBLANES), cols] * tap(j)
        return _silu(jnp.concatenate([lo, hi], axis=0))

    bg = bg_ref[...]
    for hh in H8:
        cols = slice(hh * dk, (hh + 1) * dk)
        qn[hh] = _l2norm_rows(conv_silu(q_ref, cwq_ref, cols)) * (dk ** -0.5)
        kn[hh] = _l2norm_rows(conv_silu(k_ref, cwk_ref, cols))
        vn[hh] = conv_silu(v_ref, cwv_ref, cols)
        beta, g = _head_columns(bg, pl.program_id(1) * hb + hh, n_heads)
        bb[hh] = jnp.broadcast_to(beta, (L, dk))
        gb[hh] = jnp.broadcast_to(g, (L, dk))

    ri = lax.broadcasted_iota(jnp.int32, (C, C), 0)
    ci = lax.broadcasted_iota(jnp.int32, (C, C), 1)
    causal = ri >= ci
    strict = ri > ci
    tri_incl = jnp.where(causal, 1.0, 0.0).astype(F32)
    eye = jnp.where(ri == ci, 1.0, 0.0).astype(F32)
    level_masks = []
    n = 1
    while n < C:
        sh = n.bit_length() - 1
        same_2n = (ri >> (sh + 1)) == (ci >> (sh + 1))
        diff_n = (ri >> sh) != (ci >> sh)
        level_masks.append(jnp.where(same_2n & diff_n & strict, 1.0, 0.0).astype(F32))
        n *= 2

    group = math.gcd(n_chunks, GDN_CHUNKS_PER_GROUP)

    def intra_group(hh, i):
        G = range(group)
        rows = [pl.ds(pl.multiple_of((i * group + j) * C, C), C) for j in G]
        k = [kn[hh, r, :] for r in rows]
        bet = [bb[hh, r, :] for r in rows]
        gcb = [jnp.dot(tri_incl, gb[hh, r, :], precision=lax.Precision.HIGHEST,
                       preferred_element_type=F32) for r in rows]
        gamma = [jnp.where(causal, jnp.exp(jnp.minimum(g - g.T, 0.0)), 0.0) for g in gcb]
        kb = [k[j] * bet[j] for j in G]
        a_mat = [jnp.where(strict, _bdot_nt(kb[j], k[j]) * gamma[j], 0.0) for j in G]
        q = [qn[hh, r, :] for r in rows]
        for j in G:
            qks[hh, rows[j], :] = _bdot_nt(q[j], k[j]) * gamma[j]
        t = [eye - a * level_masks[0] for a in a_mat]
        for m in level_masks[1:]:
            x = [_bdot(a_mat[j] * m, t[j]) for j in G]
            t = [t[j] - _bdot(t[j], x[j]) for j in G]
        eg = [jnp.exp(g) for g in gcb]
        for j in G:
            us[hh, rows[j], :] = _bdot(t[j], vn[hh, rows[j], :] * bet[j])
        for j in G:
            ws[hh, rows[j], :] = _bdot(t[j], kb[j] * eg[j])
        for j in G:
            qds[hh, rows[j], :] = q[j] * eg[j]
            g_last = gcb[j][C - 1:C, :]
            kdt[hh, rows[j], :] = (k[j] * jnp.exp(g_last - gcb[j])).T
            gl[hh, pl.ds(pl.multiple_of((i * group + j) * SUBLANES, SUBLANES), SUBLANES), :] = (
                jnp.broadcast_to(jnp.exp(g_last), (SUBLANES, dk)))

    for hh in H8:
        def body(i, carry, hh=hh):
            intra_group(hh, i)
            return carry
        lax.fori_loop(0, n_chunks // group, body, 0)

    def inter(c, states):
        rows = pl.ds(pl.multiple_of(c * C, C), C)
        ws_s = [_bdot(ws[hh, rows, :], states[hh]) for hh in H8]
        qd_s = [_bdot(qds[hh, rows, :], states[hh]) for hh in H8]
        v_new = [us[hh, rows, :] - ws_s[hh] for hh in H8]
        for hh in H8:
            osc[hh, rows, :] = qd_s[hh] + _bdot(qks[hh, rows, :], v_new[hh])
        decay = [gl[hh, pl.ds(pl.multiple_of(c * SUBLANES, SUBLANES), 1), :] for hh in H8]
        return tuple(states[hh] * decay[hh] + _bdot(kdt[hh, rows, :], v_new[hh]) for hh in H8)

    s_fin = lax.fori_loop(0, n_chunks, inter, tuple(jnp.zeros((dk, dk), F32) for _ in H8))
    for hh in H8:
        cols = slice(hh * dk, (hh + 1) * dk)
        s_ref[hh] = s_fin[hh]
        o_ref[:, cols] = _gated_out_norm(osc[hh], z_ref[:, cols], nw_ref[...]).astype(o_ref.dtype)


def _gdn_prompt(qkvz, bg, conv_w, norm_w, layer, batch, seq, n_heads, dk):
    H = n_heads
    L = seq
    hb = GDN_HEADS_PER_STEP if H % GDN_HEADS_PER_STEP == 0 else 1
    hg = H // hb
    n_hist = conv_w.shape[1] - 1

    def col(sec):
        return pl.BlockSpec((L, hb * dk), lambda b, h, sec=sec: (b, h + sec * hg))

    def cw(sec):
        return pl.BlockSpec((None, conv_w.shape[1], hb * dk),
                            lambda b, h, sec=sec: (layer, 0, h + sec * hg))

    scr = lambda: pltpu.VMEM((hb, L, dk), F32)
    nbytes = hb * (2 * 4 * L * dk * 4 + 11 * L * dk * 4 + 2 * L * dk * 2) + 2 * L * LANES * 4 \
        + 60 * GDN_CHUNK * GDN_CHUNK * 4
    return pl.pallas_call(
        functools.partial(_gdn_prompt_kernel, n_heads=H),
        out_shape=(jax.ShapeDtypeStruct((batch * L, H * dk), BF16),
                   jax.ShapeDtypeStruct((batch, H, dk, dk), F32))
        + (jax.ShapeDtypeStruct((batch, n_hist, H * dk), F32),) * 3,
        grid=(batch, hg),
        in_specs=[col(0), col(1), col(2), col(3),
                  pl.BlockSpec((L, LANES), lambda b, h: (b, 0)),
                  cw(0), cw(1), cw(2),
                  pl.BlockSpec((None, 1, dk), lambda b, h: (layer, 0, 0))],
        out_specs=(pl.BlockSpec((L, hb * dk), lambda b, h: (b, h)),
                   pl.BlockSpec((None, hb, dk, dk), lambda b, h: (b, h, 0, 0)))
        + (pl.BlockSpec((None, n_hist, hb * dk), lambda b, h: (b, 0, h)),) * 3,
        scratch_shapes=[scr(), scr(), scr(), scr(), scr(),
                        scr(), scr(), scr(), scr(), scr(),
                        pltpu.VMEM((hb, L // GDN_CHUNK * SUBLANES, dk), F32),
                        scr()],
        compiler_params=pltpu.CompilerParams(
            dimension_semantics=("parallel", "parallel"),
            vmem_limit_bytes=_vmem_limit(nbytes)),
        name="gdn_prompt",
    )(qkvz, qkvz, qkvz, qkvz, bg, conv_w, conv_w, conv_w, norm_w)


def _gdn_sample_kernel(q_ref, k_ref, v_ref, z_ref, bg_ref, bq_ref, bk_ref, bv_ref,
                       cwq_ref, cwk_ref, cwv_ref, nw_ref, s_in_ref, *rest, n_heads):
    o_ref, s_out_ref, cq_ref, ck_ref, cv_ref, osc = rest[-6:]
    nb, dk = q_ref.shape
    head = pl.program_id(0)

    def conv_silu(x_ref, buf_ref, new_ref, cw_ref):
        cw = cw_ref[...]
        n_hist = buf_ref.shape[1]
        x = x_ref[...]
        y = x * cw[n_hist:n_hist + 1, :]
        for i in range(n_hist):
            row = buf_ref[:, i, :]
            y = y + row * cw[i:i + 1, :]
            if i > 0:
                new_ref[:, i - 1, :] = row
        new_ref[:, n_hist - 1, :] = x
        return _silu(y)

    q = _l2norm_rows(conv_silu(q_ref, bq_ref, cq_ref, cwq_ref)) * (dk ** -0.5)
    k = _l2norm_rows(conv_silu(k_ref, bk_ref, ck_ref, cwk_ref))
    v = conv_silu(v_ref, bv_ref, cv_ref, cwv_ref)
    beta, g = _head_columns(bg_ref[...], head, n_heads)
    decay = jnp.exp(g)
    kt = jnp.concatenate([k, jnp.zeros((LANES - nb, dk), F32)], axis=0).T if nb < LANES else k.T
    qt = jnp.concatenate([q, jnp.zeros((LANES - nb, dk), F32)], axis=0).T if nb < LANES else q.T
    for b in range(nb):
        s = s_in_ref[b] * decay[b:b + 1, :]
        kcol = kt[:, b:b + 1]
        v_new = (v[b:b + 1, :] - _sum_rows(s * kcol)) * beta[b:b + 1, :]
        s = s + kcol * v_new
        s_out_ref[b] = s
        osc[b:b + 1, :] = _sum_rows(s * qt[:, b:b + 1])
    o_ref[...] = _gated_out_norm(osc[...], z_ref[...], nw_ref[...]).astype(o_ref.dtype)


def _gdn_sample(qkvz, bg, row0, conv_buf, conv_w, norm_w, state, layer, n_heads, dk,
                state_out=None):
    H = n_heads
    nb_total = state.shape[1]
    nb = 16
    assert nb_total % nb == 0 and row0 % nb == 0
    r0 = row0 // nb
    n_hist = conv_buf.shape[2]

    def col(off):
        return pl.BlockSpec((nb, dk), lambda h, i, off=off: (i + r0, h + off))

    def buf(off):
        return pl.BlockSpec((None, nb, n_hist, dk), lambda h, i, off=off: (layer, i, 0, h + off))

    def cw(off):
        return pl.BlockSpec((None, conv_w.shape[1], dk), lambda h, i, off=off: (layer, 0, h + off))

    in_specs = [col(0), col(H), col(2 * H), col(3 * H),
                pl.BlockSpec((nb, LANES), lambda h, i: (i + r0, 0)),
                buf(0), buf(H), buf(2 * H),
                cw(0), cw(H), cw(2 * H),
                pl.BlockSpec((None, 1, dk), lambda h, i: (layer, 0, 0)),
                pl.BlockSpec((None, nb, None, dk, dk), lambda h, i: (layer, i, h, 0, 0))]
    args = [qkvz, qkvz, qkvz, qkvz, bg, conv_buf, conv_buf, conv_buf,
            conv_w, conv_w, conv_w, norm_w, state]
    aliases = {}
    if state_out is not None:
        in_specs.append(pl.BlockSpec(memory_space=pl.ANY))
        args.append(state_out)
        aliases = {len(args) - 1: 1}
    nbytes = 4 * nb * dk * dk * 4 + 64 * nb * dk * 4 + 64 * dk * dk * 4
    return pl.pallas_call(
        functools.partial(_gdn_sample_kernel, n_heads=H),
        out_shape=(jax.ShapeDtypeStruct((nb_total, H * dk), BF16),
                   jax.ShapeDtypeStruct(state.shape, F32))
        + (jax.ShapeDtypeStruct((nb_total, n_hist, H * dk), F32),) * 3,
        grid=(H, nb_total // nb),
        in_specs=in_specs,
        out_specs=(pl.BlockSpec((nb, dk), lambda h, i: (i, h)),
                   pl.BlockSpec((None, nb, None, dk, dk), lambda h, i: (layer, i, h, 0, 0)))
        + (pl.BlockSpec((nb, n_hist, dk), lambda h, i: (i, 0, h)),) * 3,
        scratch_shapes=[pltpu.VMEM((nb, dk), F32)],
        input_output_aliases=aliases,
        compiler_params=pltpu.CompilerParams(
            dimension_semantics=("parallel", "parallel"),
            vmem_limit_bytes=_vmem_limit(nbytes)),
        name="gdn_sample",
    )(*args)


def _s5_tables(lam_re, lam_im, log_dt, b_re, b_im, c_re, c_im):
    D, G, P = lam_re.shape
    gc = b_re.shape[-1]
    gpb = LANES // gc
    nblk = G // gpb
    dt = jnp.exp(log_dt)[..., None]
    mag = jnp.exp(lam_re * dt)
    ar = mag * jnp.cos(lam_im * dt)
    ai = mag * jnp.sin(lam_im * dt)
    nr = ar - 1.0
    den = lam_re * lam_re + lam_im * lam_im
    fr = (nr * lam_re + ai * lam_im) / den
    fi = (ai * lam_re - nr * lam_im) / den
    bbar_re = fr[..., None] * b_re - fi[..., None] * b_im
    bbar_im = fr[..., None] * b_im + fi[..., None] * b_re
    eye = jnp.eye(gpb, dtype=F32)

    def bmat(bb):
        t = bb.reshape(D, nblk, gpb, P, gc)
        return jnp.einsum('djgpc,gh->djgchp', t, eye).reshape(D, nblk, gpb * gc, gpb * P)

    def cmat(cc):
        t = cc.reshape(D, nblk, gpb, gc, P)
        return jnp.einsum('djgcp,gh->djhpgc', t, eye).reshape(D, nblk, gpb * P, gpb * gc)

    b_blk = jnp.concatenate([bmat(bbar_re), bmat(bbar_im)], axis=3).astype(BF16)
    c_blk = jnp.concatenate([cmat(c_re), -cmat(c_im)], axis=2).astype(BF16)
    return (b_blk, c_blk, ar.reshape(D, nblk, 1, gpb * P), ai.reshape(D, nblk, 1, gpb * P))


def _s5_prompt_kernel(u_ref, bblk_ref, cblk_ref, ar_ref, ai_ref, d_ref,
                      g5_ref, g5b_ref, xre_ref, xim_ref, up, xs, ys):
    L = u_ref.shape[0]
    ns = ar_ref.shape[-1]
    nseg = S5_SEGMENTS
    seg = L // nseg

    def slab(t):
        return pl.ds(pl.multiple_of(t * nseg, nseg), nseg)

    def gather(t, carry):
        up[slab(t), :] = u_ref[pl.ds(t, nseg, stride=seg), :]
        return carry

    lax.fori_loop(0, seg, gather, 0, unroll=8)
    u = up[...]
    xs[...] = jnp.dot(u.astype(BF16), bblk_ref[...], preferred_element_type=F32)
    ar = jnp.broadcast_to(ar_ref[...], (nseg, ns))
    ai = jnp.broadcast_to(ai_ref[...], (nseg, ns))

    def step(x, t):
        xr, xi = x
        r = xs[slab(t), :]
        return (ar * xr - ai * xi + r[:, :ns], ar * xi + ai * xr + r[:, ns:])

    zero = jnp.zeros((nseg, ns), F32)
    er, ei = lax.fori_loop(0, seg, lambda t, x: step(x, t), (zero, zero), unroll=8)
    pr, pi = ar_ref[...], ai_ref[...]
    for _ in range(seg.bit_length() - 1):
        pr, pi = pr * pr - pi * pi, 2.0 * pr * pi
    assert seg == 1 << (seg.bit_length() - 1)
    cr = [jnp.zeros((1, ns), F32)]
    ci = [jnp.zeros((1, ns), F32)]
    for s in range(nseg - 1):
        cr.append(er[s:s + 1] + pr * cr[s] - pi * ci[s])
        ci.append(ei[s:s + 1] + pr * ci[s] + pi * cr[s])
    x0 = (jnp.concatenate(cr, axis=0), jnp.concatenate(ci, axis=0))

    def step_store(t, x):
        xr, xi = step(x, t)
        xs[slab(t), :] = jnp.concatenate([xr, xi], axis=1)
        return (xr, xi)

    xr, xi = lax.fori_loop(0, seg, step_store, x0, unroll=8)
    xre_ref[...] = xr[nseg - 1:nseg]
    xim_ref[...] = xi[nseg - 1:nseg]
    y = jnp.dot(xs[...].astype(BF16), cblk_ref[...], preferred_element_type=F32) + d_ref[...] * u
    ys[...] = _gelu_tanh(y)
    for s in range(nseg):
        g5 = ys[pl.ds(s, seg, stride=nseg), :]
        g5_ref[s * seg:(s + 1) * seg, :] = g5
        g5b_ref[s * seg:(s + 1) * seg, :] = g5.astype(BF16)


def _s5_prompt(u_all, tables, d_skip, layer, batch, seq, n_ch):
    b_blk, c_blk, ar, ai = tables
    nblk, _, ns2 = b_blk.shape[1:]
    ns = ns2 // 2
    L = seq
    nbytes = (2 * L * LANES * 4 + 2 * L * LANES * 6 + 3 * L * ns2 * 4 + L * ns2 * 2
              + 8 * LANES * ns2 * 2)
    blkp = lambda r, c: pl.BlockSpec((None, None, r, c), lambda b, j: (layer, j, 0, 0))
    return pl.pallas_call(
        _s5_prompt_kernel,
        out_shape=(jax.ShapeDtypeStruct((batch * L, n_ch), F32),
                   jax.ShapeDtypeStruct((batch * L, n_ch), BF16),
                   jax.ShapeDtypeStruct((batch, 1, nblk * ns), F32),
                   jax.ShapeDtypeStruct((batch, 1, nblk * ns), F32)),
        grid=(batch, nblk),
        in_specs=[pl.BlockSpec((L, LANES), lambda b, j: (b, j)),
                  blkp(LANES, ns2), blkp(ns2, LANES), blkp(1, ns), blkp(1, ns),
                  pl.BlockSpec((None, 1, LANES), lambda b, j: (layer, 0, j))],
        out_specs=(pl.BlockSpec((L, LANES), lambda b, j: (b, j)),
                   pl.BlockSpec((L, LANES), lambda b, j: (b, j)),
                   pl.BlockSpec((None, 1, ns), lambda b, j: (b, 0, j)),
                   pl.BlockSpec((None, 1, ns), lambda b, j: (b, 0, j))),
        scratch_shapes=[pltpu.VMEM((L, LANES), F32),
                        pltpu.VMEM((L, ns2), F32),
                        pltpu.VMEM((L, LANES), F32)],
        compiler_params=pltpu.CompilerParams(
            dimension_semantics=("parallel", "parallel"),
            vmem_limit_bytes=_vmem_limit(nbytes)),
        name="s5_prompt",
    )(u_all, b_blk, c_blk, ar, ai, d_skip)


def _s5_sample_kernel(u_ref, bblk_ref, cblk_ref, ar_ref, ai_ref, d_ref, x0r_ref, x0i_ref,
                      g5_ref, g5b_ref, xre_ref, xim_ref):
    ns = ar_ref.shape[-1]
    u = u_ref[...]
    bu = jnp.dot(u.astype(BF16), bblk_ref[...], preferred_element_type=F32)
    ar, ai = ar_ref[...], ai_ref[...]
    x0r, x0i = x0r_ref[...], x0i_ref[...]
    xr = ar * x0r - ai * x0i + bu[:, :ns]
    xi = ar * x0i + ai * x0r + bu[:, ns:]
    xre_ref[...] = xr
    xim_ref[...] = xi
    x = jnp.concatenate([xr, xi], axis=1)
    y = jnp.dot(x.astype(BF16), cblk_ref[...], preferred_element_type=F32) + d_ref[...] * u
    g5 = _gelu_tanh(y)
    g5_ref[...] = g5
    g5b_ref[...] = g5.astype(BF16)


def _s5_sample(u_all, row0, tables, d_skip, x0_re, x0_im, layer, n_ch):
    b_blk, c_blk, ar, ai = tables
    nblk, _, ns2 = b_blk.shape[1:]
    ns = ns2 // 2
    nb = x0_re.shape[1]
    assert row0 % nb == 0
    r0 = row0 // nb
    blkp = lambda r, c: pl.BlockSpec((None, None, r, c), lambda j: (layer, j, 0, 0))
    nbytes = 16 * nb * ns2 * 4 + 8 * LANES * ns2 * 2
    return pl.pallas_call(
        _s5_sample_kernel,
        out_shape=(jax.ShapeDtypeStruct((nb, n_ch), F32),
                   jax.ShapeDtypeStruct((nb, n_ch), BF16),
                   jax.ShapeDtypeStruct((nb, nblk * ns), F32),
                   jax.ShapeDtypeStruct((nb, nblk * ns), F32)),
        grid=(nblk,),
        in_specs=[pl.BlockSpec((nb, LANES), lambda j: (r0, j)),
                  blkp(LANES, ns2), blkp(ns2, LANES), blkp(1, ns), blkp(1, ns),
                  pl.BlockSpec((None, 1, LANES), lambda j: (layer, 0, j)),
                  pl.BlockSpec((None, nb, ns), lambda j: (layer, 0, j)),
                  pl.BlockSpec((None, nb, ns), lambda j: (layer, 0, j))],
        out_specs=(pl.BlockSpec((nb, LANES), lambda j: (0, j)),
                   pl.BlockSpec((nb, LANES), lambda j: (0, j)),
                   pl.BlockSpec((nb, ns), lambda j: (0, j)),
                   pl.BlockSpec((nb, ns), lambda j: (0, j))),
        compiler_params=pltpu.CompilerParams(
            dimension_semantics=("parallel",), vmem_limit_bytes=_vmem_limit(nbytes)),
        name="s5_sample",
    )(u_all, b_blk, c_blk, ar, ai, d_skip, x0_re, x0_im)


def kernel(x_prompt, x_sample, state_dn_conv, state_dn_ssm, state_s5_re, state_s5_im, norm1, w_in, dn_conv_w, dn_a_log, dn_dt_bias, dn_norm_w, w_br_dn, s5_lam_re, s5_lam_im, s5_log_dt, s5_b_re, s5_b_im, s5_c_re, s5_c_im, s5_d, w_glu, w_br_s5, w_out, norm2, w_ffn_gate, w_ffn_up, w_ffn_down, norm_f):
    batch, seq, d_model = x_prompt.shape
    nb, dec_seq, _ = x_sample.shape
    assert dec_seq == 1
    depth, _, n_heads, dk, dv = state_dn_ssm.shape
    assert dk == LANES and dv == LANES and seq % GDN_CHUNK == 0 and seq % S5_SEGMENTS == 0
    qk_dim = n_heads * dk
    conv_ch = dn_conv_w.shape[2]
    assert conv_ch == 3 * qk_dim
    n_ch = s5_d.shape[1]
    n_groups, n_state = s5_lam_re.shape[1:]
    ffn = w_ffn_gate.shape[2]
    mp = batch * seq
    m = mp + nb
    z_end = 4 * qk_dim
    rest0 = z_end + 2 * n_heads
    assert w_in.shape[2] == rest0 + n_ch + 2 * d_model and 2 * n_heads <= LANES

    x = (x_prompt.reshape(mp, d_model), x_sample.reshape(nb, d_model))

    w_bg = jnp.pad(w_in[:, :, z_end:rest0], ((0, 0), (0, 0), (0, LANES - 2 * n_heads)))
    w_s5 = w_in[:, :, rest0:rest0 + n_ch]
    gate_tn = 512
    gate0 = rest0 + n_ch
    gate_base = gate0 - gate0 % gate_tn
    gate_shift = gate0 - gate_base
    gate_cols = -(-(gate_shift + 2 * d_model) // gate_tn) * gate_tn
    assert gate_shift < LANES and gate_base + gate_cols - gate_tn < w_in.shape[2]
    pad_heads = lambda a: jnp.pad(a, ((0, 0), (n_heads, LANES - 2 * n_heads)))[:, None, :]
    alog_pad = pad_heads(dn_a_log)
    dtb_pad = pad_heads(dn_dt_bias)
    x0_re = state_s5_re.reshape(depth, nb, n_groups * n_state)
    x0_im = state_s5_im.reshape(depth, nb, n_groups * n_state)

    tm = _pick_tile(m, 1664, 64)
    norm_w3 = dn_norm_w[:, None, :]
    d_skip3 = s5_d[:, None, :]
    tables = _s5_tables(s5_lam_re, s5_lam_im, s5_log_dt, s5_b_re, s5_b_im, s5_c_re, s5_c_im)
    outs = {k: [] for k in ("p_conv", "p_ssm", "p_re", "p_im", "s_conv", "s_re", "s_im")}
    s_ssm = None
    for l in range(depth):
        h = _rmsnorm(x, norm1[l][None, :], BF16)
        qkvz = _fused_matmul([h], [(0, w_in, l, 0)], [], _ep_identity, z_end, F32,
                             tm=tm, tn=512, name="in_qkvz")
        bg = _fused_matmul([h], [(0, w_bg, l, 0)], [], _ep_beta_decay(n_heads), LANES, F32,
                           tm=tm, tn=LANES, name="in_bg",
                           col_params=[(alog_pad, l), (dtb_pad, l)])
        s5u = _fused_matmul([h], [(0, w_s5, l, 0)], [], _ep_identity, n_ch, F32,
                            tm=tm, tn=512, name="in_s5")
        gates = _fused_matmul([h], [(0, w_in, l, gate_base)], [], _ep_identity, gate_cols, F32,
                              tm=tm, tn=gate_tn, name="in_gates")

        o_p, ssm_p, *pc = _gdn_prompt(qkvz, bg, dn_conv_w, norm_w3, l, batch, seq, n_heads, dk)
        o_s, s_ssm, *sc = _gdn_sample(qkvz, bg, mp, state_dn_conv, dn_conv_w, norm_w3,
                                      state_dn_ssm, l, n_heads, dk, state_out=s_ssm)
        outs["p_conv"].append(jnp.concatenate(pc, axis=-1))
        outs["s_conv"].append(jnp.concatenate(sc, axis=-1))
        outs["p_ssm"].append(ssm_p)

        g5_p, g5b_p, re_p, im_p = _s5_prompt(s5u, tables, d_skip3, l, batch, seq, n_ch)
        g5_s, g5b_s, re_s, im_s = _s5_sample(s5u, mp, tables, d_skip3, x0_re, x0_im, l, n_ch)
        outs["p_re"].append(re_p.reshape(batch, n_groups, n_state))
        outs["p_im"].append(im_p.reshape(batch, n_groups, n_state))
        outs["s_re"].append(re_s.reshape(nb, n_groups, n_state))
        outs["s_im"].append(im_s.reshape(nb, n_groups, n_state))
        g5g = _fused_matmul([(g5b_p, g5b_s)], [(0, w_glu, l, 0)], [((g5_p, g5_s), 0)],
                            _ep_glu_self, n_ch, BF16, tm=tm, tn=512, name="s5_glu")

        merged = _fused_matmul([(o_p, o_s), g5g], [(0, w_br_dn, l, 0), (1, w_br_s5, l, 0)],
                               [(gates, gate_shift), (gates, gate_shift + d_model)],
                               _ep_gated_merge,
                               d_model, BF16, tm=tm, tn=512, name="branch_merge")
        x = _fused_matmul([merged], [(0, w_out, l, 0)], [(x, 0)], _ep_residual, d_model, F32,
                          tm=tm, tn=512, name="out_proj")

        h2 = _rmsnorm(x, norm2[l][None, :], BF16)
        hmid = _fused_matmul([h2], [(0, w_ffn_gate, l, 0), (0, w_ffn_up, l, 0)], [], _ep_swiglu,
                             ffn, BF16, tm=tm, tn=256, name="ffn_up")
        x = _matmul_residual_wstat(hmid, w_ffn_down, l, x, tm=_pick_tile(m, 416, 16), tn=512,
                                   name="ffn_down")

    y_p, y_s = _rmsnorm(x, norm_f[None, :], F32, split_rows=(mp, nb))
    st = lambda k: jnp.stack(outs[k])
    return (y_p.reshape(batch, seq, d_model), y_s.reshape(nb, 1, d_model),
            st("p_conv"), st("p_ssm"), st("p_re"), st("p_im"),
            st("s_conv"), s_ssm, st("s_re"), st("s_im"))
```

```python
import functools
import math

import jax
import jax.numpy as jnp
from jax import lax
from jax.experimental import pallas as pl
from jax.experimental.pallas import tpu as pltpu

F32 = jnp.float32
BF16 = jnp.bfloat16

NORM_EPS = 1e-6
L2_EPS = 1e-6
LANES = 128
SUBLANES = 8
VMEM_CAP_BYTES = 56 * 1024 * 1024
GDN_CHUNK = 128
S5_SEGMENTS = SUBLANES
GDN_HEADS_PER_STEP = 2
GDN_CHUNKS_PER_GROUP = 16


def _vmem_limit(nbytes):
    return int(min(VMEM_CAP_BYTES, nbytes * 5 // 4 + (4 << 20)))


def _pick_tile(n, target, mult):
    best = None
    for t in range(mult, min(n, target) + 1, mult):
        if n % t == 0:
            best = t
    return best if best is not None else n


def _sigmoid(x):
    return 1.0 / (1.0 + jnp.exp(-x))


def _silu(x):
    return x * _sigmoid(x)


def _softplus(x):
    return jnp.maximum(x, 0.0) + jnp.log1p(jnp.exp(-jnp.abs(x)))


def _gelu_tanh(x):
    c = math.sqrt(2.0 / math.pi)
    return 0.5 * x * (1.0 + jnp.tanh(c * (x + 0.044715 * (x * x * x))))


def _bdot(a, b):
    return jnp.dot(a.astype(BF16), b.astype(BF16), preferred_element_type=F32)


def _bdot_nt(a, b):
    return lax.dot_general(a.astype(BF16), b.astype(BF16), (((1,), (1,)), ((), ())),
                           preferred_element_type=F32)


def _rows_of(op):
    return op[0].shape[0] + op[1].shape[0] if isinstance(op, tuple) else op.shape[0]


def _row_specs(op, tm, ncols, index_map):
    if not isinstance(op, tuple):
        return [pl.BlockSpec((tm, ncols), index_map)], [op], None
    p, s = op
    tail = p.shape[0] % tm
    assert tail + s.shape[0] == tm and tail % 16 == 0, (p.shape, s.shape, tm)

    def s_map(*idx):
        return (0,) + tuple(index_map(*idx)[1:])

    return ([pl.BlockSpec((tm, ncols), index_map), pl.BlockSpec((s.shape[0], ncols), s_map)],
            [p, s], tail)


def _load_rows(refs, tail, last):
    if tail is None or not last:
        return refs[0][...]
    return jnp.concatenate([refs[0][:tail, :], refs[1][...]], axis=0)


def _lane_window(parts, shift, width):
    x = parts[0] if len(parts) == 1 else jnp.concatenate(parts, axis=1)
    if shift:
        x = pltpu.roll(x, x.shape[1] - shift, 1)
    return x[:, :width]


def _on_row_tiles(i, n_tiles, any_split, body):
    if not any_split:
        body(False)
        return
    if n_tiles > 1:
        pl.when(i < n_tiles - 1)(lambda: body(False))
    pl.when(i == n_tiles - 1)(lambda: body(True))


def _rmsnorm_kernel(*refs, tail, n_tiles, out_tail):
    n_out = 1 if out_tail is None else 2
    x_refs, w_ref, o_refs = refs[:-1 - n_out], refs[-1 - n_out], refs[-n_out:]

    def body(last):
        x = _load_rows(x_refs, tail, last)
        y = x * lax.rsqrt(jnp.mean(x * x, axis=-1, keepdims=True) + NORM_EPS)
        y = (y * w_ref[...]).astype(o_refs[0].dtype)
        if out_tail is None:
            o_refs[0][...] = y
        elif not last:
            o_refs[0][...] = y
        else:
            o_refs[0][:out_tail, :] = y[:out_tail]
            o_refs[1][...] = y[out_tail:]

    _on_row_tiles(pl.program_id(0), n_tiles, tail is not None or out_tail is not None, body)


def _rmsnorm(x, w_row, out_dtype, split_rows=None):
    m = _rows_of(x)
    d = w_row.shape[1]
    tr = _pick_tile(m, 832, 64)
    specs, arrs, tail = _row_specs(x, tr, d, lambda i: (i, 0))
    nbytes = 2 * tr * d * 4 + 2 * tr * d * jnp.dtype(out_dtype).itemsize + 3 * tr * d * 4
    if split_rows is None:
        out_shape = jax.ShapeDtypeStruct((m, d), out_dtype)
        out_specs = pl.BlockSpec((tr, d), lambda i: (i, 0))
        out_tail = None
    else:
        mp, nb = split_rows
        out_tail = mp % tr
        assert mp + nb == m and out_tail + nb == tr
        out_shape = (jax.ShapeDtypeStruct((mp, d), out_dtype),
                     jax.ShapeDtypeStruct((nb, d), out_dtype))
        out_specs = (pl.BlockSpec((tr, d), lambda i: (i, 0)),
                     pl.BlockSpec((nb, d), lambda i: (0, 0)))
    return pl.pallas_call(
        functools.partial(_rmsnorm_kernel, tail=tail, n_tiles=m // tr, out_tail=out_tail),
        out_shape=out_shape,
        grid=(m // tr,),
        in_specs=specs + [pl.BlockSpec((1, d), lambda i: (0, 0))],
        out_specs=out_specs,
        compiler_params=pltpu.CompilerParams(
            dimension_semantics=("arbitrary",), vmem_limit_bytes=_vmem_limit(nbytes)),
        name="rmsnorm",
    )(*arrs, w_row)


def _mm_kernel(*refs, a_idx, w_transposed, a_groups, e_groups, n_tiles, epilogue):
    pos = 0
    a_refs = []
    for n, _ in a_groups:
        a_refs.append(refs[pos:pos + n])
        pos += n
    w_refs = refs[pos:pos + len(a_idx)]
    pos += len(a_idx)
    e_refs = []
    for n, _, _ in e_groups:
        e_refs.append(refs[pos:pos + n])
        pos += n
    o_ref = refs[pos]
    any_split = any(g[1] is not None for g in a_groups + e_groups)

    def load_extra(r, tail, shift, last):
        if shift:
            return _lane_window([r[0][...], r[1][...]], shift, o_ref.shape[1])
        return _load_rows(r, tail, last)

    def body(last):
        a_vals = [_load_rows(r, t, last) for r, (_, t) in zip(a_refs, a_groups)]
        parts = [(_bdot_nt if wt else _bdot)(a_vals[ai], w[...])
                 for ai, wt, w in zip(a_idx, w_transposed, w_refs)]
        e_vals = [load_extra(r, t, sh, last) for r, (_, t, sh) in zip(e_refs, e_groups)]
        o_ref[...] = epilogue(parts, e_vals).astype(o_ref.dtype)

    _on_row_tiles(pl.program_id(0), n_tiles, any_split, body)


def _fused_matmul(a_list, w_list, extras, epilogue, n_out, out_dtype, *, tm, tn, name,
                  col_params=()):
    m = _rows_of(a_list[0])
    assert m % tm == 0 and n_out % tn == 0
    in_specs, args, a_groups, e_groups = [], [], [], []
    kdims = []
    for a in a_list:
        kd = (a[0] if isinstance(a, tuple) else a).shape[1]
        specs, arrs, tail = _row_specs(a, tm, kd, lambda i, j: (i, 0))
        assert _rows_of(a) == m
        in_specs += specs
        args += arrs
        a_groups.append((len(arrs), tail))
        kdims.append(kd)
    w_list = [tuple(e) + (False,) * (5 - len(e)) for e in w_list]
    for ai, w, layer, col0, transposed in w_list:
        assert col0 % tn == 0 and w.shape[2 if transposed else 1] == kdims[ai]
        if transposed:
            spec = pl.BlockSpec((None, tn, w.shape[2]),
                                lambda i, j, layer=layer, off=col0 // tn: (layer, j + off, 0))
        else:
            spec = pl.BlockSpec((None, w.shape[1], tn),
                                lambda i, j, layer=layer, off=col0 // tn: (layer, 0, j + off))
        in_specs.append(spec)
        args.append(w)
    for e, col0 in extras:
        shift = col0 % tn
        assert shift < LANES and _rows_of(e) == m
        specs, arrs, tail = _row_specs(e, tm, tn, lambda i, j, off=col0 // tn: (i, j + off))
        if shift:
            assert not isinstance(e, tuple) and tn % LANES == 0
            specs.append(pl.BlockSpec(
                (tm, LANES), lambda i, j, off=col0 // tn: (i, (j + off + 1) * (tn // LANES))))
            arrs.append(e)
        in_specs += specs
        args += arrs
        e_groups.append((len(arrs), tail, shift))
    for p, layer in col_params:
        in_specs.append(pl.BlockSpec((None, 1, tn), lambda i, j, layer=layer: (layer, 0, j)))
        args.append(p)
        e_groups.append((1, None, 0))
    osz = jnp.dtype(out_dtype).itemsize
    nbytes = (sum(3 * tm * kd * 2 for kd in kdims)
              + sum(kdims[ai] * tn * (2 * w.dtype.itemsize + 2) for ai, w, _, _, _ in w_list)
              + sum(3 * tm * tn * 4 for _ in extras)
              + 2 * tm * tn * osz + (2 + len(w_list)) * tm * tn * 4)
    kern = functools.partial(_mm_kernel, a_idx=tuple(e[0] for e in w_list),
                             w_transposed=tuple(e[4] for e in w_list), a_groups=tuple(a_groups), e_groups=tuple(e_groups),
                             n_tiles=m // tm, epilogue=epilogue)
    return pl.pallas_call(
        kern,
        out_shape=jax.ShapeDtypeStruct((m, n_out), out_dtype),
        grid=(m // tm, n_out // tn),
        in_specs=in_specs,
        out_specs=pl.BlockSpec((tm, tn), lambda i, j: (i, j)),
        compiler_params=pltpu.CompilerParams(
            dimension_semantics=("parallel", "parallel"),
            vmem_limit_bytes=_vmem_limit(nbytes)),
        name=name,
    )(*args)


def _mm_residual_wstat_kernel(a_ref, w_ref, x_ref, o_ref, wb):
    @pl.when(pl.program_id(1) == 0)
    def _():
        wb[...] = w_ref[...].astype(BF16)

    o_ref[...] = x_ref[...] + jnp.dot(a_ref[...], wb[...], preferred_element_type=F32)


def _matmul_residual_wstat(a, w, layer, x, *, tm, tn, name):
    m, kdim = a.shape
    n_out = w.shape[2]
    assert m % tm == 0 and n_out % tn == 0 and w.shape[1] == kdim
    nbytes = 2 * tm * kdim * 2 + kdim * tn * (2 * w.dtype.itemsize + 2) + 6 * tm * tn * 4
    return pl.pallas_call(
        _mm_residual_wstat_kernel,
        out_shape=jax.ShapeDtypeStruct((m, n_out), F32),
        grid=(n_out // tn, m // tm),
        in_specs=[pl.BlockSpec((tm, kdim), lambda j, i: (i, 0)),
                  pl.BlockSpec((None, kdim, tn), lambda j, i: (layer, 0, j)),
                  pl.BlockSpec((tm, tn), lambda j, i: (i, j))],
        out_specs=pl.BlockSpec((tm, tn), lambda j, i: (i, j)),
        scratch_shapes=[pltpu.VMEM((kdim, tn), BF16)],
        compiler_params=pltpu.CompilerParams(
            dimension_semantics=("parallel", "arbitrary"),
            vmem_limit_bytes=_vmem_limit(nbytes)),
        name=name,
    )(a, w, x)


def _ep_identity(accs, extras):
    return accs[0]


def _ep_residual(accs, extras):
    return extras[0] + accs[0]


def _ep_swiglu(accs, extras):
    return _silu(accs[0]) * accs[1]


def _ep_glu_self(accs, extras):
    g5 = extras[0]
    return g5 * _sigmoid(accs[0])


def _ep_gated_merge(accs, extras):
    return _sigmoid(extras[0]) * accs[0] + _sigmoid(extras[1]) * accs[1]


def _ep_beta_decay(n_heads):
    def ep(accs, extras):
        acc = accs[0]
        alog_row, dtb_row = extras
        lane = lax.broadcasted_iota(jnp.int32, acc.shape, 1)
        return jnp.where(lane < n_heads, _sigmoid(acc),
                         -jnp.exp(alog_row) * _softplus(acc + dtb_row))
    return ep


def _head_columns(bg, head, n_heads):
    lane = lax.broadcasted_iota(jnp.int32, bg.shape, 1)
    beta = jnp.sum(jnp.where(lane == head, bg, 0.0), axis=-1, keepdims=True)
    g = jnp.sum(jnp.where(lane == head + n_heads, bg, 0.0), axis=-1, keepdims=True)
    return beta, g


def _l2norm_rows(x):
    return x * lax.rsqrt(jnp.sum(x * x, axis=-1, keepdims=True) + L2_EPS)


def _sum_rows(x):
    acc = x[0:SUBLANES]
    for i in range(1, x.shape[0] // SUBLANES):
        acc = acc + x[i * SUBLANES:(i + 1) * SUBLANES]
    shift = SUBLANES // 2
    while shift:
        acc = acc + pltpu.roll(acc, shift, 0)
        shift //= 2
    return acc[0:1]


def _gated_out_norm(o, z, nw_row):
    y = o * lax.rsqrt(jnp.mean(o * o, axis=-1, keepdims=True) + NORM_EPS)
    return y * nw_row * _silu(z)


def _gdn_prompt_kernel(q_ref, k_ref, v_ref, z_ref, bg_ref, cwq_ref, cwk_ref, cwv_ref, nw_ref,
                       o_ref, s_ref, pcq_ref, pck_ref, pcv_ref,
                       qn, kn, vn, gb, bb, us, ws, qks, qds, kdt, gl, osc, *, n_heads):
    hb, L, dk = qn.shape
    n_hist = pcq_ref.shape[0]
    for x_ref, pc_ref in ((q_ref, pcq_ref), (k_ref, pck_ref), (v_ref, pcv_ref)):
        pc_ref[...] = x_ref[L - n_hist:L, :]
    C = GDN_CHUNK
    n_chunks = L // C
    H8 = range(hb)

    row8 = lax.broadcasted_iota(jnp.int32, (SUBLANES, dk), 0)

    def conv_silu(x_ref, cw_ref, cols):
        cw = cw_ref[:, cols]
        n_taps = cw.shape[0]
        assert n_taps - 1 <= SUBLANES
        tap = lambda j: cw[n_taps - 1 - j:n_taps - j, :]
        head8 = x_ref[0:SUBLANES, cols]
        lo = head8 * tap(0)
        hi = x_ref[SUBLANES:L, cols] * tap(0)
        for j in range(1, n_taps):
            lo = lo + jnp.where(row8 >= j, pltpu.roll(head8, j, 0), 0.0) * tap(j)
            hi = hi + x_ref[pl.ds(SUBLANES - j, L - SUBLANES), cols] * tap(j)
        return _silu(jnp.concatenate([lo, hi], axis=0))

    bg = bg_ref[...]
    for hh in H8:
        cols = slice(hh * dk, (hh + 1) * dk)
        qn[hh] = _l2norm_rows(conv_silu(q_ref, cwq_ref, cols)) * (dk ** -0.5)
        kn[hh] = _l2norm_rows(conv_silu(k_ref, cwk_ref, cols))
        vn[hh] = conv_silu(v_ref, cwv_ref, cols)
        beta, g = _head_columns(bg, pl.program_id(1) * hb + hh, n_heads)
        bb[hh] = jnp.broadcast_to(beta, (L, dk))
        gb[hh] = jnp.broadcast_to(g, (L, dk))

    ri = lax.broadcasted_iota(jnp.int32, (C, C), 0)
    ci = lax.broadcasted_iota(jnp.int32, (C, C), 1)
    causal = ri >= ci
    strict = ri > ci
    tri_incl = jnp.where(causal, 1.0, 0.0).astype(F32)
    eye = jnp.where(ri == ci, 1.0, 0.0).astype(F32)
    level_masks = []
    n = 1
    while n < C:
        sh = n.bit_length() - 1
        same_2n = (ri >> (sh + 1)) == (ci >> (sh + 1))
        diff_n = (ri >> sh) != (ci >> sh)
        level_masks.append(jnp.where(same_2n & diff_n & strict, 1.0, 0.0).astype(F32))
        n *= 2

    group = math.gcd(n_chunks, GDN_CHUNKS_PER_GROUP)

    def intra_group(hh, i):
        G = range(group)
        rows = [pl.ds(pl.multiple_of((i * group + j) * C, C), C) for j in G]
        k = [kn[hh, r, :] for r in rows]
        bet = [bb[hh, r, :] for r in rows]
        gcb = [jnp.dot(tri_incl, gb[hh, r, :], precision=lax.Precision.HIGHEST,
                       preferred_element_type=F32) for r in rows]
        gamma = [jnp.where(causal, jnp.exp(jnp.minimum(g - g.T, 0.0)), 0.0) for g in gcb]
        kb = [k[j] * bet[j] for j in G]
        a_mat = [jnp.where(strict, _bdot_nt(kb[j], k[j]) * gamma[j], 0.0) for j in G]
        q = [qn[hh, r, :] for r in rows]
        for j in G:
            qks[hh, rows[j], :] = _bdot_nt(q[j], k[j]) * gamma[j]
        t = [eye - a * level_masks[0] for a in a_mat]
        for m in level_masks[1:]:
            x = [_bdot(a_mat[j] * m, t[j]) for j in G]
            t = [t[j] - _bdot(t[j], x[j]) for j in G]
        eg = [jnp.exp(g) for g in gcb]
        for j in G:
            us[hh, rows[j], :] = _bdot(t[j], vn[hh, rows[j], :] * bet[j])
        for j in G:
            ws[hh, rows[j], :] = _bdot(t[j], kb[j] * eg[j])
        for j in G:
            qds[hh, rows[j], :] = q[j] * eg[j]
            g_last = gcb[j][C - 1:C, :]
            kdt[hh, rows[j], :] = (k[j] * jnp.exp(g_last - gcb[j])).T
            gl[hh, pl.ds(pl.multiple_of((i * group + j) * SUBLANES, SUBLANES), SUBLANES), :] = (
                jnp.broadcast_to(jnp.exp(g_last), (SUBLANES, dk)))

    for hh in H8:
        def body(i, carry, hh=hh):
            intra_group(hh, i)
            return carry
        lax.fori_loop(0, n_chunks // group, body, 0)

    def inter(c, states):
        rows = pl.ds(pl.multiple_of(c * C, C), C)
        ws_s = [_bdot(ws[hh, rows, :], states[hh]) for hh in H8]
        qd_s = [_bdot(qds[hh, rows, :], states[hh]) for hh in H8]
        v_new = [us[hh, rows, :] - ws_s[hh] for hh in H8]
        for hh in H8:
            osc[hh, rows, :] = qd_s[hh] + _bdot(qks[hh, rows, :], v_new[hh])
        decay = [gl[hh, pl.ds(pl.multiple_of(c * SUBLANES, SUBLANES), 1), :] for hh in H8]
        return tuple(states[hh] * decay[hh] + _bdot(kdt[hh, rows, :], v_new[hh]) for hh in H8)

    s_fin = lax.fori_loop(0, n_chunks, inter, tuple(jnp.zeros((dk, dk), F32) for _ in H8))
    for hh in H8:
        cols = slice(hh * dk, (hh + 1) * dk)
        s_ref[hh] = s_fin[hh]
        o_ref[:, cols] = _gated_out_norm(osc[hh], z_ref[:, cols], nw_ref[...]).astype(o_ref.dtype)


def _gdn_prompt(qkvz, bg, conv_w, norm_w, layer, batch, seq, n_heads, dk):
    H = n_heads
    L = seq
    hb = GDN_HEADS_PER_STEP if H % GDN_HEADS_PER_STEP == 0 else 1
    hg = H // hb
    n_hist = conv_w.shape[1] - 1

    def col(sec):
        return pl.BlockSpec((L, hb * dk), lambda b, h, sec=sec: (b, h + sec * hg))

    def cw(sec):
        return pl.BlockSpec((None, conv_w.shape[1], hb * dk),
                            lambda b, h, sec=sec: (layer, 0, h + sec * hg))

    scr = lambda: pltpu.VMEM((hb, L, dk), F32)
    nbytes = hb * (2 * 4 * L * dk * 4 + 11 * L * dk * 4 + 2 * L * dk * 2) + 2 * L * LANES * 4 \
        + 60 * GDN_CHUNK * GDN_CHUNK * 4
    return pl.pallas_call(
        functools.partial(_gdn_prompt_kernel, n_heads=H),
        out_shape=(jax.ShapeDtypeStruct((batch * L, H * dk), BF16),
                   jax.ShapeDtypeStruct((batch, H, dk, dk), F32))
        + (jax.ShapeDtypeStruct((batch, n_hist, H * dk), F32),) * 3,
        grid=(batch, hg),
        in_specs=[col(0), col(1), col(2), col(3),
                  pl.BlockSpec((L, LANES), lambda b, h: (b, 0)),
                  cw(0), cw(1), cw(2),
                  pl.BlockSpec((None, 1, dk), lambda b, h: (layer, 0, 0))],
        out_specs=(pl.BlockSpec((L, hb * dk), lambda b, h: (b, h)),
                   pl.BlockSpec((None, hb, dk, dk), lambda b, h: (b, h, 0, 0)))
        + (pl.BlockSpec((None, n_hist, hb * dk), lambda b, h: (b, 0, h)),) * 3,
        scratch_shapes=[scr(), scr(), scr(), scr(), scr(),
                        scr(), scr(), scr(), scr(), scr(),
                        pltpu.VMEM((hb, L // GDN_CHUNK * SUBLANES, dk), F32),
                        scr()],
        compiler_params=pltpu.CompilerParams(
            dimension_semantics=("parallel", "parallel"),
            vmem_limit_bytes=_vmem_limit(nbytes)),
        name="gdn_prompt",
    )(qkvz, qkvz, qkvz, qkvz, bg, conv_w, conv_w, conv_w, norm_w)


def _gdn_sample_kernel(q_ref, k_ref, v_ref, z_ref, bg_ref, bq_ref, bk_ref, bv_ref,
                       cwq_ref, cwk_ref, cwv_ref, nw_ref, s_in_ref, *rest, n_heads):
    o_ref, s_out_ref, cq_ref, ck_ref, cv_ref, osc = rest[-6:]
    nb, dk = q_ref.shape
    head = pl.program_id(0)

    def conv_silu(x_ref, buf_ref, new_ref, cw_ref):
        cw = cw_ref[...]
        n_hist = buf_ref.shape[0]
        x = x_ref[...]
        y = x * cw[n_hist:n_hist + 1, :]
        for i in range(n_hist):
            row = buf_ref[i]
            y = y + row * cw[i:i + 1, :]
            if i > 0:
                new_ref[i - 1] = row
        new_ref[n_hist - 1] = x
        return _silu(y)

    q = _l2norm_rows(conv_silu(q_ref, bq_ref, cq_ref, cwq_ref)) * (dk ** -0.5)
    k = _l2norm_rows(conv_silu(k_ref, bk_ref, ck_ref, cwk_ref))
    v = conv_silu(v_ref, bv_ref, cv_ref, cwv_ref)
    beta, g = _head_columns(bg_ref[...], head, n_heads)
    decay = jnp.exp(g)
    kt = jnp.concatenate([k, jnp.zeros((LANES - nb, dk), F32)], axis=0).T if nb < LANES else k.T
    qt = jnp.concatenate([q, jnp.zeros((LANES - nb, dk), F32)], axis=0).T if nb < LANES else q.T
    for b in range(nb):
        s = s_in_ref[b] * decay[b:b + 1, :]
        kcol = kt[:, b:b + 1]
        v_new = (v[b:b + 1, :] - _sum_rows(s * kcol)) * beta[b:b + 1, :]
        s = s + kcol * v_new
        s_out_ref[b] = s
        osc[b:b + 1, :] = _sum_rows(s * qt[:, b:b + 1])
    o_ref[...] = _gated_out_norm(osc[...], z_ref[...], nw_ref[...]).astype(o_ref.dtype)


def _gdn_sample(qkvz, bg, row0, conv_buf, conv_w, norm_w, state, layer, n_heads, dk,
                state_out=None):
    H = n_heads
    nb_total = state.shape[1]
    nb = 16
    assert nb_total % nb == 0 and row0 % nb == 0
    r0 = row0 // nb
    n_hist = conv_buf.shape[1]

    def col(off):
        return pl.BlockSpec((nb, dk), lambda h, i, off=off: (i + r0, h + off))

    def buf(off):
        return pl.BlockSpec((None, n_hist, nb, dk), lambda h, i, off=off: (layer, 0, i, h + off))

    def cw(off):
        return pl.BlockSpec((None, conv_w.shape[1], dk), lambda h, i, off=off: (layer, 0, h + off))

    in_specs = [col(0), col(H), col(2 * H), col(3 * H),
                pl.BlockSpec((nb, LANES), lambda h, i: (i + r0, 0)),
                buf(0), buf(H), buf(2 * H),
                cw(0), cw(H), cw(2 * H),
                pl.BlockSpec((None, 1, dk), lambda h, i: (layer, 0, 0)),
                pl.BlockSpec((None, nb, None, dk, dk), lambda h, i: (layer, i, h, 0, 0))]
    args = [qkvz, qkvz, qkvz, qkvz, bg, conv_buf, conv_buf, conv_buf,
            conv_w, conv_w, conv_w, norm_w, state]
    aliases = {}
    if state_out is not None:
        in_specs.append(pl.BlockSpec(memory_space=pl.ANY))
        args.append(state_out)
        aliases = {len(args) - 1: 1}
    nbytes = 4 * nb * dk * dk * 4 + 64 * nb * dk * 4 + 64 * dk * dk * 4
    return pl.pallas_call(
        functools.partial(_gdn_sample_kernel, n_heads=H),
        out_shape=(jax.ShapeDtypeStruct((nb_total, H * dk), BF16),
                   jax.ShapeDtypeStruct(state.shape, F32))
        + (jax.ShapeDtypeStruct((n_hist, nb_total, H * dk), F32),) * 3,
        grid=(H, nb_total // nb),
        in_specs=in_specs,
        out_specs=(pl.BlockSpec((nb, dk), lambda h, i: (i, h)),
                   pl.BlockSpec((None, nb, None, dk, dk), lambda h, i: (layer, i, h, 0, 0)))
        + (pl.BlockSpec((n_hist, nb, dk), lambda h, i: (0, i, h)),) * 3,
        scratch_shapes=[pltpu.VMEM((nb, dk), F32)],
        input_output_aliases=aliases,
        compiler_params=pltpu.CompilerParams(
            dimension_semantics=("parallel", "parallel"),
            vmem_limit_bytes=_vmem_limit(nbytes)),
        name="gdn_sample",
    )(*args)


def _s5_tables(lam_re, lam_im, log_dt, b_re, b_im, c_re, c_im):
    D, G, P = lam_re.shape
    gc = b_re.shape[-1]
    gpb = LANES // gc
    nblk = G // gpb
    dt = jnp.exp(log_dt)[..., None]
    mag = jnp.exp(lam_re * dt)
    ar = mag * jnp.cos(lam_im * dt)
    ai = mag * jnp.sin(lam_im * dt)
    nr = ar - 1.0
    den = lam_re * lam_re + lam_im * lam_im
    fr = (nr * lam_re + ai * lam_im) / den
    fi = (ai * lam_re - nr * lam_im) / den
    bbar_re = fr[..., None] * b_re - fi[..., None] * b_im
    bbar_im = fr[..., None] * b_im + fi[..., None] * b_re
    rg = lax.broadcasted_iota(jnp.int32, (gpb * gc, gpb * P), 0) // gc
    cg = lax.broadcasted_iota(jnp.int32, (gpb * gc, gpb * P), 1) // P
    diag = rg == cg

    def bmat(bb):
        t = jnp.swapaxes(bb.reshape(D, nblk, gpb * P, gc), 2, 3)
        return jnp.where(diag, jnp.tile(t, (1, 1, gpb, 1)), 0.0)

    def cmat(cc):
        t = jnp.swapaxes(cc.reshape(D, nblk, gpb, gc, P), 3, 4).reshape(D, nblk, gpb * P, gc)
        return jnp.where(diag.T, jnp.tile(t, (1, 1, 1, gpb)), 0.0)

    b_blk = jnp.concatenate([bmat(bbar_re), bmat(bbar_im)], axis=3).astype(BF16)
    c_blk = jnp.concatenate([cmat(c_re), -cmat(c_im)], axis=2).astype(BF16)
    return (b_blk, c_blk, ar.reshape(D, nblk, 1, gpb * P), ai.reshape(D, nblk, 1, gpb * P))


def _s5_prompt_kernel(u_ref, bblk_ref, cblk_ref, ar_ref, ai_ref, d_ref,
                      g5_ref, g5b_ref, xre_ref, xim_ref, up, xs, ys):
    L = u_ref.shape[0]
    ns = ar_ref.shape[-1]
    nseg = S5_SEGMENTS
    seg = L // nseg

    def slab(t):
        return pl.ds(pl.multiple_of(t * nseg, nseg), nseg)

    def gather(t, carry):
        up[slab(t), :] = u_ref[pl.ds(t, nseg, stride=seg), :]
        return carry

    lax.fori_loop(0, seg, gather, 0, unroll=8)
    u = up[...]
    xs[...] = jnp.dot(u.astype(BF16), bblk_ref[...], preferred_element_type=F32)
    ar = jnp.broadcast_to(ar_ref[...], (nseg, ns))
    ai = jnp.broadcast_to(ai_ref[...], (nseg, ns))

    def step(x, t):
        xr, xi = x
        r = xs[slab(t), :]
        return (ar * xr - ai * xi + r[:, :ns], ar * xi + ai * xr + r[:, ns:])

    zero = jnp.zeros((nseg, ns), F32)
    er, ei = lax.fori_loop(0, seg, lambda t, x: step(x, t), (zero, zero), unroll=8)
    pr, pi = ar_ref[...], ai_ref[...]
    for _ in range(seg.bit_length() - 1):
        pr, pi = pr * pr - pi * pi, 2.0 * pr * pi
    assert seg == 1 << (seg.bit_length() - 1)
    cr = [jnp.zeros((1, ns), F32)]
    ci = [jnp.zeros((1, ns), F32)]
    for s in range(nseg - 1):
        cr.append(er[s:s + 1] + pr * cr[s] - pi * ci[s])
        ci.append(ei[s:s + 1] + pr * ci[s] + pi * cr[s])
    x0 = (jnp.concatenate(cr, axis=0), jnp.concatenate(ci, axis=0))

    def step_store(t, x):
        xr, xi = step(x, t)
        xs[slab(t), :] = jnp.concatenate([xr, xi], axis=1)
        return (xr, xi)

    xr, xi = lax.fori_loop(0, seg, step_store, x0, unroll=8)
    xre_ref[...] = xr[nseg - 1:nseg]
    xim_ref[...] = xi[nseg - 1:nseg]
    y = jnp.dot(xs[...].astype(BF16), cblk_ref[...], preferred_element_type=F32) + d_ref[...] * u
    ys[...] = _gelu_tanh(y)
    for s in range(nseg):
        g5 = ys[pl.ds(s, seg, stride=nseg), :]
        g5_ref[s * seg:(s + 1) * seg, :] = g5
        g5b_ref[s * seg:(s + 1) * seg, :] = g5.astype(BF16)


def _s5_prompt(u_all, tables, d_skip, layer, batch, seq, n_ch):
    b_blk, c_blk, ar, ai = tables
    nblk, _, ns2 = b_blk.shape[1:]
    ns = ns2 // 2
    L = seq
    nbytes = (2 * L * LANES * 4 + 2 * L * LANES * 6 + 3 * L * ns2 * 4 + L * ns2 * 2
              + 8 * LANES * ns2 * 2)
    blkp = lambda r, c: pl.BlockSpec((None, None, r, c), lambda b, j: (layer, j, 0, 0))
    return pl.pallas_call(
        _s5_prompt_kernel,
        out_shape=(jax.ShapeDtypeStruct((batch * L, n_ch), F32),
                   jax.ShapeDtypeStruct((batch * L, n_ch), BF16),
                   jax.ShapeDtypeStruct((batch, 1, nblk * ns), F32),
                   jax.ShapeDtypeStruct((batch, 1, nblk * ns), F32)),
        grid=(batch, nblk),
        in_specs=[pl.BlockSpec((L, LANES), lambda b, j: (b, j)),
                  blkp(LANES, ns2), blkp(ns2, LANES), blkp(1, ns), blkp(1, ns),
                  pl.BlockSpec((None, 1, LANES), lambda b, j: (layer, 0, j))],
        out_specs=(pl.BlockSpec((L, LANES), lambda b, j: (b, j)),
                   pl.BlockSpec((L, LANES), lambda b, j: (b, j)),
                   pl.BlockSpec((None, 1, ns), lambda b, j: (b, 0, j)),
                   pl.BlockSpec((None, 1, ns), lambda b, j: (b, 0, j))),
        scratch_shapes=[pltpu.VMEM((L, LANES), F32),
                        pltpu.VMEM((L, ns2), F32),
                        pltpu.VMEM((L, LANES), F32)],
        compiler_params=pltpu.CompilerParams(
            dimension_semantics=("parallel", "parallel"),
            vmem_limit_bytes=_vmem_limit(nbytes)),
        name="s5_prompt",
    )(u_all, b_blk, c_blk, ar, ai, d_skip)


def _s5_sample_kernel(u_ref, bblk_ref, cblk_ref, ar_ref, ai_ref, d_ref, x0r_ref, x0i_ref,
                      g5_ref, g5b_ref, xre_ref, xim_ref):
    ns = ar_ref.shape[-1]
    u = u_ref[...]
    bu = jnp.dot(u.astype(BF16), bblk_ref[...], preferred_element_type=F32)
    ar, ai = ar_ref[...], ai_ref[...]
    x0r, x0i = x0r_ref[...], x0i_ref[...]
    xr = ar * x0r - ai * x0i + bu[:, :ns]
    xi = ar * x0i + ai * x0r + bu[:, ns:]
    xre_ref[...] = xr
    xim_ref[...] = xi
    x = jnp.concatenate([xr, xi], axis=1)
    y = jnp.dot(x.astype(BF16), cblk_ref[...], preferred_element_type=F32) + d_ref[...] * u
    g5 = _gelu_tanh(y)
    g5_ref[...] = g5
    g5b_ref[...] = g5.astype(BF16)


def _s5_sample(u_all, row0, tables, d_skip, x0_re, x0_im, layer, n_ch):
    b_blk, c_blk, ar, ai = tables
    nblk, _, ns2 = b_blk.shape[1:]
    ns = ns2 // 2
    nb = x0_re.shape[1]
    assert row0 % nb == 0
    r0 = row0 // nb
    blkp = lambda r, c: pl.BlockSpec((None, None, r, c), lambda j: (layer, j, 0, 0))
    nbytes = 16 * nb * ns2 * 4 + 8 * LANES * ns2 * 2
    return pl.pallas_call(
        _s5_sample_kernel,
        out_shape=(jax.ShapeDtypeStruct((nb, n_ch), F32),
                   jax.ShapeDtypeStruct((nb, n_ch), BF16),
                   jax.ShapeDtypeStruct((nb, nblk * ns), F32),
                   jax.ShapeDtypeStruct((nb, nblk * ns), F32)),
        grid=(nblk,),
        in_specs=[pl.BlockSpec((nb, LANES), lambda j: (r0, j)),
                  blkp(LANES, ns2), blkp(ns2, LANES), blkp(1, ns), blkp(1, ns),
                  pl.BlockSpec((None, 1, LANES), lambda j: (layer, 0, j)),
                  pl.BlockSpec((None, nb, ns), lambda j: (layer, 0, j)),
                  pl.BlockSpec((None, nb, ns), lambda j: (layer, 0, j))],
        out_specs=(pl.BlockSpec((nb, LANES), lambda j: (0, j)),
                   pl.BlockSpec((nb, LANES), lambda j: (0, j)),
                   pl.BlockSpec((nb, ns), lambda j: (0, j)),
                   pl.BlockSpec((nb, ns), lambda j: (0, j))),
        compiler_params=pltpu.CompilerParams(
            dimension_semantics=("parallel",), vmem_limit_bytes=_vmem_limit(nbytes)),
        name="s5_sample",
    )(u_all, b_blk, c_blk, ar, ai, d_skip, x0_re, x0_im)


def kernel(x_prompt, x_sample, state_dn_conv, state_dn_ssm, state_s5_re, state_s5_im, norm1, w_in, dn_conv_w, dn_a_log, dn_dt_bias, dn_norm_w, w_br_dn, s5_lam_re, s5_lam_im, s5_log_dt, s5_b_re, s5_b_im, s5_c_re, s5_c_im, s5_d, w_glu, w_br_s5, w_out, norm2, w_ffn_gate, w_ffn_up, w_ffn_down, norm_f):
    batch, seq, d_model = x_prompt.shape
    nb, dec_seq, _ = x_sample.shape
    assert dec_seq == 1
    depth, _, n_heads, dk, dv = state_dn_ssm.shape
    assert dk == LANES and dv == LANES and seq % GDN_CHUNK == 0 and seq % S5_SEGMENTS == 0
    qk_dim = n_heads * dk
    conv_ch = dn_conv_w.shape[2]
    assert conv_ch == 3 * qk_dim
    n_ch = s5_d.shape[1]
    n_groups, n_state = s5_lam_re.shape[1:]
    ffn = w_ffn_gate.shape[2]
    mp = batch * seq
    m = mp + nb
    z_end = 4 * qk_dim
    rest0 = z_end + 2 * n_heads
    assert w_in.shape[2] == rest0 + n_ch + 2 * d_model and 2 * n_heads <= LANES

    x = (x_prompt.reshape(mp, d_model), x_sample.reshape(nb, d_model))

    w_in_t = jnp.swapaxes(w_in, 1, 2)
    w_bg = jnp.pad(w_in_t[:, z_end:rest0, :], ((0, 0), (0, LANES - 2 * n_heads), (0, 0)))
    w_s5 = w_in_t[:, rest0:rest0 + n_ch, :]
    gate_tn = 512
    gate0 = rest0 + n_ch
    gate_base = gate0 - gate0 % gate_tn
    gate_shift = gate0 - gate_base
    gate_cols = -(-(gate_shift + 2 * d_model) // gate_tn) * gate_tn
    assert gate_shift < LANES and gate_base + gate_cols - gate_tn < w_in.shape[2]
    conv_hist = jnp.swapaxes(state_dn_conv, 1, 2)
    pad_heads = lambda a: jnp.pad(a, ((0, 0), (n_heads, LANES - 2 * n_heads)))[:, None, :]
    alog_pad = pad_heads(dn_a_log)
    dtb_pad = pad_heads(dn_dt_bias)
    x0_re = state_s5_re.reshape(depth, nb, n_groups * n_state)
    x0_im = state_s5_im.reshape(depth, nb, n_groups * n_state)

    tm = _pick_tile(m, 1664, 64)
    norm_w3 = dn_norm_w[:, None, :]
    d_skip3 = s5_d[:, None, :]
    tables = _s5_tables(s5_lam_re, s5_lam_im, s5_log_dt, s5_b_re, s5_b_im, s5_c_re, s5_c_im)
    outs = {k: [] for k in ("p_conv", "p_ssm", "p_re", "p_im", "s_conv", "s_re", "s_im")}
    s_ssm = None
    for l in range(depth):
        h = _rmsnorm(x, norm1[l][None, :], BF16)
        qkvz = _fused_matmul([h], [(0, w_in_t, l, 0, True)], [], _ep_identity, z_end, F32,
                             tm=tm, tn=512, name="in_qkvz")
        bg = _fused_matmul([h], [(0, w_bg, l, 0, True)], [], _ep_beta_decay(n_heads), LANES, F32,
                           tm=tm, tn=LANES, name="in_bg",
                           col_params=[(alog_pad, l), (dtb_pad, l)])
        s5u = _fused_matmul([h], [(0, w_s5, l, 0, True)], [], _ep_identity, n_ch, F32,
                            tm=tm, tn=512, name="in_s5")
        gates = _fused_matmul([h], [(0, w_in_t, l, gate_base, True)], [], _ep_identity, gate_cols, F32,
                              tm=tm, tn=gate_tn, name="in_gates")

        o_p, ssm_p, *pc = _gdn_prompt(qkvz, bg, dn_conv_w, norm_w3, l, batch, seq, n_heads, dk)
        o_s, s_ssm, *sc = _gdn_sample(qkvz, bg, mp, conv_hist, dn_conv_w, norm_w3,
                                      state_dn_ssm, l, n_heads, dk, state_out=s_ssm)
        outs["p_conv"].append(jnp.concatenate(pc, axis=-1))
        outs["s_conv"].append(jnp.concatenate(sc, axis=-1))
        outs["p_ssm"].append(ssm_p)

        g5_p, g5b_p, re_p, im_p = _s5_prompt(s5u, tables, d_skip3, l, batch, seq, n_ch)
        g5_s, g5b_s, re_s, im_s = _s5_sample(s5u, mp, tables, d_skip3, x0_re, x0_im, l, n_ch)
        outs["p_re"].append(re_p.reshape(batch, n_groups, n_state))
        outs["p_im"].append(im_p.reshape(batch, n_groups, n_state))
        outs["s_re"].append(re_s.reshape(nb, n_groups, n_state))
        outs["s_im"].append(im_s.reshape(nb, n_groups, n_state))
        g5g = _fused_matmul([(g5b_p, g5b_s)], [(0, w_glu, l, 0)], [((g5_p, g5_s), 0)],
                            _ep_glu_self, n_ch, BF16, tm=tm, tn=512, name="s5_glu")

        merged = _fused_matmul([(o_p, o_s), g5g], [(0, w_br_dn, l, 0), (1, w_br_s5, l, 0)],
                               [(gates, gate_shift), (gates, gate_shift + d_model)],
                               _ep_gated_merge,
                               d_model, BF16, tm=tm, tn=512, name="branch_merge")
        x = _fused_matmul([merged], [(0, w_out, l, 0)], [(x, 0)], _ep_residual, d_model, F32,
                          tm=tm, tn=512, name="out_proj")

        h2 = _rmsnorm(x, norm2[l][None, :], BF16)
        hmid = _fused_matmul([h2], [(0, w_ffn_gate, l, 0), (0, w_ffn_up, l, 0)], [], _ep_swiglu,
                             ffn, BF16, tm=tm, tn=256, name="ffn_up")
        x = _matmul_residual_wstat(hmid, w_ffn_down, l, x, tm=_pick_tile(m, 416, 16), tn=512,
                                   name="ffn_down")

    y_p, y_s = _rmsnorm(x, norm_f[None, :], F32, split_rows=(mp, nb))
    st = lambda k: jnp.stack(outs[k])
    return (y_p.reshape(batch, seq, d_model), y_s.reshape(nb, 1, d_model),
            st("p_conv"), st("p_ssm"), st("p_re"), st("p_im"),
            jnp.swapaxes(st("s_conv"), 1, 2), s_ssm, st("s_re"), st("s_im"))
```

```python
import functools
import math

import jax
import jax.numpy as jnp
from jax import lax
from jax.experimental import pallas as pl
from jax.experimental.pallas import tpu as pltpu

F32 = jnp.float32
BF16 = jnp.bfloat16

NORM_EPS = 1e-6
L2_EPS = 1e-6
LANES = 128
SUBLANES = 8
VMEM_CAP_BYTES = 56 * 1024 * 1024
GDN_CHUNK = 128
S5_SEGMENTS = SUBLANES
GDN_HEADS_PER_STEP = 2
GDN_CHUNKS_PER_GROUP = 16
S5_BLOCK_STEPS = 32


def _vmem_limit(nbytes):
    return int(min(VMEM_CAP_BYTES, nbytes * 5 // 4 + (4 << 20)))


def _pick_tile(n, target, mult):
    best = None
    for t in range(mult, min(n, target) + 1, mult):
        if n % t == 0:
            best = t
    return best if best is not None else n


def _sigmoid(x):
    return 1.0 / (1.0 + jnp.exp(-x))


def _silu(x):
    return x * _sigmoid(x)


def _softplus(x):
    return jnp.maximum(x, 0.0) + jnp.log1p(jnp.exp(-jnp.abs(x)))


def _gelu_tanh(x):
    c = math.sqrt(2.0 / math.pi)
    return 0.5 * x * (1.0 + jnp.tanh(c * (x + 0.044715 * (x * x * x))))


def _bdot(a, b):
    return jnp.dot(a.astype(BF16), b.astype(BF16), preferred_element_type=F32)


def _bdot_nt(a, b):
    return lax.dot_general(a.astype(BF16), b.astype(BF16), (((1,), (1,)), ((), ())),
                           preferred_element_type=F32)


def _rows_of(op):
    return op[0].shape[0] + op[1].shape[0] if isinstance(op, tuple) else op.shape[0]


def _row_specs(op, tm, ncols, index_map):
    if not isinstance(op, tuple):
        return [pl.BlockSpec((tm, ncols), index_map)], [op], None
    p, s = op
    tail = p.shape[0] % tm
    assert tail + s.shape[0] == tm and tail % 16 == 0, (p.shape, s.shape, tm)

    def s_map(*idx):
        return (0,) + tuple(index_map(*idx)[1:])

    return ([pl.BlockSpec((tm, ncols), index_map), pl.BlockSpec((s.shape[0], ncols), s_map)],
            [p, s], tail)


def _load_rows(refs, tail, last):
    if tail is None or not last:
        return refs[0][...]
    return jnp.concatenate([refs[0][:tail, :], refs[1][...]], axis=0)


def _lane_window(parts, shift, width):
    x = parts[0] if len(parts) == 1 else jnp.concatenate(parts, axis=1)
    if shift:
        x = pltpu.roll(x, x.shape[1] - shift, 1)
    return x[:, :width]


def _on_row_tiles(i, n_tiles, any_split, body):
    if not any_split:
        body(False)
        return
    if n_tiles > 1:
        pl.when(i < n_tiles - 1)(lambda: body(False))
    pl.when(i == n_tiles - 1)(lambda: body(True))


def _rmsnorm_kernel(*refs, tail, n_tiles, out_tail):
    n_out = 1 if out_tail is None else 2
    x_refs, w_ref, o_refs = refs[:-1 - n_out], refs[-1 - n_out], refs[-n_out:]

    def body(last):
        x = _load_rows(x_refs, tail, last)
        y = x * lax.rsqrt(jnp.mean(x * x, axis=-1, keepdims=True) + NORM_EPS)
        y = (y * w_ref[...]).astype(o_refs[0].dtype)
        if out_tail is None:
            o_refs[0][...] = y
        elif not last:
            o_refs[0][...] = y
        else:
            o_refs[0][:out_tail, :] = y[:out_tail]
            o_refs[1][...] = y[out_tail:]

    _on_row_tiles(pl.program_id(0), n_tiles, tail is not None or out_tail is not None, body)


def _rmsnorm(x, w_row, out_dtype, split_rows=None):
    m = _rows_of(x)
    d = w_row.shape[1]
    tr = _pick_tile(m, 832, 64)
    specs, arrs, tail = _row_specs(x, tr, d, lambda i: (i, 0))
    nbytes = 2 * tr * d * 4 + 2 * tr * d * jnp.dtype(out_dtype).itemsize + 3 * tr * d * 4
    if split_rows is None:
        out_shape = jax.ShapeDtypeStruct((m, d), out_dtype)
        out_specs = pl.BlockSpec((tr, d), lambda i: (i, 0))
        out_tail = None
    else:
        mp, nb = split_rows
        out_tail = mp % tr
        assert mp + nb == m and out_tail + nb == tr
        out_shape = (jax.ShapeDtypeStruct((mp, d), out_dtype),
                     jax.ShapeDtypeStruct((nb, d), out_dtype))
        out_specs = (pl.BlockSpec((tr, d), lambda i: (i, 0)),
                     pl.BlockSpec((nb, d), lambda i: (0, 0)))
    return pl.pallas_call(
        functools.partial(_rmsnorm_kernel, tail=tail, n_tiles=m // tr, out_tail=out_tail),
        out_shape=out_shape,
        grid=(m // tr,),
        in_specs=specs + [pl.BlockSpec((1, d), lambda i: (0, 0))],
        out_specs=out_specs,
        compiler_params=pltpu.CompilerParams(
            dimension_semantics=("arbitrary",), vmem_limit_bytes=_vmem_limit(nbytes)),
        name="rmsnorm",
    )(*arrs, w_row)


def _mm_kernel(*refs, a_idx, w_transposed, a_groups, e_groups, n_tiles, epilogue, normed):
    pos = 0
    a_refs = []
    for n, _ in a_groups:
        a_refs.append(refs[pos:pos + n])
        pos += n
    w_refs = refs[pos:pos + len(a_idx)]
    pos += len(a_idx)
    e_refs = []
    for n, _, _ in e_groups:
        e_refs.append(refs[pos:pos + n])
        pos += n
    if normed:
        nw_ref, o_ref, h_scr = refs[pos:pos + 3]
    else:
        o_ref = refs[pos]
    any_split = any(g[1] is not None for g in a_groups + e_groups)

    def load_extra(r, tail, shift, last):
        if shift:
            return _lane_window([r[0][...], r[1][...]], shift, o_ref.shape[1])
        return _load_rows(r, tail, last)

    def body(last):
        if normed:
            @pl.when(pl.program_id(1) == 0)
            def _():
                x = _load_rows(a_refs[0], a_groups[0][1], last)
                y = x * lax.rsqrt(jnp.mean(x * x, axis=-1, keepdims=True) + NORM_EPS)
                h_scr[...] = (y * nw_ref[...]).astype(BF16)

            a_vals = [h_scr[...]]
        else:
            a_vals = [_load_rows(r, t, last) for r, (_, t) in zip(a_refs, a_groups)]
        parts = [(_bdot_nt if wt else _bdot)(a_vals[ai], w[...])
                 for ai, wt, w in zip(a_idx, w_transposed, w_refs)]
        e_vals = [load_extra(r, t, sh, last) for r, (_, t, sh) in zip(e_refs, e_groups)]
        o_ref[...] = epilogue(parts, e_vals).astype(o_ref.dtype)

    _on_row_tiles(pl.program_id(0), n_tiles, any_split, body)


def _fused_matmul(a_list, w_list, extras, epilogue, n_out, out_dtype, *, tm, tn, name,
                  col_params=(), norm_w=None):
    m = _rows_of(a_list[0])
    assert m % tm == 0 and n_out % tn == 0
    in_specs, args, a_groups, e_groups = [], [], [], []
    kdims = []
    for a in a_list:
        kd = (a[0] if isinstance(a, tuple) else a).shape[1]
        specs, arrs, tail = _row_specs(a, tm, kd, lambda i, j: (i, 0))
        assert _rows_of(a) == m
        in_specs += specs
        args += arrs
        a_groups.append((len(arrs), tail))
        kdims.append(kd)
    w_list = [tuple(e) + (False,) * (5 - len(e)) for e in w_list]
    for ai, w, layer, col0, transposed in w_list:
        assert col0 % tn == 0 and w.shape[2 if transposed else 1] == kdims[ai]
        if transposed:
            spec = pl.BlockSpec((None, tn, w.shape[2]),
                                lambda i, j, layer=layer, off=col0 // tn: (layer, j + off, 0))
        else:
            spec = pl.BlockSpec((None, w.shape[1], tn),
                                lambda i, j, layer=layer, off=col0 // tn: (layer, 0, j + off))
        in_specs.append(spec)
        args.append(w)
    for e, col0 in extras:
        shift = col0 % tn
        assert shift < LANES and _rows_of(e) == m
        specs, arrs, tail = _row_specs(e, tm, tn, lambda i, j, off=col0 // tn: (i, j + off))
        if shift:
            assert not isinstance(e, tuple) and tn % LANES == 0
            specs.append(pl.BlockSpec(
                (tm, LANES), lambda i, j, off=col0 // tn: (i, (j + off + 1) * (tn // LANES))))
            arrs.append(e)
        in_specs += specs
        args += arrs
        e_groups.append((len(arrs), tail, shift))
    for p, layer in col_params:
        in_specs.append(pl.BlockSpec((None, 1, tn), lambda i, j, layer=layer: (layer, 0, j)))
        args.append(p)
        e_groups.append((1, None, 0))
    normed = norm_w is not None
    scratch = []
    if normed:
        assert len(a_list) == 1
        in_specs.append(pl.BlockSpec((1, kdims[0]), lambda i, j: (0, 0)))
        args.append(norm_w)
        scratch.append(pltpu.VMEM((tm, kdims[0]), BF16))
    osz = jnp.dtype(out_dtype).itemsize
    nbytes = (sum(3 * tm * kd * (6 if normed else 2) for kd in kdims)
              + sum(kdims[ai] * tn * (2 * w.dtype.itemsize + 2) for ai, w, _, _, _ in w_list)
              + sum(3 * tm * tn * 4 for _ in extras)
              + 2 * tm * tn * osz + (2 + len(w_list)) * tm * tn * 4)
    kern = functools.partial(_mm_kernel, a_idx=tuple(e[0] for e in w_list),
                             w_transposed=tuple(e[4] for e in w_list), a_groups=tuple(a_groups), e_groups=tuple(e_groups),
                             n_tiles=m // tm, epilogue=epilogue, normed=normed)
    return pl.pallas_call(
        kern,
        out_shape=jax.ShapeDtypeStruct((m, n_out), out_dtype),
        grid=(m // tm, n_out // tn),
        in_specs=in_specs,
        out_specs=pl.BlockSpec((tm, tn), lambda i, j: (i, j)),
        scratch_shapes=scratch,
        compiler_params=pltpu.CompilerParams(
            dimension_semantics=("parallel", "arbitrary" if normed else "parallel"),
            vmem_limit_bytes=_vmem_limit(nbytes)),
        name=name,
    )(*args)


def _mm_residual_wstat_kernel(a_ref, w_ref, x_ref, o_ref, wb):
    @pl.when(pl.program_id(1) == 0)
    def _():
        wb[...] = w_ref[...].astype(BF16)

    o_ref[...] = x_ref[...] + jnp.dot(a_ref[...], wb[...], preferred_element_type=F32)


def _matmul_residual_wstat(a, w, layer, x, *, tm, tn, name):
    m, kdim = a.shape
    n_out = w.shape[2]
    assert m % tm == 0 and n_out % tn == 0 and w.shape[1] == kdim
    nbytes = 2 * tm * kdim * 2 + kdim * tn * (2 * w.dtype.itemsize + 2) + 6 * tm * tn * 4
    return pl.pallas_call(
        _mm_residual_wstat_kernel,
        out_shape=jax.ShapeDtypeStruct((m, n_out), F32),
        grid=(n_out // tn, m // tm),
        in_specs=[pl.BlockSpec((tm, kdim), lambda j, i: (i, 0)),
                  pl.BlockSpec((None, kdim, tn), lambda j, i: (layer, 0, j)),
                  pl.BlockSpec((tm, tn), lambda j, i: (i, j))],
        out_specs=pl.BlockSpec((tm, tn), lambda j, i: (i, j)),
        scratch_shapes=[pltpu.VMEM((kdim, tn), BF16)],
        compiler_params=pltpu.CompilerParams(
            dimension_semantics=("parallel", "arbitrary"),
            vmem_limit_bytes=_vmem_limit(nbytes)),
        name=name,
    )(a, w, x)


def _ep_identity(accs, extras):
    return accs[0]


def _ep_residual(accs, extras):
    return extras[0] + accs[0]


def _ep_swiglu(accs, extras):
    return _silu(accs[0]) * accs[1]


def _ep_glu_self(accs, extras):
    g5 = extras[0]
    return g5 * _sigmoid(accs[0])


def _ep_gated_merge(accs, extras):
    return _sigmoid(extras[0]) * accs[0] + _sigmoid(extras[1]) * accs[1]


def _ep_beta_decay(n_heads):
    def ep(accs, extras):
        acc = accs[0]
        alog_row, dtb_row = extras
        lane = lax.broadcasted_iota(jnp.int32, acc.shape, 1)
        return jnp.where(lane < n_heads, _sigmoid(acc),
                         -jnp.exp(alog_row) * _softplus(acc + dtb_row))
    return ep


def _head_columns(bg, head, n_heads):
    lane = lax.broadcasted_iota(jnp.int32, bg.shape, 1)
    beta = jnp.sum(jnp.where(lane == head, bg, 0.0), axis=-1, keepdims=True)
    g = jnp.sum(jnp.where(lane == head + n_heads, bg, 0.0), axis=-1, keepdims=True)
    return beta, g


def _l2norm_rows(x):
    return x * lax.rsqrt(jnp.sum(x * x, axis=-1, keepdims=True) + L2_EPS)


def _sum_rows(x):
    acc = x[0:SUBLANES]
    for i in range(1, x.shape[0] // SUBLANES):
        acc = acc + x[i * SUBLANES:(i + 1) * SUBLANES]
    shift = SUBLANES // 2
    while shift:
        acc = acc + pltpu.roll(acc, shift, 0)
        shift //= 2
    return acc[0:1]


def _gated_out_norm(o, z, nw_row):
    y = o * lax.rsqrt(jnp.mean(o * o, axis=-1, keepdims=True) + NORM_EPS)
    return y * nw_row * _silu(z)


def _gdn_prompt_kernel(q_ref, k_ref, v_ref, z_ref, bg_ref, cwq_ref, cwk_ref, cwv_ref, nw_ref,
                       o_ref, s_ref, pcq_ref, pck_ref, pcv_ref,
                       qn, kn, vn, gb, bb, us, ws, qks, qds, kdt, gl, osc, *, n_heads):
    hb, L, dk = qn.shape
    n_hist = pcq_ref.shape[0]
    for x_ref, pc_ref in ((q_ref, pcq_ref), (k_ref, pck_ref), (v_ref, pcv_ref)):
        pc_ref[...] = x_ref[L - n_hist:L, :]
    C = GDN_CHUNK
    n_chunks = L // C
    H8 = range(hb)

    row8 = lax.broadcasted_iota(jnp.int32, (SUBLANES, dk), 0)

    def conv_silu(x_ref, cw_ref, cols):
        cw = cw_ref[:, cols]
        n_taps = cw.shape[0]
        assert n_taps - 1 <= SUBLANES
        tap = lambda j: cw[n_taps - 1 - j:n_taps - j, :]
        head8 = x_ref[0:SUBLANES, cols]
        lo = head8 * tap(0)
        hi = x_ref[SUBLANES:L, cols] * tap(0)
        for j in range(1, n_taps):
            lo = lo + jnp.where(row8 >= j, pltpu.roll(head8, j, 0), 0.0) * tap(j)
            hi = hi + x_ref[pl.ds(SUBLANES - j, L - SUBLANES), cols] * tap(j)
        return _silu(jnp.concatenate([lo, hi], axis=0))

    bg = bg_ref[...]
    for hh in H8:
        cols = slice(hh * dk, (hh + 1) * dk)
        qn[hh] = _l2norm_rows(conv_silu(q_ref, cwq_ref, cols)) * (dk ** -0.5)
        kn[hh] = _l2norm_rows(conv_silu(k_ref, cwk_ref, cols))
        vn[hh] = conv_silu(v_ref, cwv_ref, cols)
        beta, g = _head_columns(bg, pl.program_id(1) * hb + hh, n_heads)
        bb[hh] = jnp.broadcast_to(beta, (L, dk))
        gb[hh] = jnp.broadcast_to(g, (L, dk))

    ri = lax.broadcasted_iota(jnp.int32, (C, C), 0)
    ci = lax.broadcasted_iota(jnp.int32, (C, C), 1)
    causal = ri >= ci
    strict = ri > ci
    tri_incl = jnp.where(causal, 1.0, 0.0).astype(F32)
    eye = jnp.where(ri == ci, 1.0, 0.0).astype(F32)
    level_masks = []
    n = 1
    while n < C:
        sh = n.bit_length() - 1
        same_2n = (ri >> (sh + 1)) == (ci >> (sh + 1))
        diff_n = (ri >> sh) != (ci >> sh)
        level_masks.append(jnp.where(same_2n & diff_n & strict, 1.0, 0.0).astype(F32))
        n *= 2

    group = math.gcd(n_chunks, GDN_CHUNKS_PER_GROUP)

    def intra_group(hh, i):
        G = range(group)
        rows = [pl.ds(pl.multiple_of((i * group + j) * C, C), C) for j in G]
        k = [kn[hh, r, :] for r in rows]
        bet = [bb[hh, r, :] for r in rows]
        gcb = [jnp.dot(tri_incl, gb[hh, r, :], precision=lax.Precision.HIGHEST,
                       preferred_element_type=F32) for r in rows]
        gamma = [jnp.where(causal, jnp.exp(jnp.minimum(g - g.T, 0.0)), 0.0) for g in gcb]
        kb = [k[j] * bet[j] for j in G]
        a_mat = [jnp.where(strict, _bdot_nt(kb[j], k[j]) * gamma[j], 0.0) for j in G]
        q = [qn[hh, r, :] for r in rows]
        for j in G:
            qks[hh, rows[j], :] = _bdot_nt(q[j], k[j]) * gamma[j]
        t = [eye - a * level_masks[0] for a in a_mat]
        for m in level_masks[1:]:
            x = [_bdot(a_mat[j] * m, t[j]) for j in G]
            t = [t[j] - _bdot(t[j], x[j]) for j in G]
        eg = [jnp.exp(g) for g in gcb]
        for j in G:
            us[hh, rows[j], :] = _bdot(t[j], vn[hh, rows[j], :] * bet[j])
        for j in G:
            ws[hh, rows[j], :] = _bdot(t[j], kb[j] * eg[j])
        for j in G:
            qds[hh, rows[j], :] = q[j] * eg[j]
            g_last = gcb[j][C - 1:C, :]
            kdt[hh, rows[j], :] = (k[j] * jnp.exp(g_last - gcb[j])).T
            gl[hh, pl.ds(pl.multiple_of((i * group + j) * SUBLANES, SUBLANES), SUBLANES), :] = (
                jnp.broadcast_to(jnp.exp(g_last), (SUBLANES, dk)))

    for hh in H8:
        if n_chunks == group:
            intra_group(hh, 0)
        else:
            def body(i, carry, hh=hh):
                intra_group(hh, i)
                return carry
            lax.fori_loop(0, n_chunks // group, body, 0)

    def inter(c, states):
        rows = pl.ds(pl.multiple_of(c * C, C), C)
        ws_s = [_bdot(ws[hh, rows, :], states[hh]) for hh in H8]
        qd_s = [_bdot(qds[hh, rows, :], states[hh]) for hh in H8]
        v_new = [us[hh, rows, :] - ws_s[hh] for hh in H8]
        for hh in H8:
            osc[hh, rows, :] = qd_s[hh] + _bdot(qks[hh, rows, :], v_new[hh])
        decay = [gl[hh, pl.ds(pl.multiple_of(c * SUBLANES, SUBLANES), 1), :] for hh in H8]
        return tuple(states[hh] * decay[hh] + _bdot(kdt[hh, rows, :], v_new[hh]) for hh in H8)

    s_fin = lax.fori_loop(0, n_chunks, inter, tuple(jnp.zeros((dk, dk), F32) for _ in H8))
    for hh in H8:
        cols = slice(hh * dk, (hh + 1) * dk)
        s_ref[hh] = s_fin[hh]
        o_ref[:, cols] = _gated_out_norm(osc[hh], z_ref[:, cols], nw_ref[...]).astype(o_ref.dtype)


def _gdn_prompt(qkvz, bg, conv_w, norm_w, layer, batch, seq, n_heads, dk):
    H = n_heads
    L = seq
    hb = GDN_HEADS_PER_STEP if H % GDN_HEADS_PER_STEP == 0 else 1
    hg = H // hb
    n_hist = conv_w.shape[1] - 1

    def col(sec):
        return pl.BlockSpec((L, hb * dk), lambda b, h, sec=sec: (b, h + sec * hg))

    def cw(sec):
        return pl.BlockSpec((None, conv_w.shape[1], hb * dk),
                            lambda b, h, sec=sec: (layer, 0, h + sec * hg))

    scr = lambda: pltpu.VMEM((hb, L, dk), F32)
    nbytes = hb * (2 * 4 * L * dk * 4 + 11 * L * dk * 4 + 2 * L * dk * 2) + 2 * L * LANES * 4 \
        + 60 * GDN_CHUNK * GDN_CHUNK * 4
    return pl.pallas_call(
        functools.partial(_gdn_prompt_kernel, n_heads=H),
        out_shape=(jax.ShapeDtypeStruct((batch * L, H * dk), BF16),
                   jax.ShapeDtypeStruct((batch, H, dk, dk), F32))
        + (jax.ShapeDtypeStruct((batch, n_hist, H * dk), F32),) * 3,
        grid=(batch, hg),
        in_specs=[col(0), col(1), col(2), col(3),
                  pl.BlockSpec((L, LANES), lambda b, h: (b, 0)),
                  cw(0), cw(1), cw(2),
                  pl.BlockSpec((None, 1, dk), lambda b, h: (layer, 0, 0))],
        out_specs=(pl.BlockSpec((L, hb * dk), lambda b, h: (b, h)),
                   pl.BlockSpec((None, hb, dk, dk), lambda b, h: (b, h, 0, 0)))
        + (pl.BlockSpec((None, n_hist, hb * dk), lambda b, h: (b, 0, h)),) * 3,
        scratch_shapes=[scr(), scr(), scr(), scr(), scr(),
                        scr(), scr(), scr(), scr(), scr(),
                        pltpu.VMEM((hb, L // GDN_CHUNK * SUBLANES, dk), F32),
                        scr()],
        compiler_params=pltpu.CompilerParams(
            dimension_semantics=("parallel", "parallel"),
            vmem_limit_bytes=_vmem_limit(nbytes)),
        name="gdn_prompt",
    )(qkvz, qkvz, qkvz, qkvz, bg, conv_w, conv_w, conv_w, norm_w)


def _gdn_sample_kernel(q_ref, k_ref, v_ref, z_ref, bg_ref, bq_ref, bk_ref, bv_ref,
                       cwq_ref, cwk_ref, cwv_ref, nw_ref, s_in_ref, *rest, n_heads):
    o_ref, s_out_ref, cq_ref, ck_ref, cv_ref, osc = rest[-6:]
    nb, dk = q_ref.shape
    head = pl.program_id(0)

    def conv_silu(x_ref, buf_ref, new_ref, cw_ref):
        cw = cw_ref[...]
        n_hist = buf_ref.shape[0]
        x = x_ref[...]
        y = x * cw[n_hist:n_hist + 1, :]
        for i in range(n_hist):
            row = buf_ref[i]
            y = y + row * cw[i:i + 1, :]
            if i > 0:
                new_ref[i - 1] = row
        new_ref[n_hist - 1] = x
        return _silu(y)

    q = _l2norm_rows(conv_silu(q_ref, bq_ref, cq_ref, cwq_ref)) * (dk ** -0.5)
    k = _l2norm_rows(conv_silu(k_ref, bk_ref, ck_ref, cwk_ref))
    v = conv_silu(v_ref, bv_ref, cv_ref, cwv_ref)
    beta, g = _head_columns(bg_ref[...], head, n_heads)
    decay = jnp.exp(g)
    kt = jnp.concatenate([k, jnp.zeros((LANES - nb, dk), F32)], axis=0).T if nb < LANES else k.T
    qt = jnp.concatenate([q, jnp.zeros((LANES - nb, dk), F32)], axis=0).T if nb < LANES else q.T
    for b in range(nb):
        s = s_in_ref[b] * decay[b:b + 1, :]
        kcol = kt[:, b:b + 1]
        v_new = (v[b:b + 1, :] - _sum_rows(s * kcol)) * beta[b:b + 1, :]
        s = s + kcol * v_new
        s_out_ref[b] = s
        osc[b:b + 1, :] = _sum_rows(s * qt[:, b:b + 1])
    o_ref[...] = _gated_out_norm(osc[...], z_ref[...], nw_ref[...]).astype(o_ref.dtype)


def _gdn_sample(qkvz, bg, row0, conv_buf, conv_w, norm_w, state, layer, n_heads, dk,
                state_out=None):
    H = n_heads
    nb_total = state.shape[1]
    nb = 16
    assert nb_total % nb == 0 and row0 % nb == 0
    r0 = row0 // nb
    n_hist = conv_buf.shape[1]

    def col(off):
        return pl.BlockSpec((nb, dk), lambda h, i, off=off: (i + r0, h + off))

    def buf(off):
        return pl.BlockSpec((None, n_hist, nb, dk), lambda h, i, off=off: (layer, 0, i, h + off))

    def cw(off):
        return pl.BlockSpec((None, conv_w.shape[1], dk), lambda h, i, off=off: (layer, 0, h + off))

    in_specs = [col(0), col(H), col(2 * H), col(3 * H),
                pl.BlockSpec((nb, LANES), lambda h, i: (i + r0, 0)),
                buf(0), buf(H), buf(2 * H),
                cw(0), cw(H), cw(2 * H),
                pl.BlockSpec((None, 1, dk), lambda h, i: (layer, 0, 0)),
                pl.BlockSpec((None, nb, None, dk, dk), lambda h, i: (layer, i, h, 0, 0))]
    args = [qkvz, qkvz, qkvz, qkvz, bg, conv_buf, conv_buf, conv_buf,
            conv_w, conv_w, conv_w, norm_w, state]
    aliases = {}
    if state_out is not None:
        in_specs.append(pl.BlockSpec(memory_space=pl.ANY))
        args.append(state_out)
        aliases = {len(args) - 1: 1}
    nbytes = 4 * nb * dk * dk * 4 + 64 * nb * dk * 4 + 64 * dk * dk * 4
    return pl.pallas_call(
        functools.partial(_gdn_sample_kernel, n_heads=H),
        out_shape=(jax.ShapeDtypeStruct((nb_total, H * dk), BF16),
                   jax.ShapeDtypeStruct(state.shape, F32))
        + (jax.ShapeDtypeStruct((n_hist, nb_total, H * dk), F32),) * 3,
        grid=(H, nb_total // nb),
        in_specs=in_specs,
        out_specs=(pl.BlockSpec((nb, dk), lambda h, i: (i, h)),
                   pl.BlockSpec((None, nb, None, dk, dk), lambda h, i: (layer, i, h, 0, 0)))
        + (pl.BlockSpec((n_hist, nb, dk), lambda h, i: (0, i, h)),) * 3,
        scratch_shapes=[pltpu.VMEM((nb, dk), F32)],
        input_output_aliases=aliases,
        compiler_params=pltpu.CompilerParams(
            dimension_semantics=("parallel", "parallel"),
            vmem_limit_bytes=_vmem_limit(nbytes)),
        name="gdn_sample",
    )(*args)


def _s5_tables(lam_re, lam_im, log_dt, b_re, b_im, c_re, c_im):
    D, G, P = lam_re.shape
    gc = b_re.shape[-1]
    gpb = LANES // gc
    nblk = G // gpb
    dt = jnp.exp(log_dt)[..., None]
    mag = jnp.exp(lam_re * dt)
    ar = mag * jnp.cos(lam_im * dt)
    ai = mag * jnp.sin(lam_im * dt)
    nr = ar - 1.0
    den = lam_re * lam_re + lam_im * lam_im
    fr = (nr * lam_re + ai * lam_im) / den
    fi = (ai * lam_re - nr * lam_im) / den
    bbar_re = fr[..., None] * b_re - fi[..., None] * b_im
    bbar_im = fr[..., None] * b_im + fi[..., None] * b_re
    rg = lax.broadcasted_iota(jnp.int32, (gpb * gc, gpb * P), 0) // gc
    cg = lax.broadcasted_iota(jnp.int32, (gpb * gc, gpb * P), 1) // P
    diag = rg == cg

    def bmat(bb):
        t = jnp.swapaxes(bb.reshape(D, nblk, gpb * P, gc), 2, 3)
        return jnp.where(diag, jnp.tile(t, (1, 1, gpb, 1)), 0.0)

    def cmat(cc):
        t = jnp.swapaxes(cc.reshape(D, nblk, gpb, gc, P), 3, 4).reshape(D, nblk, gpb * P, gc)
        return jnp.where(diag.T, jnp.tile(t, (1, 1, 1, gpb)), 0.0)

    b_blk = jnp.concatenate([bmat(bbar_re), bmat(bbar_im)], axis=3).astype(BF16)
    c_blk = jnp.concatenate([cmat(c_re), -cmat(c_im)], axis=2).astype(BF16)
    return (b_blk, c_blk, ar.reshape(D, nblk, 1, gpb * P), ai.reshape(D, nblk, 1, gpb * P))


def _s5_prompt_kernel(u_ref, bblk_ref, cblk_ref, ar_ref, ai_ref, d_ref,
                      g5_ref, g5b_ref, xre_ref, xim_ref, up, xs0, xs1, ys):
    L = u_ref.shape[0]
    ns = ar_ref.shape[-1]
    nseg = S5_SEGMENTS
    seg = L // nseg
    R = xs0.shape[0] // max(xs0.shape[0] // (S5_BLOCK_STEPS * nseg), 1)
    steps = R // nseg
    n_pair = xs0.shape[0] // R
    n_blk = 2 * n_pair
    assert n_blk * R == L

    def gather(t, carry):
        up[pl.ds(pl.multiple_of(t * nseg, nseg), nseg), :] = u_ref[pl.ds(t, nseg, stride=seg), :]
        return carry

    lax.fori_loop(0, seg, gather, 0, unroll=8)
    bmat = bblk_ref[...]
    cmat = cblk_ref[...]
    ar = jnp.broadcast_to(ar_ref[...], (nseg, ns))
    ai = jnp.broadcast_to(ai_ref[...], (nseg, ns))

    def block(row0):
        return pl.ds(pl.multiple_of(row0, R), R)

    def project(buf, blk, row0):
        u = up[block(blk * R), :]
        buf[block(row0), :] = jnp.dot(u.astype(BF16), bmat, preferred_element_type=F32)

    def scan(buf, row0, x, store):
        for t in range(steps):
            rows = pl.ds(pl.multiple_of(row0 + t * nseg, nseg), nseg)
            r = buf[rows, :]
            xr, xi = x
            x = (ar * xr - ai * xi + r[:, :ns], ar * xi + ai * xr + r[:, ns:])
            if store:
                buf[rows, :] = jnp.concatenate(x, axis=1)
        return x

    def emit(buf, row0, blk):
        y = jnp.dot(buf[block(row0), :].astype(BF16), cmat, preferred_element_type=F32)
        ys[block(blk * R), :] = _gelu_tanh(y + d_ref[...] * up[block(blk * R), :])

    project(xs0, 0, 0)

    def pass1(p, e):
        row0 = p * R
        project(xs1, 2 * p + 1, row0)
        e = scan(xs0, row0, e, False)
        nxt = jnp.minimum(p + 1, n_pair - 1)
        project(xs0, 2 * nxt, nxt * R)
        return scan(xs1, row0, e, False)

    zero = jnp.zeros((nseg, ns), F32)
    er, ei = lax.fori_loop(0, n_pair, pass1, (zero, zero))
    pr, pi = ar_ref[...], ai_ref[...]
    for _ in range(seg.bit_length() - 1):
        pr, pi = pr * pr - pi * pi, 2.0 * pr * pi
    assert seg == 1 << (seg.bit_length() - 1)
    cr = [jnp.zeros((1, ns), F32)]
    ci = [jnp.zeros((1, ns), F32)]
    for s in range(nseg - 1):
        cr.append(er[s:s + 1] + pr * cr[s] - pi * ci[s])
        ci.append(ei[s:s + 1] + pr * ci[s] + pi * cr[s])
    x = (jnp.concatenate(cr, axis=0), jnp.concatenate(ci, axis=0))

    x = scan(xs0, 0, x, True)

    def pass2(p, x):
        row0 = p * R
        x = scan(xs1, row0, x, True)
        emit(xs0, row0, 2 * p)
        x = scan(xs0, row0 + R, x, True)
        emit(xs1, row0, 2 * p + 1)
        return x

    x = lax.fori_loop(0, n_pair - 1, pass2, x)
    last = (n_pair - 1) * R
    x = scan(xs1, last, x, True)
    emit(xs0, last, n_blk - 2)
    emit(xs1, last, n_blk - 1)
    xre_ref[...] = x[0][nseg - 1:nseg]
    xim_ref[...] = x[1][nseg - 1:nseg]
    for s in range(nseg):
        g5 = ys[pl.ds(s, seg, stride=nseg), :]
        g5_ref[s * seg:(s + 1) * seg, :] = g5
        g5b_ref[s * seg:(s + 1) * seg, :] = g5.astype(BF16)


def _s5_prompt(u_all, tables, d_skip, layer, batch, seq, n_ch):
    b_blk, c_blk, ar, ai = tables
    nblk, _, ns2 = b_blk.shape[1:]
    ns = ns2 // 2
    L = seq
    nbytes = (2 * L * LANES * 4 + 2 * L * LANES * 6 + 3 * L * ns2 * 4 + L * ns2 * 2
              + 8 * LANES * ns2 * 2)
    blkp = lambda r, c: pl.BlockSpec((None, None, r, c), lambda b, j: (layer, j, 0, 0))
    return pl.pallas_call(
        _s5_prompt_kernel,
        out_shape=(jax.ShapeDtypeStruct((batch * L, n_ch), F32),
                   jax.ShapeDtypeStruct((batch * L, n_ch), BF16),
                   jax.ShapeDtypeStruct((batch, 1, nblk * ns), F32),
                   jax.ShapeDtypeStruct((batch, 1, nblk * ns), F32)),
        grid=(batch, nblk),
        in_specs=[pl.BlockSpec((L, LANES), lambda b, j: (b, j)),
                  blkp(LANES, ns2), blkp(ns2, LANES), blkp(1, ns), blkp(1, ns),
                  pl.BlockSpec((None, 1, LANES), lambda b, j: (layer, 0, j))],
        out_specs=(pl.BlockSpec((L, LANES), lambda b, j: (b, j)),
                   pl.BlockSpec((L, LANES), lambda b, j: (b, j)),
                   pl.BlockSpec((None, 1, ns), lambda b, j: (b, 0, j)),
                   pl.BlockSpec((None, 1, ns), lambda b, j: (b, 0, j))),
        scratch_shapes=[pltpu.VMEM((L, LANES), F32),
                        pltpu.VMEM((L // 2, ns2), F32),
                        pltpu.VMEM((L // 2, ns2), F32),
                        pltpu.VMEM((L, LANES), F32)],
        compiler_params=pltpu.CompilerParams(
            dimension_semantics=("parallel", "parallel"),
            vmem_limit_bytes=_vmem_limit(nbytes)),
        name="s5_prompt",
    )(u_all, b_blk, c_blk, ar, ai, d_skip)


def _s5_sample_kernel(u_ref, bblk_ref, cblk_ref, ar_ref, ai_ref, d_ref, x0r_ref, x0i_ref,
                      g5_ref, g5b_ref, xre_ref, xim_ref):
    ns = ar_ref.shape[-1]
    u = u_ref[...]
    bu = jnp.dot(u.astype(BF16), bblk_ref[...], preferred_element_type=F32)
    ar, ai = ar_ref[...], ai_ref[...]
    x0r, x0i = x0r_ref[...], x0i_ref[...]
    xr = ar * x0r - ai * x0i + bu[:, :ns]
    xi = ar * x0i + ai * x0r + bu[:, ns:]
    xre_ref[...] = xr
    xim_ref[...] = xi
    x = jnp.concatenate([xr, xi], axis=1)
    y = jnp.dot(x.astype(BF16), cblk_ref[...], preferred_element_type=F32) + d_ref[...] * u
    g5 = _gelu_tanh(y)
    g5_ref[...] = g5
    g5b_ref[...] = g5.astype(BF16)


def _s5_sample(u_all, row0, tables, d_skip, x0_re, x0_im, layer, n_ch):
    b_blk, c_blk, ar, ai = tables
    nblk, _, ns2 = b_blk.shape[1:]
    ns = ns2 // 2
    nb = x0_re.shape[1]
    assert row0 % nb == 0
    r0 = row0 // nb
    blkp = lambda r, c: pl.BlockSpec((None, None, r, c), lambda j: (layer, j, 0, 0))
    nbytes = 16 * nb * ns2 * 4 + 8 * LANES * ns2 * 2
    return pl.pallas_call(
        _s5_sample_kernel,
        out_shape=(jax.ShapeDtypeStruct((nb, n_ch), F32),
                   jax.ShapeDtypeStruct((nb, n_ch), BF16),
                   jax.ShapeDtypeStruct((nb, nblk * ns), F32),
                   jax.ShapeDtypeStruct((nb, nblk * ns), F32)),
        grid=(nblk,),
        in_specs=[pl.BlockSpec((nb, LANES), lambda j: (r0, j)),
                  blkp(LANES, ns2), blkp(ns2, LANES), blkp(1, ns), blkp(1, ns),
                  pl.BlockSpec((None, 1, LANES), lambda j: (layer, 0, j)),
                  pl.BlockSpec((None, nb, ns), lambda j: (layer, 0, j)),
                  pl.BlockSpec((None, nb, ns), lambda j: (layer, 0, j))],
        out_specs=(pl.BlockSpec((nb, LANES), lambda j: (0, j)),
                   pl.BlockSpec((nb, LANES), lambda j: (0, j)),
                   pl.BlockSpec((nb, ns), lambda j: (0, j)),
                   pl.BlockSpec((nb, ns), lambda j: (0, j))),
        compiler_params=pltpu.CompilerParams(
            dimension_semantics=("parallel",), vmem_limit_bytes=_vmem_limit(nbytes)),
        name="s5_sample",
    )(u_all, b_blk, c_blk, ar, ai, d_skip, x0_re, x0_im)


def kernel(x_prompt, x_sample, state_dn_conv, state_dn_ssm, state_s5_re, state_s5_im, norm1, w_in, dn_conv_w, dn_a_log, dn_dt_bias, dn_norm_w, w_br_dn, s5_lam_re, s5_lam_im, s5_log_dt, s5_b_re, s5_b_im, s5_c_re, s5_c_im, s5_d, w_glu, w_br_s5, w_out, norm2, w_ffn_gate, w_ffn_up, w_ffn_down, norm_f):
    batch, seq, d_model = x_prompt.shape
    nb, dec_seq, _ = x_sample.shape
    assert dec_seq == 1
    depth, _, n_heads, dk, dv = state_dn_ssm.shape
    assert dk == LANES and dv == LANES and seq % GDN_CHUNK == 0 and seq % S5_SEGMENTS == 0
    qk_dim = n_heads * dk
    conv_ch = dn_conv_w.shape[2]
    assert conv_ch == 3 * qk_dim
    n_ch = s5_d.shape[1]
    n_groups, n_state = s5_lam_re.shape[1:]
    ffn = w_ffn_gate.shape[2]
    mp = batch * seq
    m = mp + nb
    z_end = 4 * qk_dim
    rest0 = z_end + 2 * n_heads
    assert w_in.shape[2] == rest0 + n_ch + 2 * d_model and 2 * n_heads <= LANES

    x = (x_prompt.reshape(mp, d_model), x_sample.reshape(nb, d_model))

    w_in_t = jnp.swapaxes(w_in, 1, 2)
    w_bg = jnp.pad(w_in_t[:, z_end:rest0, :], ((0, 0), (0, LANES - 2 * n_heads), (0, 0)))
    w_s5 = w_in_t[:, rest0:rest0 + n_ch, :]
    gate_tn = 512
    gate0 = rest0 + n_ch
    gate_base = gate0 - gate0 % gate_tn
    gate_shift = gate0 - gate_base
    gate_cols = -(-(gate_shift + 2 * d_model) // gate_tn) * gate_tn
    assert gate_shift < LANES and gate_base + gate_cols - gate_tn < w_in.shape[2]
    conv_hist = jnp.swapaxes(state_dn_conv, 1, 2)
    pad_heads = lambda a: jnp.pad(a, ((0, 0), (n_heads, LANES - 2 * n_heads)))[:, None, :]
    alog_pad = pad_heads(dn_a_log)
    dtb_pad = pad_heads(dn_dt_bias)
    x0_re = state_s5_re.reshape(depth, nb, n_groups * n_state)
    x0_im = state_s5_im.reshape(depth, nb, n_groups * n_state)

    tm = _pick_tile(m, 1664, 64)
    norm_w3 = dn_norm_w[:, None, :]
    d_skip3 = s5_d[:, None, :]
    tables = _s5_tables(s5_lam_re, s5_lam_im, s5_log_dt, s5_b_re, s5_b_im, s5_c_re, s5_c_im)
    outs = {k: [] for k in ("p_conv", "p_ssm", "p_re", "p_im", "s_conv", "s_re", "s_im")}
    s_ssm = None
    for l in range(depth):
        h = _rmsnorm(x, norm1[l][None, :], BF16)
        qkvz = _fused_matmul([h], [(0, w_in_t, l, 0, True)], [], _ep_identity, z_end, F32,
                             tm=tm, tn=512, name="in_qkvz")
        bg = _fused_matmul([h], [(0, w_bg, l, 0, True)], [], _ep_beta_decay(n_heads), LANES, F32,
                           tm=tm, tn=LANES, name="in_bg",
                           col_params=[(alog_pad, l), (dtb_pad, l)])
        s5u = _fused_matmul([h], [(0, w_s5, l, 0, True)], [], _ep_identity, n_ch, F32,
                            tm=tm, tn=512, name="in_s5")
        gates = _fused_matmul([h], [(0, w_in_t, l, gate_base, True)], [], _ep_identity, gate_cols, F32,
                              tm=tm, tn=gate_tn, name="in_gates")

        o_p, ssm_p, *pc = _gdn_prompt(qkvz, bg, dn_conv_w, norm_w3, l, batch, seq, n_heads, dk)
        o_s, s_ssm, *sc = _gdn_sample(qkvz, bg, mp, conv_hist, dn_conv_w, norm_w3,
                                      state_dn_ssm, l, n_heads, dk, state_out=s_ssm)
        outs["p_conv"].append(jnp.concatenate(pc, axis=-1))
        outs["s_conv"].append(jnp.concatenate(sc, axis=-1))
        outs["p_ssm"].append(ssm_p)

        g5_p, g5b_p, re_p, im_p = _s5_prompt(s5u, tables, d_skip3, l, batch, seq, n_ch)
        g5_s, g5b_s, re_s, im_s = _s5_sample(s5u, mp, tables, d_skip3, x0_re, x0_im, l, n_ch)
        outs["p_re"].append(re_p.reshape(batch, n_groups, n_state))
        outs["p_im"].append(im_p.reshape(batch, n_groups, n_state))
        outs["s_re"].append(re_s.reshape(nb, n_groups, n_state))
        outs["s_im"].append(im_s.reshape(nb, n_groups, n_state))
        g5g = _fused_matmul([(g5b_p, g5b_s)], [(0, w_glu, l, 0)], [((g5_p, g5_s), 0)],
                            _ep_glu_self, n_ch, BF16, tm=tm, tn=512, name="s5_glu")

        merged = _fused_matmul([(o_p, o_s), g5g], [(0, w_br_dn, l, 0), (1, w_br_s5, l, 0)],
                               [(gates, gate_shift), (gates, gate_shift + d_model)],
                               _ep_gated_merge,
                               d_model, BF16, tm=tm, tn=512, name="branch_merge")
        x = _fused_matmul([merged], [(0, w_out, l, 0)], [(x, 0)], _ep_residual, d_model, F32,
                          tm=tm, tn=512, name="out_proj")

        hmid = _fused_matmul([x], [(0, w_ffn_gate, l, 0), (0, w_ffn_up, l, 0)], [], _ep_swiglu,
                             ffn, BF16, tm=tm, tn=256, name="ffn_up", norm_w=norm2[l][None, :])
        x = _matmul_residual_wstat(hmid, w_ffn_down, l, x, tm=_pick_tile(m, 416, 16), tn=512,
                                   name="ffn_down")

    y_p, y_s = _rmsnorm(x, norm_f[None, :], F32, split_rows=(mp, nb))
    st = lambda k: jnp.stack(outs[k])
    return (y_p.reshape(batch, seq, d_model), y_s.reshape(nb, 1, d_model),
            st("p_conv"), st("p_ssm"), st("p_re"), st("p_im"),
            jnp.swapaxes(st("s_conv"), 1, 2), s_ssm, st("s_re"), st("s_im"))
```

```python
import functools
import math

import jax
import jax.numpy as jnp
from jax import lax
from jax.experimental import pallas as pl
from jax.experimental.pallas import tpu as pltpu

F32 = jnp.float32
BF16 = jnp.bfloat16

NORM_EPS = 1e-6
L2_EPS = 1e-6
LANES = 128
SUBLANES = 8
VMEM_CAP_BYTES = 56 * 1024 * 1024
GDN_CHUNK = 128
S5_SEGMENTS = SUBLANES
GDN_HEADS_PER_STEP = 2
GDN_CHUNKS_PER_GROUP = 16
S5_BLOCK_STEPS = 32


def _vmem_limit(nbytes):
    return int(min(VMEM_CAP_BYTES, nbytes * 5 // 4 + (4 << 20)))


def _pick_tile(n, target, mult):
    best = None
    for t in range(mult, min(n, target) + 1, mult):
        if n % t == 0:
            best = t
    return best if best is not None else n


def _sigmoid(x):
    return 1.0 / (1.0 + jnp.exp(-x))


def _silu(x):
    return x * _sigmoid(x)


def _softplus(x):
    return jnp.maximum(x, 0.0) + jnp.log1p(jnp.exp(-jnp.abs(x)))


def _gelu_tanh(x):
    c = math.sqrt(2.0 / math.pi)
    return 0.5 * x * (1.0 + jnp.tanh(c * (x + 0.044715 * (x * x * x))))


def _bdot(a, b):
    return jnp.dot(a.astype(BF16), b.astype(BF16), preferred_element_type=F32)


def _bdot_nt(a, b):
    return lax.dot_general(a.astype(BF16), b.astype(BF16), (((1,), (1,)), ((), ())),
                           preferred_element_type=F32)


def _rows_of(op):
    return op[0].shape[0] + op[1].shape[0] if isinstance(op, tuple) else op.shape[0]


def _row_specs(op, tm, ncols, index_map):
    if not isinstance(op, tuple):
        return [pl.BlockSpec((tm, ncols), index_map)], [op], None
    p, s = op
    tail = p.shape[0] % tm
    assert tail + s.shape[0] == tm and tail % 16 == 0, (p.shape, s.shape, tm)

    def s_map(*idx):
        return (0,) + tuple(index_map(*idx)[1:])

    return ([pl.BlockSpec((tm, ncols), index_map), pl.BlockSpec((s.shape[0], ncols), s_map)],
            [p, s], tail)


def _load_rows(refs, tail, last):
    if tail is None or not last:
        return refs[0][...]
    return jnp.concatenate([refs[0][:tail, :], refs[1][...]], axis=0)


def _lane_window(parts, shift, width):
    x = parts[0] if len(parts) == 1 else jnp.concatenate(parts, axis=1)
    if shift:
        x = pltpu.roll(x, x.shape[1] - shift, 1)
    return x[:, :width]


def _on_row_tiles(i, n_tiles, any_split, body):
    if not any_split:
        body(False)
        return
    if n_tiles > 1:
        pl.when(i < n_tiles - 1)(lambda: body(False))
    pl.when(i == n_tiles - 1)(lambda: body(True))


def _rmsnorm_kernel(*refs, tail, n_tiles, out_tail):
    n_out = 1 if out_tail is None else 2
    x_refs, w_ref, o_refs = refs[:-1 - n_out], refs[-1 - n_out], refs[-n_out:]

    def body(last):
        x = _load_rows(x_refs, tail, last)
        y = x * lax.rsqrt(jnp.mean(x * x, axis=-1, keepdims=True) + NORM_EPS)
        y = (y * w_ref[...]).astype(o_refs[0].dtype)
        if out_tail is None:
            o_refs[0][...] = y
        elif not last:
            o_refs[0][...] = y
        else:
            o_refs[0][:out_tail, :] = y[:out_tail]
            o_refs[1][...] = y[out_tail:]

    _on_row_tiles(pl.program_id(0), n_tiles, tail is not None or out_tail is not None, body)


def _rmsnorm(x, w_row, out_dtype, split_rows=None):
    m = _rows_of(x)
    d = w_row.shape[1]
    tr = _pick_tile(m, 832, 64)
    specs, arrs, tail = _row_specs(x, tr, d, lambda i: (i, 0))
    nbytes = 2 * tr * d * 4 + 2 * tr * d * jnp.dtype(out_dtype).itemsize + 3 * tr * d * 4
    if split_rows is None:
        out_shape = jax.ShapeDtypeStruct((m, d), out_dtype)
        out_specs = pl.BlockSpec((tr, d), lambda i: (i, 0))
        out_tail = None
    else:
        mp, nb = split_rows
        out_tail = mp % tr
        assert mp + nb == m and out_tail + nb == tr
        out_shape = (jax.ShapeDtypeStruct((mp, d), out_dtype),
                     jax.ShapeDtypeStruct((nb, d), out_dtype))
        out_specs = (pl.BlockSpec((tr, d), lambda i: (i, 0)),
                     pl.BlockSpec((nb, d), lambda i: (0, 0)))
    return pl.pallas_call(
        functools.partial(_rmsnorm_kernel, tail=tail, n_tiles=m // tr, out_tail=out_tail),
        out_shape=out_shape,
        grid=(m // tr,),
        in_specs=specs + [pl.BlockSpec((1, d), lambda i: (0, 0))],
        out_specs=out_specs,
        compiler_params=pltpu.CompilerParams(
            dimension_semantics=("arbitrary",), vmem_limit_bytes=_vmem_limit(nbytes)),
        name="rmsnorm",
    )(*arrs, w_row)


def _mm_kernel(*refs, a_idx, w_transposed, a_groups, e_groups, n_tiles, epilogue, normed,
               zero_tail_from):
    pos = 0
    a_refs = []
    for n, _ in a_groups:
        a_refs.append(refs[pos:pos + n])
        pos += n
    w_refs = refs[pos:pos + len(a_idx)]
    pos += len(a_idx)
    e_refs = []
    for n, _, _ in e_groups:
        e_refs.append(refs[pos:pos + n])
        pos += n
    if normed:
        nw_ref, o_ref, h_scr = refs[pos:pos + 3]
    else:
        o_ref = refs[pos]
    any_split = any(g[1] is not None for g in a_groups + e_groups)

    def load_extra(r, tail, shift, last):
        if shift:
            return _lane_window([r[0][...], r[1][...]], shift, o_ref.shape[1])
        return _load_rows(r, tail, last)

    def body(last):
        if normed:
            @pl.when(pl.program_id(1) == 0)
            def _():
                x = _load_rows(a_refs[0], a_groups[0][1], last)
                y = x * lax.rsqrt(jnp.mean(x * x, axis=-1, keepdims=True) + NORM_EPS)
                h_scr[...] = (y * nw_ref[...]).astype(BF16)

            a_vals = [h_scr[...]]
        else:
            a_vals = [_load_rows(r, t, last) for r, (_, t) in zip(a_refs, a_groups)]
        parts = [(_bdot_nt if wt else _bdot)(a_vals[ai], w[...])
                 for ai, wt, w in zip(a_idx, w_transposed, w_refs)]
        e_vals = [load_extra(r, t, sh, last) for r, (_, t, sh) in zip(e_refs, e_groups)]
        o_ref[...] = epilogue(parts, e_vals).astype(o_ref.dtype)
        if zero_tail_from is not None:
            @pl.when(pl.program_id(1) == pl.num_programs(1) - 1)
            def _():
                o_ref[:, zero_tail_from:] = jnp.zeros(
                    (o_ref.shape[0], o_ref.shape[1] - zero_tail_from), o_ref.dtype)

    _on_row_tiles(pl.program_id(0), n_tiles, any_split, body)


def _fused_matmul(a_list, w_list, extras, epilogue, n_out, out_dtype, *, tm, tn, name,
                  col_params=(), norm_w=None, valid_cols=None):
    m = _rows_of(a_list[0])
    assert m % tm == 0 and n_out % tn == 0
    in_specs, args, a_groups, e_groups = [], [], [], []
    kdims = []
    for a in a_list:
        kd = (a[0] if isinstance(a, tuple) else a).shape[1]
        specs, arrs, tail = _row_specs(a, tm, kd, lambda i, j: (i, 0))
        assert _rows_of(a) == m
        in_specs += specs
        args += arrs
        a_groups.append((len(arrs), tail))
        kdims.append(kd)
    w_list = [tuple(e) + (False,) * (5 - len(e)) for e in w_list]
    for ai, w, layer, col0, transposed in w_list:
        assert col0 % tn == 0 and w.shape[2 if transposed else 1] == kdims[ai]
        if transposed:
            spec = pl.BlockSpec((None, tn, w.shape[2]),
                                lambda i, j, layer=layer, off=col0 // tn: (layer, j + off, 0))
        else:
            spec = pl.BlockSpec((None, w.shape[1], tn),
                                lambda i, j, layer=layer, off=col0 // tn: (layer, 0, j + off))
        in_specs.append(spec)
        args.append(w)
    for e, col0 in extras:
        shift = col0 % tn
        assert shift < LANES and _rows_of(e) == m
        specs, arrs, tail = _row_specs(e, tm, tn, lambda i, j, off=col0 // tn: (i, j + off))
        if shift:
            assert not isinstance(e, tuple) and tn % LANES == 0
            specs.append(pl.BlockSpec(
                (tm, LANES), lambda i, j, off=col0 // tn: (i, (j + off + 1) * (tn // LANES))))
            arrs.append(e)
        in_specs += specs
        args += arrs
        e_groups.append((len(arrs), tail, shift))
    for p, layer in col_params:
        in_specs.append(pl.BlockSpec((None, 1, tn), lambda i, j, layer=layer: (layer, 0, j)))
        args.append(p)
        e_groups.append((1, None, 0))
    normed = norm_w is not None
    scratch = []
    if normed:
        assert len(a_list) == 1
        in_specs.append(pl.BlockSpec((1, kdims[0]), lambda i, j: (0, 0)))
        args.append(norm_w)
        scratch.append(pltpu.VMEM((tm, kdims[0]), BF16))
    osz = jnp.dtype(out_dtype).itemsize
    nbytes = (sum(3 * tm * kd * (6 if normed else 2) for kd in kdims)
              + sum(kdims[ai] * tn * (2 * w.dtype.itemsize + 2) for ai, w, _, _, _ in w_list)
              + sum(3 * tm * tn * 4 for _ in extras)
              + 2 * tm * tn * osz + (2 + len(w_list)) * tm * tn * 4)
    kern = functools.partial(_mm_kernel, a_idx=tuple(e[0] for e in w_list),
                             w_transposed=tuple(e[4] for e in w_list), a_groups=tuple(a_groups), e_groups=tuple(e_groups),
                             n_tiles=m // tm, epilogue=epilogue, normed=normed,
                             zero_tail_from=None if valid_cols is None else valid_cols % tn)
    assert valid_cols is None or n_out - tn < valid_cols < n_out
    return pl.pallas_call(
        kern,
        out_shape=jax.ShapeDtypeStruct((m, n_out), out_dtype),
        grid=(m // tm, n_out // tn),
        in_specs=in_specs,
        out_specs=pl.BlockSpec((tm, tn), lambda i, j: (i, j)),
        scratch_shapes=scratch,
        compiler_params=pltpu.CompilerParams(
            dimension_semantics=("parallel", "arbitrary" if normed else "parallel"),
            vmem_limit_bytes=_vmem_limit(nbytes)),
        name=name,
    )(*args)


def _mm_residual_wstat_kernel(a_ref, w_ref, x_ref, o_ref, wb):
    @pl.when(pl.program_id(1) == 0)
    def _():
        wb[...] = w_ref[...].astype(BF16)

    o_ref[...] = x_ref[...] + jnp.dot(a_ref[...], wb[...], preferred_element_type=F32)


def _matmul_residual_wstat(a, w, layer, x, *, tm, tn, name):
    m, kdim = a.shape
    n_out = w.shape[2]
    assert m % tm == 0 and n_out % tn == 0 and w.shape[1] == kdim
    nbytes = 2 * tm * kdim * 2 + kdim * tn * (2 * w.dtype.itemsize + 2) + 6 * tm * tn * 4
    return pl.pallas_call(
        _mm_residual_wstat_kernel,
        out_shape=jax.ShapeDtypeStruct((m, n_out), F32),
        grid=(n_out // tn, m // tm),
        in_specs=[pl.BlockSpec((tm, kdim), lambda j, i: (i, 0)),
                  pl.BlockSpec((None, kdim, tn), lambda j, i: (layer, 0, j)),
                  pl.BlockSpec((tm, tn), lambda j, i: (i, j))],
        out_specs=pl.BlockSpec((tm, tn), lambda j, i: (i, j)),
        scratch_shapes=[pltpu.VMEM((kdim, tn), BF16)],
        compiler_params=pltpu.CompilerParams(
            dimension_semantics=("parallel", "arbitrary"),
            vmem_limit_bytes=_vmem_limit(nbytes)),
        name=name,
    )(a, w, x)


def _ep_identity(accs, extras):
    return accs[0]


def _ep_residual(accs, extras):
    return extras[0] + accs[0]


def _ep_swiglu(accs, extras):
    return _silu(accs[0]) * accs[1]


def _ep_glu_self(accs, extras):
    g5 = extras[0]
    return g5 * _sigmoid(accs[0])


def _ep_gated_merge(accs, extras):
    return _sigmoid(extras[0]) * accs[0] + _sigmoid(extras[1]) * accs[1]


def _ep_beta_decay(n_heads):
    def ep(accs, extras):
        acc = accs[0]
        alog_row, dtb_row = extras
        lane = lax.broadcasted_iota(jnp.int32, acc.shape, 1)
        return jnp.where(lane < n_heads, _sigmoid(acc),
                         -jnp.exp(alog_row) * _softplus(acc + dtb_row))
    return ep


def _head_columns(bg, head, n_heads):
    lane = lax.broadcasted_iota(jnp.int32, bg.shape, 1)
    beta = jnp.sum(jnp.where(lane == head, bg, 0.0), axis=-1, keepdims=True)
    g = jnp.sum(jnp.where(lane == head + n_heads, bg, 0.0), axis=-1, keepdims=True)
    return beta, g


def _l2norm_rows(x):
    return x * lax.rsqrt(jnp.sum(x * x, axis=-1, keepdims=True) + L2_EPS)


def _sum_rows(x):
    acc = x[0:SUBLANES]
    for i in range(1, x.shape[0] // SUBLANES):
        acc = acc + x[i * SUBLANES:(i + 1) * SUBLANES]
    shift = SUBLANES // 2
    while shift:
        acc = acc + pltpu.roll(acc, shift, 0)
        shift //= 2
    return acc[0:1]


def _gated_out_norm(o, z, nw_row):
    y = o * lax.rsqrt(jnp.mean(o * o, axis=-1, keepdims=True) + NORM_EPS)
    return y * nw_row * _silu(z)


def _gdn_prompt_kernel(q_ref, k_ref, v_ref, z_ref, bg_ref, cwq_ref, cwk_ref, cwv_ref, nw_ref,
                       o_ref, s_ref, pcq_ref, pck_ref, pcv_ref,
                       qn, kn, vn, gb, bb, us, ws, qks, qds, kdt, gl, osc, *, n_heads):
    hb, L, dk = qn.shape
    n_hist = pcq_ref.shape[0]
    for x_ref, pc_ref in ((q_ref, pcq_ref), (k_ref, pck_ref), (v_ref, pcv_ref)):
        pc_ref[...] = x_ref[L - n_hist:L, :]
    C = GDN_CHUNK
    n_chunks = L // C
    H8 = range(hb)

    row8 = lax.broadcasted_iota(jnp.int32, (SUBLANES, dk), 0)

    def conv_silu(x_ref, cw_ref, cols):
        cw = cw_ref[:, cols]
        n_taps = cw.shape[0]
        assert n_taps - 1 <= SUBLANES
        tap = lambda j: cw[n_taps - 1 - j:n_taps - j, :]
        head8 = x_ref[0:SUBLANES, cols]
        lo = head8 * tap(0)
        hi = x_ref[SUBLANES:L, cols] * tap(0)
        for j in range(1, n_taps):
            lo = lo + jnp.where(row8 >= j, pltpu.roll(head8, j, 0), 0.0) * tap(j)
            hi = hi + x_ref[pl.ds(SUBLANES - j, L - SUBLANES), cols] * tap(j)
        return _silu(jnp.concatenate([lo, hi], axis=0))

    bg = bg_ref[...]
    for hh in H8:
        cols = slice(hh * dk, (hh + 1) * dk)
        qn[hh] = _l2norm_rows(conv_silu(q_ref, cwq_ref, cols)) * (dk ** -0.5)
        kn[hh] = _l2norm_rows(conv_silu(k_ref, cwk_ref, cols))
        vn[hh] = conv_silu(v_ref, cwv_ref, cols)
        beta, g = _head_columns(bg, pl.program_id(1) * hb + hh, n_heads)
        bb[hh] = jnp.broadcast_to(beta, (L, dk))
        gb[hh] = jnp.broadcast_to(g, (L, dk))

    ri = lax.broadcasted_iota(jnp.int32, (C, C), 0)
    ci = lax.broadcasted_iota(jnp.int32, (C, C), 1)
    causal = ri >= ci
    strict = ri > ci
    tri_incl = jnp.where(causal, 1.0, 0.0).astype(F32)
    eye = jnp.where(ri == ci, 1.0, 0.0).astype(F32)
    level_masks = []
    n = 1
    while n < C:
        sh = n.bit_length() - 1
        same_2n = (ri >> (sh + 1)) == (ci >> (sh + 1))
        diff_n = (ri >> sh) != (ci >> sh)
        level_masks.append(jnp.where(same_2n & diff_n & strict, 1.0, 0.0).astype(F32))
        n *= 2

    group = math.gcd(n_chunks, GDN_CHUNKS_PER_GROUP)

    def intra_group(hh, i):
        G = range(group)
        rows = [pl.ds(pl.multiple_of((i * group + j) * C, C), C) for j in G]
        k = [kn[hh, r, :] for r in rows]
        bet = [bb[hh, r, :] for r in rows]
        gcb = [jnp.dot(tri_incl, gb[hh, r, :], precision=lax.Precision.HIGHEST,
                       preferred_element_type=F32) for r in rows]
        gamma = [jnp.where(causal, jnp.exp(jnp.minimum(g - g.T, 0.0)), 0.0) for g in gcb]
        kb = [k[j] * bet[j] for j in G]
        a_mat = [jnp.where(strict, _bdot_nt(kb[j], k[j]) * gamma[j], 0.0) for j in G]
        q = [qn[hh, r, :] for r in rows]
        for j in G:
            qks[hh, rows[j], :] = _bdot_nt(q[j], k[j]) * gamma[j]
        t = [eye - a * level_masks[0] for a in a_mat]
        for m in level_masks[1:]:
            x = [_bdot(a_mat[j] * m, t[j]) for j in G]
            t = [t[j] - _bdot(t[j], x[j]) for j in G]
        eg = [jnp.exp(g) for g in gcb]
        for j in G:
            us[hh, rows[j], :] = _bdot(t[j], vn[hh, rows[j], :] * bet[j])
        for j in G:
            ws[hh, rows[j], :] = _bdot(t[j], kb[j] * eg[j])
        for j in G:
            qds[hh, rows[j], :] = q[j] * eg[j]
            g_last = gcb[j][C - 1:C, :]
            kdt[hh, rows[j], :] = (k[j] * jnp.exp(g_last - gcb[j])).T
            gl[hh, pl.ds(pl.multiple_of((i * group + j) * SUBLANES, SUBLANES), SUBLANES), :] = (
                jnp.broadcast_to(jnp.exp(g_last), (SUBLANES, dk)))

    for hh in H8:
        if n_chunks == group:
            intra_group(hh, 0)
        else:
            def body(i, carry, hh=hh):
                intra_group(hh, i)
                return carry
            lax.fori_loop(0, n_chunks // group, body, 0)

    def inter(c, states):
        rows = pl.ds(pl.multiple_of(c * C, C), C)
        ws_s = [_bdot(ws[hh, rows, :], states[hh]) for hh in H8]
        qd_s = [_bdot(qds[hh, rows, :], states[hh]) for hh in H8]
        v_new = [us[hh, rows, :] - ws_s[hh] for hh in H8]
        for hh in H8:
            osc[hh, rows, :] = qd_s[hh] + _bdot(qks[hh, rows, :], v_new[hh])
        decay = [gl[hh, pl.ds(pl.multiple_of(c * SUBLANES, SUBLANES), 1), :] for hh in H8]
        return tuple(states[hh] * decay[hh] + _bdot(kdt[hh, rows, :], v_new[hh]) for hh in H8)

    s_fin = lax.fori_loop(0, n_chunks, inter, tuple(jnp.zeros((dk, dk), F32) for _ in H8))
    for hh in H8:
        cols = slice(hh * dk, (hh + 1) * dk)
        s_ref[hh] = s_fin[hh]
        o_ref[:, cols] = _gated_out_norm(osc[hh], z_ref[:, cols], nw_ref[...]).astype(o_ref.dtype)


def _gdn_prompt(qkvz, bg, conv_w, norm_w, layer, batch, seq, n_heads, dk):
    H = n_heads
    L = seq
    hb = GDN_HEADS_PER_STEP if H % GDN_HEADS_PER_STEP == 0 else 1
    hg = H // hb
    n_hist = conv_w.shape[1] - 1

    def col(sec):
        return pl.BlockSpec((L, hb * dk), lambda b, h, sec=sec: (b, h + sec * hg))

    def cw(sec):
        return pl.BlockSpec((None, conv_w.shape[1], hb * dk),
                            lambda b, h, sec=sec: (layer, 0, h + sec * hg))

    scr = lambda: pltpu.VMEM((hb, L, dk), F32)
    nbytes = hb * (2 * 4 * L * dk * 4 + 11 * L * dk * 4 + 2 * L * dk * 2) + 2 * L * LANES * 4 \
        + 60 * GDN_CHUNK * GDN_CHUNK * 4
    return pl.pallas_call(
        functools.partial(_gdn_prompt_kernel, n_heads=H),
        out_shape=(jax.ShapeDtypeStruct((batch * L, H * dk), BF16),
                   jax.ShapeDtypeStruct((batch, H, dk, dk), F32))
        + (jax.ShapeDtypeStruct((batch, n_hist, H * dk), F32),) * 3,
        grid=(batch, hg),
        in_specs=[col(0), col(1), col(2), col(3),
                  pl.BlockSpec((L, LANES), lambda b, h: (b, 0)),
                  cw(0), cw(1), cw(2),
                  pl.BlockSpec((None, 1, dk), lambda b, h: (layer, 0, 0))],
        out_specs=(pl.BlockSpec((L, hb * dk), lambda b, h: (b, h)),
                   pl.BlockSpec((None, hb, dk, dk), lambda b, h: (b, h, 0, 0)))
        + (pl.BlockSpec((None, n_hist, hb * dk), lambda b, h: (b, 0, h)),) * 3,
        scratch_shapes=[scr(), scr(), scr(), scr(), scr(),
                        scr(), scr(), scr(), scr(), scr(),
                        pltpu.VMEM((hb, L // GDN_CHUNK * SUBLANES, dk), F32),
                        scr()],
        compiler_params=pltpu.CompilerParams(
            dimension_semantics=("parallel", "parallel"),
            vmem_limit_bytes=_vmem_limit(nbytes)),
        name="gdn_prompt",
    )(qkvz, qkvz, qkvz, qkvz, bg, conv_w, conv_w, conv_w, norm_w)


def _gdn_sample_kernel(q_ref, k_ref, v_ref, z_ref, bg_ref, bq_ref, bk_ref, bv_ref,
                       cwq_ref, cwk_ref, cwv_ref, nw_ref, s_in_ref, *rest, n_heads, stack_layer):
    o_ref, s_out_ref, cq_ref, ck_ref, cv_ref, osc = rest[-6:]
    nb, dk = q_ref.shape
    head = pl.program_id(0)

    def conv_silu(x_ref, buf_ref, new_ref, cw_ref):
        cw = cw_ref[...]
        n_hist = buf_ref.shape[0]
        x = x_ref[...]
        y = x * cw[n_hist:n_hist + 1, :]
        for i in range(n_hist):
            row = buf_ref[i]
            y = y + row * cw[i:i + 1, :]
            if i > 0:
                new_ref[i - 1] = row
        new_ref[n_hist - 1] = x
        return _silu(y)

    q = _l2norm_rows(conv_silu(q_ref, bq_ref, cq_ref, cwq_ref)) * (dk ** -0.5)
    k = _l2norm_rows(conv_silu(k_ref, bk_ref, ck_ref, cwk_ref))
    v = conv_silu(v_ref, bv_ref, cv_ref, cwv_ref)
    beta, g = _head_columns(bg_ref[...], head, n_heads)
    decay = jnp.exp(g)
    kt = jnp.concatenate([k, jnp.zeros((LANES - nb, dk), F32)], axis=0).T if nb < LANES else k.T
    qt = jnp.concatenate([q, jnp.zeros((LANES - nb, dk), F32)], axis=0).T if nb < LANES else q.T
    for b in range(nb):
        s = s_in_ref[b] * decay[b:b + 1, :]
        kcol = kt[:, b:b + 1]
        v_new = (v[b:b + 1, :] - _sum_rows(s * kcol)) * beta[b:b + 1, :]
        s = s + kcol * v_new
        if stack_layer is None:
            s_out_ref[b] = s
        else:
            for d in range(s_out_ref.shape[0]):
                s_out_ref[d, b] = s if d == stack_layer else jnp.zeros_like(s)
        osc[b:b + 1, :] = _sum_rows(s * qt[:, b:b + 1])
    o_ref[...] = _gated_out_norm(osc[...], z_ref[...], nw_ref[...]).astype(o_ref.dtype)


def _gdn_sample(qkvz, bg, row0, conv_buf, conv_w, norm_w, state, layer, n_heads, dk,
                state_out=None):
    H = n_heads
    nb_total = state.shape[1]
    nb = 16
    assert nb_total % nb == 0 and row0 % nb == 0
    r0 = row0 // nb
    n_hist = conv_buf.shape[1]

    def col(off):
        return pl.BlockSpec((nb, dk), lambda h, i, off=off: (i + r0, h + off))

    def buf(off):
        return pl.BlockSpec((None, n_hist, nb, dk), lambda h, i, off=off: (layer, 0, i, h + off))

    def cw(off):
        return pl.BlockSpec((None, conv_w.shape[1], dk), lambda h, i, off=off: (layer, 0, h + off))

    in_specs = [col(0), col(H), col(2 * H), col(3 * H),
                pl.BlockSpec((nb, LANES), lambda h, i: (i + r0, 0)),
                buf(0), buf(H), buf(2 * H),
                cw(0), cw(H), cw(2 * H),
                pl.BlockSpec((None, 1, dk), lambda h, i: (layer, 0, 0)),
                pl.BlockSpec((None, nb, None, dk, dk), lambda h, i: (layer, i, h, 0, 0))]
    args = [qkvz, qkvz, qkvz, qkvz, bg, conv_buf, conv_buf, conv_buf,
            conv_w, conv_w, conv_w, norm_w, state]
    aliases = {}
    if state_out is not None:
        in_specs.append(pl.BlockSpec(memory_space=pl.ANY))
        args.append(state_out)
        aliases = {len(args) - 1: 1}
    depth = state.shape[0]
    if state_out is None:
        stack_spec = pl.BlockSpec((depth, nb, None, dk, dk), lambda h, i: (0, i, h, 0, 0))
    else:
        stack_spec = pl.BlockSpec((None, nb, None, dk, dk), lambda h, i: (layer, i, h, 0, 0))
    nbytes = (4 + 2 * depth) * nb * dk * dk * 4 + 64 * nb * dk * 4 + 64 * dk * dk * 4
    return pl.pallas_call(
        functools.partial(_gdn_sample_kernel, n_heads=H,
                          stack_layer=layer if state_out is None else None),
        out_shape=(jax.ShapeDtypeStruct((nb_total, H * dk), BF16),
                   jax.ShapeDtypeStruct(state.shape, F32))
        + (jax.ShapeDtypeStruct((n_hist, nb_total, H * dk), F32),) * 3,
        grid=(H, nb_total // nb),
        in_specs=in_specs,
        out_specs=(pl.BlockSpec((nb, dk), lambda h, i: (i, h)), stack_spec)
        + (pl.BlockSpec((n_hist, nb, dk), lambda h, i: (0, i, h)),) * 3,
        scratch_shapes=[pltpu.VMEM((nb, dk), F32)],
        input_output_aliases=aliases,
        compiler_params=pltpu.CompilerParams(
            dimension_semantics=("parallel", "parallel"),
            vmem_limit_bytes=_vmem_limit(nbytes)),
        name="gdn_sample",
    )(*args)


def _s5_tables(lam_re, lam_im, log_dt, b_re, b_im, c_re, c_im):
    D, G, P = lam_re.shape
    gc = b_re.shape[-1]
    gpb = LANES // gc
    nblk = G // gpb
    dt = jnp.exp(log_dt)[..., None]
    mag = jnp.exp(lam_re * dt)
    ar = mag * jnp.cos(lam_im * dt)
    ai = mag * jnp.sin(lam_im * dt)
    nr = ar - 1.0
    den = lam_re * lam_re + lam_im * lam_im
    fr = (nr * lam_re + ai * lam_im) / den
    fi = (ai * lam_re - nr * lam_im) / den
    bbar_re = fr[..., None] * b_re - fi[..., None] * b_im
    bbar_im = fr[..., None] * b_im + fi[..., None] * b_re
    rg = lax.broadcasted_iota(jnp.int32, (gpb * gc, gpb * P), 0) // gc
    cg = lax.broadcasted_iota(jnp.int32, (gpb * gc, gpb * P), 1) // P
    diag = rg == cg

    def bmat(bb):
        t = jnp.swapaxes(bb.reshape(D, nblk, gpb * P, gc), 2, 3)
        return jnp.where(diag, jnp.tile(t, (1, 1, gpb, 1)), 0.0)

    def cmat(cc):
        t = jnp.swapaxes(cc.reshape(D, nblk, gpb, gc, P), 3, 4).reshape(D, nblk, gpb * P, gc)
        return jnp.where(diag.T, jnp.tile(t, (1, 1, 1, gpb)), 0.0)

    b_blk = jnp.concatenate([bmat(bbar_re), bmat(bbar_im)], axis=3).astype(BF16)
    c_blk = jnp.concatenate([cmat(c_re), -cmat(c_im)], axis=2).astype(BF16)
    return (b_blk, c_blk, ar.reshape(D, nblk, 1, gpb * P), ai.reshape(D, nblk, 1, gpb * P))


def _s5_prompt_kernel(u_ref, bblk_ref, cblk_ref, ar_ref, ai_ref, d_ref,
                      g5_ref, g5b_ref, xre_ref, xim_ref, up, xs0, xs1, ys):
    L = u_ref.shape[0]
    ns = ar_ref.shape[-1]
    nseg = S5_SEGMENTS
    seg = L // nseg
    R = xs0.shape[0] // max(xs0.shape[0] // (S5_BLOCK_STEPS * nseg), 1)
    steps = R // nseg
    n_pair = xs0.shape[0] // R
    n_blk = 2 * n_pair
    assert n_blk * R == L

    def gather(t, carry):
        up[pl.ds(pl.multiple_of(t * nseg, nseg), nseg), :] = u_ref[pl.ds(t, nseg, stride=seg), :]
        return carry

    lax.fori_loop(0, seg, gather, 0, unroll=8)
    bmat = bblk_ref[...]
    cmat = cblk_ref[...]
    ar = jnp.broadcast_to(ar_ref[...], (nseg, ns))
    ai = jnp.broadcast_to(ai_ref[...], (nseg, ns))

    def block(row0):
        return pl.ds(pl.multiple_of(row0, R), R)

    def project(buf, blk, row0):
        u = up[block(blk * R), :]
        buf[block(row0), :] = jnp.dot(u.astype(BF16), bmat, preferred_element_type=F32)

    def scan(buf, row0, x, store):
        for t in range(steps):
            rows = pl.ds(pl.multiple_of(row0 + t * nseg, nseg), nseg)
            r = buf[rows, :]
            xr, xi = x
            x = (ar * xr - ai * xi + r[:, :ns], ar * xi + ai * xr + r[:, ns:])
            if store:
                buf[rows, :] = jnp.concatenate(x, axis=1)
        return x

    def emit(buf, row0, blk):
        y = jnp.dot(buf[block(row0), :].astype(BF16), cmat, preferred_element_type=F32)
        ys[block(blk * R), :] = _gelu_tanh(y + d_ref[...] * up[block(blk * R), :])

    project(xs0, 0, 0)

    def pass1(p, e):
        row0 = p * R
        project(xs1, 2 * p + 1, row0)
        e = scan(xs0, row0, e, False)
        nxt = jnp.minimum(p + 1, n_pair - 1)
        project(xs0, 2 * nxt, nxt * R)
        return scan(xs1, row0, e, False)

    zero = jnp.zeros((nseg, ns), F32)
    er, ei = lax.fori_loop(0, n_pair, pass1, (zero, zero))
    pr, pi = ar_ref[...], ai_ref[...]
    for _ in range(seg.bit_length() - 1):
        pr, pi = pr * pr - pi * pi, 2.0 * pr * pi
    assert seg == 1 << (seg.bit_length() - 1)
    cr = [jnp.zeros((1, ns), F32)]
    ci = [jnp.zeros((1, ns), F32)]
    for s in range(nseg - 1):
        cr.append(er[s:s + 1] + pr * cr[s] - pi * ci[s])
        ci.append(ei[s:s + 1] + pr * ci[s] + pi * cr[s])
    x = (jnp.concatenate(cr, axis=0), jnp.concatenate(ci, axis=0))

    x = scan(xs0, 0, x, True)

    def pass2(p, x):
        row0 = p * R
        x = scan(xs1, row0, x, True)
        emit(xs0, row0, 2 * p)
        x = scan(xs0, row0 + R, x, True)
        emit(xs1, row0, 2 * p + 1)
        return x

    x = lax.fori_loop(0, n_pair - 1, pass2, x)
    last = (n_pair - 1) * R
    x = scan(xs1, last, x, True)
    emit(xs0, last, n_blk - 2)
    emit(xs1, last, n_blk - 1)
    xre_ref[...] = x[0][nseg - 1:nseg]
    xim_ref[...] = x[1][nseg - 1:nseg]
    for s in range(nseg):
        g5 = ys[pl.ds(s, seg, stride=nseg), :]
        g5_ref[s * seg:(s + 1) * seg, :] = g5
        g5b_ref[s * seg:(s + 1) * seg, :] = g5.astype(BF16)


def _s5_prompt(u_all, tables, d_skip, layer, batch, seq, n_ch):
    b_blk, c_blk, ar, ai = tables
    nblk, _, ns2 = b_blk.shape[1:]
    ns = ns2 // 2
    L = seq
    nbytes = (2 * L * LANES * 4 + 2 * L * LANES * 6 + 3 * L * ns2 * 4 + L * ns2 * 2
              + 8 * LANES * ns2 * 2)
    blkp = lambda r, c: pl.BlockSpec((None, None, r, c), lambda b, j: (layer, j, 0, 0))
    return pl.pallas_call(
        _s5_prompt_kernel,
        out_shape=(jax.ShapeDtypeStruct((batch * L, n_ch), F32),
                   jax.ShapeDtypeStruct((batch * L, n_ch), BF16),
                   jax.ShapeDtypeStruct((batch, 1, nblk * ns), F32),
                   jax.ShapeDtypeStruct((batch, 1, nblk * ns), F32)),
        grid=(batch, nblk),
        in_specs=[pl.BlockSpec((L, LANES), lambda b, j: (b, j)),
                  blkp(LANES, ns2), blkp(ns2, LANES), blkp(1, ns), blkp(1, ns),
                  pl.BlockSpec((None, 1, LANES), lambda b, j: (layer, 0, j))],
        out_specs=(pl.BlockSpec((L, LANES), lambda b, j: (b, j)),
                   pl.BlockSpec((L, LANES), lambda b, j: (b, j)),
                   pl.BlockSpec((None, 1, ns), lambda b, j: (b, 0, j)),
                   pl.BlockSpec((None, 1, ns), lambda b, j: (b, 0, j))),
        scratch_shapes=[pltpu.VMEM((L, LANES), F32),
                        pltpu.VMEM((L // 2, ns2), F32),
                        pltpu.VMEM((L // 2, ns2), F32),
                        pltpu.VMEM((L, LANES), F32)],
        compiler_params=pltpu.CompilerParams(
            dimension_semantics=("parallel", "parallel"),
            vmem_limit_bytes=_vmem_limit(nbytes)),
        name="s5_prompt",
    )(u_all, b_blk, c_blk, ar, ai, d_skip)


def _s5_sample_kernel(u_ref, bblk_ref, cblk_ref, ar_ref, ai_ref, d_ref, x0r_ref, x0i_ref,
                      g5_ref, g5b_ref, xre_ref, xim_ref):
    ns = ar_ref.shape[-1]
    u = u_ref[...]
    bu = jnp.dot(u.astype(BF16), bblk_ref[...], preferred_element_type=F32)
    ar, ai = ar_ref[...], ai_ref[...]
    x0r, x0i = x0r_ref[...], x0i_ref[...]
    xr = ar * x0r - ai * x0i + bu[:, :ns]
    xi = ar * x0i + ai * x0r + bu[:, ns:]
    xre_ref[...] = xr
    xim_ref[...] = xi
    x = jnp.concatenate([xr, xi], axis=1)
    y = jnp.dot(x.astype(BF16), cblk_ref[...], preferred_element_type=F32) + d_ref[...] * u
    g5 = _gelu_tanh(y)
    g5_ref[...] = g5
    g5b_ref[...] = g5.astype(BF16)


def _s5_sample(u_all, row0, tables, d_skip, x0_re, x0_im, layer, n_ch):
    b_blk, c_blk, ar, ai = tables
    nblk, _, ns2 = b_blk.shape[1:]
    ns = ns2 // 2
    nb = x0_re.shape[1]
    assert row0 % nb == 0
    r0 = row0 // nb
    blkp = lambda r, c: pl.BlockSpec((None, None, r, c), lambda j: (layer, j, 0, 0))
    nbytes = 16 * nb * ns2 * 4 + 8 * LANES * ns2 * 2
    return pl.pallas_call(
        _s5_sample_kernel,
        out_shape=(jax.ShapeDtypeStruct((nb, n_ch), F32),
                   jax.ShapeDtypeStruct((nb, n_ch), BF16),
                   jax.ShapeDtypeStruct((nb, nblk * ns), F32),
                   jax.ShapeDtypeStruct((nb, nblk * ns), F32)),
        grid=(nblk,),
        in_specs=[pl.BlockSpec((nb, LANES), lambda j: (r0, j)),
                  blkp(LANES, ns2), blkp(ns2, LANES), blkp(1, ns), blkp(1, ns),
                  pl.BlockSpec((None, 1, LANES), lambda j: (layer, 0, j)),
                  pl.BlockSpec((None, nb, ns), lambda j: (layer, 0, j)),
                  pl.BlockSpec((None, nb, ns), lambda j: (layer, 0, j))],
        out_specs=(pl.BlockSpec((nb, LANES), lambda j: (0, j)),
                   pl.BlockSpec((nb, LANES), lambda j: (0, j)),
                   pl.BlockSpec((nb, ns), lambda j: (0, j)),
                   pl.BlockSpec((nb, ns), lambda j: (0, j))),
        compiler_params=pltpu.CompilerParams(
            dimension_semantics=("parallel",), vmem_limit_bytes=_vmem_limit(nbytes)),
        name="s5_sample",
    )(u_all, b_blk, c_blk, ar, ai, d_skip, x0_re, x0_im)


def kernel(x_prompt, x_sample, state_dn_conv, state_dn_ssm, state_s5_re, state_s5_im, norm1, w_in, dn_conv_w, dn_a_log, dn_dt_bias, dn_norm_w, w_br_dn, s5_lam_re, s5_lam_im, s5_log_dt, s5_b_re, s5_b_im, s5_c_re, s5_c_im, s5_d, w_glu, w_br_s5, w_out, norm2, w_ffn_gate, w_ffn_up, w_ffn_down, norm_f):
    batch, seq, d_model = x_prompt.shape
    nb, dec_seq, _ = x_sample.shape
    assert dec_seq == 1
    depth, _, n_heads, dk, dv = state_dn_ssm.shape
    assert dk == LANES and dv == LANES and seq % GDN_CHUNK == 0 and seq % S5_SEGMENTS == 0
    qk_dim = n_heads * dk
    conv_ch = dn_conv_w.shape[2]
    assert conv_ch == 3 * qk_dim
    n_ch = s5_d.shape[1]
    n_groups, n_state = s5_lam_re.shape[1:]
    ffn = w_ffn_gate.shape[2]
    mp = batch * seq
    m = mp + nb
    z_end = 4 * qk_dim
    rest0 = z_end + 2 * n_heads
    assert w_in.shape[2] == rest0 + n_ch + 2 * d_model and 2 * n_heads <= LANES

    x = (x_prompt.reshape(mp, d_model), x_sample.reshape(nb, d_model))

    w_in_t = jnp.swapaxes(w_in, 1, 2)
    w_bg = jnp.pad(w_in_t[:, z_end:rest0, :], ((0, 0), (0, LANES - 2 * n_heads), (0, 0)))
    w_s5 = w_in_t[:, rest0:rest0 + n_ch, :]
    gate_tn = 512
    gate0 = rest0 + n_ch
    gate_base = gate0 - gate0 % gate_tn
    gate_shift = gate0 - gate_base
    gate_cols = -(-(gate_shift + 2 * d_model) // gate_tn) * gate_tn
    assert gate_shift < LANES and gate_base + gate_cols - gate_tn < w_in.shape[2]
    conv_hist = jnp.swapaxes(state_dn_conv, 1, 2)
    pad_heads = lambda a: jnp.pad(a, ((0, 0), (n_heads, LANES - 2 * n_heads)))[:, None, :]
    alog_pad = pad_heads(dn_a_log)
    dtb_pad = pad_heads(dn_dt_bias)
    x0_re = state_s5_re.reshape(depth, nb, n_groups * n_state)
    x0_im = state_s5_im.reshape(depth, nb, n_groups * n_state)

    tm = _pick_tile(m, 1664, 64)
    norm_w3 = dn_norm_w[:, None, :]
    d_skip3 = s5_d[:, None, :]
    tables = _s5_tables(s5_lam_re, s5_lam_im, s5_log_dt, s5_b_re, s5_b_im, s5_c_re, s5_c_im)
    outs = {k: [] for k in ("p_conv", "p_ssm", "p_re", "p_im", "s_conv", "s_re", "s_im")}
    s_ssm = None
    for l in range(depth):
        h = _rmsnorm(x, norm1[l][None, :], BF16)
        qkvz = _fused_matmul([h], [(0, w_in_t, l, 0, True)], [], _ep_identity, z_end, F32,
                             tm=tm, tn=512, name="in_qkvz")
        bg = _fused_matmul([h], [(0, w_bg, l, 0, True)], [], _ep_beta_decay(n_heads), LANES, F32,
                           tm=tm, tn=LANES, name="in_bg",
                           col_params=[(alog_pad, l), (dtb_pad, l)])
        s5u = _fused_matmul([h], [(0, w_s5, l, 0, True)], [], _ep_identity, n_ch, F32,
                            tm=tm, tn=512, name="in_s5")
        gates = _fused_matmul([h], [(0, w_in_t, l, gate_base, True)], [], _ep_identity, gate_cols, F32,
                              tm=tm, tn=gate_tn, name="in_gates",
                              valid_cols=gate_shift + 2 * d_model)

        o_p, ssm_p, *pc = _gdn_prompt(qkvz, bg, dn_conv_w, norm_w3, l, batch, seq, n_heads, dk)
        o_s, s_ssm, *sc = _gdn_sample(qkvz, bg, mp, conv_hist, dn_conv_w, norm_w3,
                                      state_dn_ssm, l, n_heads, dk, state_out=s_ssm)
        outs["p_conv"].append(jnp.concatenate(pc, axis=-1))
        outs["s_conv"].append(jnp.concatenate(sc, axis=-1))
        outs["p_ssm"].append(ssm_p)

        g5_p, g5b_p, re_p, im_p = _s5_prompt(s5u, tables, d_skip3, l, batch, seq, n_ch)
        g5_s, g5b_s, re_s, im_s = _s5_sample(s5u, mp, tables, d_skip3, x0_re, x0_im, l, n_ch)
        outs["p_re"].append(re_p.reshape(batch, n_groups, n_state))
        outs["p_im"].append(im_p.reshape(batch, n_groups, n_state))
        outs["s_re"].append(re_s.reshape(nb, n_groups, n_state))
        outs["s_im"].append(im_s.reshape(nb, n_groups, n_state))
        g5g = _fused_matmul([(g5b_p, g5b_s)], [(0, w_glu, l, 0)], [((g5_p, g5_s), 0)],
                            _ep_glu_self, n_ch, BF16, tm=tm, tn=512, name="s5_glu")

        merged = _fused_matmul([(o_p, o_s), g5g], [(0, w_br_dn, l, 0), (1, w_br_s5, l, 0)],
                               [(gates, gate_shift), (gates, gate_shift + d_model)],
                               _ep_gated_merge,
                               d_model, BF16, tm=tm, tn=512, name="branch_merge")
        x = _fused_matmul([merged], [(0, w_out, l, 0)], [(x, 0)], _ep_residual, d_model, F32,
                          tm=tm, tn=512, name="out_proj")

        hmid = _fused_matmul([x], [(0, w_ffn_gate, l, 0), (0, w_ffn_up, l, 0)], [], _ep_swiglu,
                             ffn, BF16, tm=tm, tn=256, name="ffn_up", norm_w=norm2[l][None, :])
        x = _matmul_residual_wstat(hmid, w_ffn_down, l, x, tm=_pick_tile(m, 416, 16), tn=512,
                                   name="ffn_down")

    y_p, y_s = _rmsnorm(x, norm_f[None, :], F32, split_rows=(mp, nb))
    st = lambda k: jnp.stack(outs[k])
    return (y_p.reshape(batch, seq, d_model), y_s.reshape(nb, 1, d_model),
            st("p_conv"), st("p_ssm"), st("p_re"), st("p_im"),
            jnp.swapaxes(st("s_conv"), 1, 2), s_ssm, st("s_re"), st("s_im"))
```

```python
import functools
import math

import jax
import jax.numpy as jnp
from jax import lax
from jax.experimental import pallas as pl
from jax.experimental.pallas import tpu as pltpu

F32 = jnp.float32
BF16 = jnp.bfloat16

NORM_EPS = 1e-6
L2_EPS = 1e-6
LANES = 128
SUBLANES = 8
VMEM_CAP_BYTES = 56 * 1024 * 1024
GDN_CHUNK = 128
S5_SEGMENTS = SUBLANES
GDN_HEADS_PER_STEP = 2
GDN_CHUNKS_PER_GROUP = 16
S5_BLOCK_STEPS = 32


def _vmem_limit(nbytes):
    return int(min(VMEM_CAP_BYTES, nbytes * 5 // 4 + (4 << 20)))


def _pick_tile(n, target, mult):
    best = None
    for t in range(mult, min(n, target) + 1, mult):
        if n % t == 0:
            best = t
    return best if best is not None else n


def _sigmoid(x):
    return 1.0 / (1.0 + jnp.exp(-x))


def _silu(x):
    return x * _sigmoid(x)


def _softplus(x):
    return jnp.maximum(x, 0.0) + jnp.log1p(jnp.exp(-jnp.abs(x)))


def _gelu_tanh(x):
    c = math.sqrt(2.0 / math.pi)
    return 0.5 * x * (1.0 + jnp.tanh(c * (x + 0.044715 * (x * x * x))))


def _bdot(a, b):
    return jnp.dot(a.astype(BF16), b.astype(BF16), preferred_element_type=F32)


def _bdot_nt(a, b):
    return lax.dot_general(a.astype(BF16), b.astype(BF16), (((1,), (1,)), ((), ())),
                           preferred_element_type=F32)


def _rows_of(op):
    return op[0].shape[0] + op[1].shape[0] if isinstance(op, tuple) else op.shape[0]


def _row_specs(op, tm, ncols, index_map):
    if not isinstance(op, tuple):
        return [pl.BlockSpec((tm, ncols), index_map)], [op], None
    p, s = op
    tail = p.shape[0] % tm
    assert tail + s.shape[0] == tm and tail % 16 == 0, (p.shape, s.shape, tm)

    def s_map(*idx):
        return (0,) + tuple(index_map(*idx)[1:])

    return ([pl.BlockSpec((tm, ncols), index_map), pl.BlockSpec((s.shape[0], ncols), s_map)],
            [p, s], tail)


def _load_rows(refs, tail, last):
    if tail is None or not last:
        return refs[0][...]
    return jnp.concatenate([refs[0][:tail, :], refs[1][...]], axis=0)


def _lane_window(parts, shift, width):
    x = parts[0] if len(parts) == 1 else jnp.concatenate(parts, axis=1)
    if shift:
        x = pltpu.roll(x, x.shape[1] - shift, 1)
    return x[:, :width]


def _on_row_tiles(i, n_tiles, any_split, body):
    if not any_split:
        body(False)
        return
    if n_tiles > 1:
        pl.when(i < n_tiles - 1)(lambda: body(False))
    pl.when(i == n_tiles - 1)(lambda: body(True))


def _rmsnorm_kernel(*refs, tail, n_tiles, out_tail):
    n_out = 1 if out_tail is None else 2
    x_refs, w_ref, o_refs = refs[:-1 - n_out], refs[-1 - n_out], refs[-n_out:]

    def body(last):
        x = _load_rows(x_refs, tail, last)
        y = x * lax.rsqrt(jnp.mean(x * x, axis=-1, keepdims=True) + NORM_EPS)
        y = (y * w_ref[...]).astype(o_refs[0].dtype)
        if out_tail is None:
            o_refs[0][...] = y
        elif not last:
            o_refs[0][...] = y
        else:
            o_refs[0][:out_tail, :] = y[:out_tail]
            o_refs[1][...] = y[out_tail:]

    _on_row_tiles(pl.program_id(0), n_tiles, tail is not None or out_tail is not None, body)


def _rmsnorm(x, w_row, out_dtype, split_rows=None):
    m = _rows_of(x)
    d = w_row.shape[1]
    tr = _pick_tile(m, 832, 64)
    specs, arrs, tail = _row_specs(x, tr, d, lambda i: (i, 0))
    nbytes = 2 * tr * d * 4 + 2 * tr * d * jnp.dtype(out_dtype).itemsize + 3 * tr * d * 4
    if split_rows is None:
        out_shape = jax.ShapeDtypeStruct((m, d), out_dtype)
        out_specs = pl.BlockSpec((tr, d), lambda i: (i, 0))
        out_tail = None
    else:
        mp, nb = split_rows
        out_tail = mp % tr
        assert mp + nb == m and out_tail + nb == tr
        out_shape = (jax.ShapeDtypeStruct((mp, d), out_dtype),
                     jax.ShapeDtypeStruct((nb, d), out_dtype))
        out_specs = (pl.BlockSpec((tr, d), lambda i: (i, 0)),
                     pl.BlockSpec((nb, d), lambda i: (0, 0)))
    return pl.pallas_call(
        functools.partial(_rmsnorm_kernel, tail=tail, n_tiles=m // tr, out_tail=out_tail),
        out_shape=out_shape,
        grid=(m // tr,),
        in_specs=specs + [pl.BlockSpec((1, d), lambda i: (0, 0))],
        out_specs=out_specs,
        compiler_params=pltpu.CompilerParams(
            dimension_semantics=("arbitrary",), vmem_limit_bytes=_vmem_limit(nbytes)),
        name="rmsnorm",
    )(*arrs, w_row)


def _mm_kernel(*refs, a_idx, w_transposed, a_groups, e_groups, n_tiles, epilogue, normed,
               zero_tail_from):
    pos = 0
    a_refs = []
    for n, _ in a_groups:
        a_refs.append(refs[pos:pos + n])
        pos += n
    w_refs = refs[pos:pos + len(a_idx)]
    pos += len(a_idx)
    e_refs = []
    for n, _, _ in e_groups:
        e_refs.append(refs[pos:pos + n])
        pos += n
    if normed:
        nw_ref, o_ref, h_scr = refs[pos:pos + 3]
    else:
        o_ref = refs[pos]
    any_split = any(g[1] is not None for g in a_groups + e_groups)

    def load_extra(r, tail, shift, last):
        if shift:
            return _lane_window([r[0][...], r[1][...]], shift, o_ref.shape[1])
        return _load_rows(r, tail, last)

    def body(last):
        if normed:
            @pl.when(pl.program_id(1) == 0)
            def _():
                x = _load_rows(a_refs[0], a_groups[0][1], last)
                y = x * lax.rsqrt(jnp.mean(x * x, axis=-1, keepdims=True) + NORM_EPS)
                h_scr[...] = (y * nw_ref[...]).astype(BF16)

            a_vals = [h_scr[...]]
        else:
            a_vals = [_load_rows(r, t, last) for r, (_, t) in zip(a_refs, a_groups)]
        parts = [(_bdot_nt if wt else _bdot)(a_vals[ai], w[...])
                 for ai, wt, w in zip(a_idx, w_transposed, w_refs)]
        e_vals = [load_extra(r, t, sh, last) for r, (_, t, sh) in zip(e_refs, e_groups)]
        o_ref[...] = epilogue(parts, e_vals).astype(o_ref.dtype)
        if zero_tail_from is not None:
            @pl.when(pl.program_id(1) == pl.num_programs(1) - 1)
            def _():
                o_ref[:, zero_tail_from:] = jnp.zeros(
                    (o_ref.shape[0], o_ref.shape[1] - zero_tail_from), o_ref.dtype)

    _on_row_tiles(pl.program_id(0), n_tiles, any_split, body)


def _fused_matmul(a_list, w_list, extras, epilogue, n_out, out_dtype, *, tm, tn, name,
                  col_params=(), norm_w=None, valid_cols=None):
    m = _rows_of(a_list[0])
    assert m % tm == 0 and n_out % tn == 0
    in_specs, args, a_groups, e_groups = [], [], [], []
    kdims = []
    for a in a_list:
        kd = (a[0] if isinstance(a, tuple) else a).shape[1]
        specs, arrs, tail = _row_specs(a, tm, kd, lambda i, j: (i, 0))
        assert _rows_of(a) == m
        in_specs += specs
        args += arrs
        a_groups.append((len(arrs), tail))
        kdims.append(kd)
    w_list = [tuple(e) + (False,) * (5 - len(e)) for e in w_list]
    for ai, w, layer, col0, transposed in w_list:
        assert col0 % tn == 0 and w.shape[2 if transposed else 1] == kdims[ai]
        if transposed:
            spec = pl.BlockSpec((None, tn, w.shape[2]),
                                lambda i, j, layer=layer, off=col0 // tn: (layer, j + off, 0))
        else:
            spec = pl.BlockSpec((None, w.shape[1], tn),
                                lambda i, j, layer=layer, off=col0 // tn: (layer, 0, j + off))
        in_specs.append(spec)
        args.append(w)
    for e, col0 in extras:
        shift = col0 % tn
        assert shift < LANES and _rows_of(e) == m
        specs, arrs, tail = _row_specs(e, tm, tn, lambda i, j, off=col0 // tn: (i, j + off))
        if shift:
            assert not isinstance(e, tuple) and tn % LANES == 0
            specs.append(pl.BlockSpec(
                (tm, LANES), lambda i, j, off=col0 // tn: (i, (j + off + 1) * (tn // LANES))))
            arrs.append(e)
        in_specs += specs
        args += arrs
        e_groups.append((len(arrs), tail, shift))
    for p, layer in col_params:
        in_specs.append(pl.BlockSpec((None, 1, tn), lambda i, j, layer=layer: (layer, 0, j)))
        args.append(p)
        e_groups.append((1, None, 0))
    normed = norm_w is not None
    scratch = []
    if normed:
        assert len(a_list) == 1
        in_specs.append(pl.BlockSpec((1, kdims[0]), lambda i, j: (0, 0)))
        args.append(norm_w)
        scratch.append(pltpu.VMEM((tm, kdims[0]), BF16))
    osz = jnp.dtype(out_dtype).itemsize
    nbytes = (sum(3 * tm * kd * (6 if normed else 2) for kd in kdims)
              + sum(kdims[ai] * tn * (2 * w.dtype.itemsize + 2) for ai, w, _, _, _ in w_list)
              + sum(3 * tm * tn * 4 for _ in extras)
              + 2 * tm * tn * osz + (2 + len(w_list)) * tm * tn * 4)
    kern = functools.partial(_mm_kernel, a_idx=tuple(e[0] for e in w_list),
                             w_transposed=tuple(e[4] for e in w_list), a_groups=tuple(a_groups), e_groups=tuple(e_groups),
                             n_tiles=m // tm, epilogue=epilogue, normed=normed,
                             zero_tail_from=None if valid_cols is None else valid_cols % tn)
    assert valid_cols is None or n_out - tn < valid_cols < n_out
    return pl.pallas_call(
        kern,
        out_shape=jax.ShapeDtypeStruct((m, n_out), out_dtype),
        grid=(m // tm, n_out // tn),
        in_specs=in_specs,
        out_specs=pl.BlockSpec((tm, tn), lambda i, j: (i, j)),
        scratch_shapes=scratch,
        compiler_params=pltpu.CompilerParams(
            dimension_semantics=("parallel", "arbitrary" if normed else "parallel"),
            vmem_limit_bytes=_vmem_limit(nbytes)),
        name=name,
    )(*args)


def _mm_residual_wstat_kernel(a_ref, w_ref, x_ref, o_ref, wb):
    @pl.when(pl.program_id(1) == 0)
    def _():
        wb[...] = w_ref[...].astype(BF16)

    o_ref[...] = x_ref[...] + jnp.dot(a_ref[...], wb[...], preferred_element_type=F32)


def _matmul_residual_wstat(a, w, layer, x, *, tm, tn, name):
    m, kdim = a.shape
    n_out = w.shape[2]
    assert m % tm == 0 and n_out % tn == 0 and w.shape[1] == kdim
    nbytes = 2 * tm * kdim * 2 + kdim * tn * (w.dtype.itemsize + 2) + 6 * tm * tn * 4
    return pl.pallas_call(
        _mm_residual_wstat_kernel,
        out_shape=jax.ShapeDtypeStruct((m, n_out), F32),
        grid=(n_out // tn, m // tm),
        in_specs=[pl.BlockSpec((tm, kdim), lambda j, i: (i, 0)),
                  pl.BlockSpec((None, kdim, tn), lambda j, i: (layer, 0, j),
                               pipeline_mode=pl.Buffered(1)),
                  pl.BlockSpec((tm, tn), lambda j, i: (i, j))],
        out_specs=pl.BlockSpec((tm, tn), lambda j, i: (i, j)),
        scratch_shapes=[pltpu.VMEM((kdim, tn), BF16)],
        compiler_params=pltpu.CompilerParams(
            dimension_semantics=("parallel", "arbitrary"),
            vmem_limit_bytes=_vmem_limit(nbytes)),
        name=name,
    )(a, w, x)


def _ep_identity(accs, extras):
    return accs[0]


def _ep_residual(accs, extras):
    return extras[0] + accs[0]


def _ep_swiglu(accs, extras):
    return _silu(accs[0]) * accs[1]


def _ep_glu_self(accs, extras):
    g5 = extras[0]
    return g5 * _sigmoid(accs[0])


def _ep_gated_merge(accs, extras):
    return _sigmoid(extras[0]) * accs[0] + _sigmoid(extras[1]) * accs[1]


def _ep_beta_decay(n_heads):
    def ep(accs, extras):
        acc = accs[0]
        alog_row, dtb_row = extras
        lane = lax.broadcasted_iota(jnp.int32, acc.shape, 1)
        return jnp.where(lane < n_heads, _sigmoid(acc),
                         -jnp.exp(alog_row) * _softplus(acc + dtb_row))
    return ep


def _head_columns(bg, head, n_heads):
    lane = lax.broadcasted_iota(jnp.int32, bg.shape, 1)
    beta = jnp.sum(jnp.where(lane == head, bg, 0.0), axis=-1, keepdims=True)
    g = jnp.sum(jnp.where(lane == head + n_heads, bg, 0.0), axis=-1, keepdims=True)
    return beta, g


def _l2norm_rows(x):
    return x * lax.rsqrt(jnp.sum(x * x, axis=-1, keepdims=True) + L2_EPS)


def _sum_rows(x):
    acc = x[0:SUBLANES]
    for i in range(1, x.shape[0] // SUBLANES):
        acc = acc + x[i * SUBLANES:(i + 1) * SUBLANES]
    shift = SUBLANES // 2
    while shift:
        acc = acc + pltpu.roll(acc, shift, 0)
        shift //= 2
    return acc[0:1]


def _gated_out_norm(o, z, nw_row):
    y = o * lax.rsqrt(jnp.mean(o * o, axis=-1, keepdims=True) + NORM_EPS)
    return y * nw_row * _silu(z)


def _gdn_prompt_kernel(q_ref, k_ref, v_ref, z_ref, bg_ref, cwq_ref, cwk_ref, cwv_ref, nw_ref,
                       o_ref, s_ref, pcq_ref, pck_ref, pcv_ref,
                       qn, kn, vn, gb, bb, us, ws, qks, qds, kdt, gl, osc, *, n_heads):
    hb, L, dk = qn.shape
    n_hist = pcq_ref.shape[0]
    for x_ref, pc_ref in ((q_ref, pcq_ref), (k_ref, pck_ref), (v_ref, pcv_ref)):
        pc_ref[...] = x_ref[L - n_hist:L, :]
    C = GDN_CHUNK
    n_chunks = L // C
    H8 = range(hb)

    row8 = lax.broadcasted_iota(jnp.int32, (SUBLANES, dk), 0)

    def conv_silu(x_ref, cw_ref, cols):
        cw = cw_ref[:, cols]
        n_taps = cw.shape[0]
        assert n_taps - 1 <= SUBLANES
        tap = lambda j: cw[n_taps - 1 - j:n_taps - j, :]
        head8 = x_ref[0:SUBLANES, cols]
        lo = head8 * tap(0)
        hi = x_ref[SUBLANES:L, cols] * tap(0)
        for j in range(1, n_taps):
            lo = lo + jnp.where(row8 >= j, pltpu.roll(head8, j, 0), 0.0) * tap(j)
            hi = hi + x_ref[pl.ds(SUBLANES - j, L - SUBLANES), cols] * tap(j)
        return _silu(jnp.concatenate([lo, hi], axis=0))

    bg = bg_ref[...]
    for hh in H8:
        cols = slice(hh * dk, (hh + 1) * dk)
        qn[hh] = _l2norm_rows(conv_silu(q_ref, cwq_ref, cols)) * (dk ** -0.5)
        kn[hh] = _l2norm_rows(conv_silu(k_ref, cwk_ref, cols))
        vn[hh] = conv_silu(v_ref, cwv_ref, cols)
        beta, g = _head_columns(bg, pl.program_id(1) * hb + hh, n_heads)
        bb[hh] = jnp.broadcast_to(beta, (L, dk))
        gb[hh] = jnp.broadcast_to(g, (L, dk))

    ri = lax.broadcasted_iota(jnp.int32, (C, C), 0)
    ci = lax.broadcasted_iota(jnp.int32, (C, C), 1)
    causal = ri >= ci
    strict = ri > ci
    tri_incl = jnp.where(causal, 1.0, 0.0).astype(F32)
    eye = jnp.where(ri == ci, 1.0, 0.0).astype(F32)
    level_masks = []
    n = 1
    while n < C:
        sh = n.bit_length() - 1
        same_2n = (ri >> (sh + 1)) == (ci >> (sh + 1))
        diff_n = (ri >> sh) != (ci >> sh)
        level_masks.append(jnp.where(same_2n & diff_n & strict, 1.0, 0.0).astype(F32))
        n *= 2

    group = math.gcd(n_chunks, GDN_CHUNKS_PER_GROUP)

    def intra_group(hh, i):
        G = range(group)
        rows = [pl.ds(pl.multiple_of((i * group + j) * C, C), C) for j in G]
        k = [kn[hh, r, :] for r in rows]
        bet = [bb[hh, r, :] for r in rows]
        gcb = [jnp.dot(tri_incl, gb[hh, r, :], precision=lax.Precision.HIGHEST,
                       preferred_element_type=F32) for r in rows]
        gamma = [jnp.where(causal, jnp.exp(jnp.minimum(g - g.T, 0.0)), 0.0) for g in gcb]
        kb = [k[j] * bet[j] for j in G]
        a_mat = [jnp.where(strict, _bdot_nt(kb[j], k[j]) * gamma[j], 0.0) for j in G]
        q = [qn[hh, r, :] for r in rows]
        for j in G:
            qks[hh, rows[j], :] = _bdot_nt(q[j], k[j]) * gamma[j]
        t = [eye - a * level_masks[0] for a in a_mat]
        for m in level_masks[1:]:
            x = [_bdot(a_mat[j] * m, t[j]) for j in G]
            t = [t[j] - _bdot(t[j], x[j]) for j in G]
        eg = [jnp.exp(g) for g in gcb]
        for j in G:
            us[hh, rows[j], :] = _bdot(t[j], vn[hh, rows[j], :] * bet[j])
        for j in G:
            ws[hh, rows[j], :] = _bdot(t[j], kb[j] * eg[j])
        for j in G:
            qds[hh, rows[j], :] = q[j] * eg[j]
            g_last = gcb[j][C - 1:C, :]
            kdt[hh, rows[j], :] = (k[j] * jnp.exp(g_last - gcb[j])).T
            gl[hh, pl.ds(pl.multiple_of((i * group + j) * SUBLANES, SUBLANES), SUBLANES), :] = (
                jnp.broadcast_to(jnp.exp(g_last), (SUBLANES, dk)))

    for hh in H8:
        if n_chunks == group:
            intra_group(hh, 0)
        else:
            def body(i, carry, hh=hh):
                intra_group(hh, i)
                return carry
            lax.fori_loop(0, n_chunks // group, body, 0)

    def inter(c, states):
        rows = pl.ds(pl.multiple_of(c * C, C), C)
        ws_s = [_bdot(ws[hh, rows, :], states[hh]) for hh in H8]
        qd_s = [_bdot(qds[hh, rows, :], states[hh]) for hh in H8]
        v_new = [us[hh, rows, :] - ws_s[hh] for hh in H8]
        for hh in H8:
            osc[hh, rows, :] = qd_s[hh] + _bdot(qks[hh, rows, :], v_new[hh])
        decay = [gl[hh, pl.ds(pl.multiple_of(c * SUBLANES, SUBLANES), 1), :] for hh in H8]
        return tuple(states[hh] * decay[hh] + _bdot(kdt[hh, rows, :], v_new[hh]) for hh in H8)

    s_fin = lax.fori_loop(0, n_chunks, inter, tuple(jnp.zeros((dk, dk), F32) for _ in H8))
    for hh in H8:
        cols = slice(hh * dk, (hh + 1) * dk)
        s_ref[hh] = s_fin[hh]
        o_ref[:, cols] = _gated_out_norm(osc[hh], z_ref[:, cols], nw_ref[...]).astype(o_ref.dtype)


def _gdn_prompt(qkvz, bg, conv_w, norm_w, layer, batch, seq, n_heads, dk):
    H = n_heads
    L = seq
    hb = GDN_HEADS_PER_STEP if H % GDN_HEADS_PER_STEP == 0 else 1
    hg = H // hb
    n_hist = conv_w.shape[1] - 1

    def col(sec):
        return pl.BlockSpec((L, hb * dk), lambda b, h, sec=sec: (b, h + sec * hg))

    def cw(sec):
        return pl.BlockSpec((None, conv_w.shape[1], hb * dk),
                            lambda b, h, sec=sec: (layer, 0, h + sec * hg))

    scr = lambda: pltpu.VMEM((hb, L, dk), F32)
    nbytes = hb * (2 * 4 * L * dk * 4 + 11 * L * dk * 4 + 2 * L * dk * 2) + 2 * L * LANES * 4 \
        + 60 * GDN_CHUNK * GDN_CHUNK * 4
    return pl.pallas_call(
        functools.partial(_gdn_prompt_kernel, n_heads=H),
        out_shape=(jax.ShapeDtypeStruct((batch * L, H * dk), BF16),
                   jax.ShapeDtypeStruct((batch, H, dk, dk), F32))
        + (jax.ShapeDtypeStruct((batch, n_hist, H * dk), F32),) * 3,
        grid=(batch, hg),
        in_specs=[col(0), col(1), col(2), col(3),
                  pl.BlockSpec((L, LANES), lambda b, h: (b, 0)),
                  cw(0), cw(1), cw(2),
                  pl.BlockSpec((None, 1, dk), lambda b, h: (layer, 0, 0))],
        out_specs=(pl.BlockSpec((L, hb * dk), lambda b, h: (b, h)),
                   pl.BlockSpec((None, hb, dk, dk), lambda b, h: (b, h, 0, 0)))
        + (pl.BlockSpec((None, n_hist, hb * dk), lambda b, h: (b, 0, h)),) * 3,
        scratch_shapes=[scr(), scr(), scr(), scr(), scr(),
                        scr(), scr(), scr(), scr(), scr(),
                        pltpu.VMEM((hb, L // GDN_CHUNK * SUBLANES, dk), F32),
                        scr()],
        compiler_params=pltpu.CompilerParams(
            dimension_semantics=("parallel", "parallel"),
            vmem_limit_bytes=_vmem_limit(nbytes)),
        name="gdn_prompt",
    )(qkvz, qkvz, qkvz, qkvz, bg, conv_w, conv_w, conv_w, norm_w)


def _gdn_sample_kernel(q_ref, k_ref, v_ref, z_ref, bg_ref, bq_ref, bk_ref, bv_ref,
                       cwq_ref, cwk_ref, cwv_ref, nw_ref, s_in_ref, *rest, n_heads, stack_layer):
    o_ref, s_out_ref, cq_ref, ck_ref, cv_ref, osc = rest[-6:]
    nb, dk = q_ref.shape
    head = pl.program_id(0)

    def conv_silu(x_ref, buf_ref, new_ref, cw_ref):
        cw = cw_ref[...]
        n_hist = buf_ref.shape[0]
        x = x_ref[...]
        y = x * cw[n_hist:n_hist + 1, :]
        for i in range(n_hist):
            row = buf_ref[i]
            y = y + row * cw[i:i + 1, :]
            if i > 0:
                new_ref[i - 1] = row
        new_ref[n_hist - 1] = x
        return _silu(y)

    q = _l2norm_rows(conv_silu(q_ref, bq_ref, cq_ref, cwq_ref)) * (dk ** -0.5)
    k = _l2norm_rows(conv_silu(k_ref, bk_ref, ck_ref, cwk_ref))
    v = conv_silu(v_ref, bv_ref, cv_ref, cwv_ref)
    beta, g = _head_columns(bg_ref[...], head, n_heads)
    decay = jnp.exp(g)
    kt = jnp.concatenate([k, jnp.zeros((LANES - nb, dk), F32)], axis=0).T if nb < LANES else k.T
    qt = jnp.concatenate([q, jnp.zeros((LANES - nb, dk), F32)], axis=0).T if nb < LANES else q.T
    for b in range(nb):
        s = s_in_ref[b] * decay[b:b + 1, :]
        kcol = kt[:, b:b + 1]
        v_new = (v[b:b + 1, :] - _sum_rows(s * kcol)) * beta[b:b + 1, :]
        s = s + kcol * v_new
        if stack_layer is None:
            s_out_ref[b] = s
        else:
            for d in range(s_out_ref.shape[0]):
                s_out_ref[d, b] = s if d == stack_layer else jnp.zeros_like(s)
        osc[b:b + 1, :] = _sum_rows(s * qt[:, b:b + 1])
    o_ref[...] = _gated_out_norm(osc[...], z_ref[...], nw_ref[...]).astype(o_ref.dtype)


def _gdn_sample(qkvz, bg, row0, conv_buf, conv_w, norm_w, state, layer, n_heads, dk,
                state_out=None):
    H = n_heads
    nb_total = state.shape[1]
    nb = 16
    assert nb_total % nb == 0 and row0 % nb == 0
    r0 = row0 // nb
    n_hist = conv_buf.shape[1]

    def col(off):
        return pl.BlockSpec((nb, dk), lambda h, i, off=off: (i + r0, h + off))

    def buf(off):
        return pl.BlockSpec((None, n_hist, nb, dk), lambda h, i, off=off: (layer, 0, i, h + off))

    def cw(off):
        return pl.BlockSpec((None, conv_w.shape[1], dk), lambda h, i, off=off: (layer, 0, h + off))

    in_specs = [col(0), col(H), col(2 * H), col(3 * H),
                pl.BlockSpec((nb, LANES), lambda h, i: (i + r0, 0)),
                buf(0), buf(H), buf(2 * H),
                cw(0), cw(H), cw(2 * H),
                pl.BlockSpec((None, 1, dk), lambda h, i: (layer, 0, 0)),
                pl.BlockSpec((None, nb, None, dk, dk), lambda h, i: (layer, i, h, 0, 0))]
    args = [qkvz, qkvz, qkvz, qkvz, bg, conv_buf, conv_buf, conv_buf,
            conv_w, conv_w, conv_w, norm_w, state]
    aliases = {}
    if state_out is not None:
        in_specs.append(pl.BlockSpec(memory_space=pl.ANY))
        args.append(state_out)
        aliases = {len(args) - 1: 1}
    depth = state.shape[0]
    if state_out is None:
        stack_spec = pl.BlockSpec((depth, nb, None, dk, dk), lambda h, i: (0, i, h, 0, 0))
    else:
        stack_spec = pl.BlockSpec((None, nb, None, dk, dk), lambda h, i: (layer, i, h, 0, 0))
    nbytes = (4 + 2 * depth) * nb * dk * dk * 4 + 64 * nb * dk * 4 + 64 * dk * dk * 4
    return pl.pallas_call(
        functools.partial(_gdn_sample_kernel, n_heads=H,
                          stack_layer=layer if state_out is None else None),
        out_shape=(jax.ShapeDtypeStruct((nb_total, H * dk), BF16),
                   jax.ShapeDtypeStruct(state.shape, F32))
        + (jax.ShapeDtypeStruct((n_hist, nb_total, H * dk), F32),) * 3,
        grid=(H, nb_total // nb),
        in_specs=in_specs,
        out_specs=(pl.BlockSpec((nb, dk), lambda h, i: (i, h)), stack_spec)
        + (pl.BlockSpec((n_hist, nb, dk), lambda h, i: (0, i, h)),) * 3,
        scratch_shapes=[pltpu.VMEM((nb, dk), F32)],
        input_output_aliases=aliases,
        compiler_params=pltpu.CompilerParams(
            dimension_semantics=("parallel", "parallel"),
            vmem_limit_bytes=_vmem_limit(nbytes)),
        name="gdn_sample",
    )(*args)


def _s5_tables(lam_re, lam_im, log_dt, b_re, b_im, c_re, c_im):
    D, G, P = lam_re.shape
    gc = b_re.shape[-1]
    gpb = LANES // gc
    nblk = G // gpb
    dt = jnp.exp(log_dt)[..., None]
    mag = jnp.exp(lam_re * dt)
    ar = mag * jnp.cos(lam_im * dt)
    ai = mag * jnp.sin(lam_im * dt)
    nr = ar - 1.0
    den = lam_re * lam_re + lam_im * lam_im
    fr = (nr * lam_re + ai * lam_im) / den
    fi = (ai * lam_re - nr * lam_im) / den
    bbar_re = fr[..., None] * b_re - fi[..., None] * b_im
    bbar_im = fr[..., None] * b_im + fi[..., None] * b_re
    rg = lax.broadcasted_iota(jnp.int32, (gpb * gc, gpb * P), 0) // gc
    cg = lax.broadcasted_iota(jnp.int32, (gpb * gc, gpb * P), 1) // P
    diag = rg == cg

    def bmat(bb):
        t = jnp.swapaxes(bb.reshape(D, nblk, gpb * P, gc), 2, 3)
        return jnp.where(diag, jnp.tile(t, (1, 1, gpb, 1)), 0.0)

    def cmat(cc):
        t = jnp.swapaxes(cc.reshape(D, nblk, gpb, gc, P), 3, 4).reshape(D, nblk, gpb * P, gc)
        return jnp.where(diag.T, jnp.tile(t, (1, 1, 1, gpb)), 0.0)

    b_blk = jnp.concatenate([bmat(bbar_re), bmat(bbar_im)], axis=3).astype(BF16)
    c_blk = jnp.concatenate([cmat(c_re), -cmat(c_im)], axis=2).astype(BF16)
    return (b_blk, c_blk, ar.reshape(D, nblk, 1, gpb * P), ai.reshape(D, nblk, 1, gpb * P))


def _s5_prompt_kernel(u_ref, bblk_ref, cblk_ref, ar_ref, ai_ref, d_ref,
                      g5_ref, g5b_ref, xre_ref, xim_ref, up, xs0, xs1, ys):
    L = u_ref.shape[0]
    ns = ar_ref.shape[-1]
    nseg = S5_SEGMENTS
    seg = L // nseg
    R = xs0.shape[0] // max(xs0.shape[0] // (S5_BLOCK_STEPS * nseg), 1)
    steps = R // nseg
    n_pair = xs0.shape[0] // R
    n_blk = 2 * n_pair
    assert n_blk * R == L

    def gather(t, carry):
        up[pl.ds(pl.multiple_of(t * nseg, nseg), nseg), :] = u_ref[pl.ds(t, nseg, stride=seg), :]
        return carry

    lax.fori_loop(0, seg, gather, 0, unroll=8)
    bmat = bblk_ref[...]
    cmat = cblk_ref[...]
    ar = jnp.broadcast_to(ar_ref[...], (nseg, ns))
    ai = jnp.broadcast_to(ai_ref[...], (nseg, ns))

    def block(row0):
        return pl.ds(pl.multiple_of(row0, R), R)

    def project(buf, blk, row0):
        u = up[block(blk * R), :]
        buf[block(row0), :] = jnp.dot(u.astype(BF16), bmat, preferred_element_type=F32)

    def scan(buf, row0, x, store):
        for t in range(steps):
            rows = pl.ds(pl.multiple_of(row0 + t * nseg, nseg), nseg)
            r = buf[rows, :]
            xr, xi = x
            x = (ar * xr - ai * xi + r[:, :ns], ar * xi + ai * xr + r[:, ns:])
            if store:
                buf[rows, :] = jnp.concatenate(x, axis=1)
        return x

    def emit(buf, row0, blk):
        y = jnp.dot(buf[block(row0), :].astype(BF16), cmat, preferred_element_type=F32)
        ys[block(blk * R), :] = _gelu_tanh(y + d_ref[...] * up[block(blk * R), :])

    project(xs0, 0, 0)

    def pass1(p, e):
        row0 = p * R
        project(xs1, 2 * p + 1, row0)
        e = scan(xs0, row0, e, False)
        nxt = jnp.minimum(p + 1, n_pair - 1)
        project(xs0, 2 * nxt, nxt * R)
        return scan(xs1, row0, e, False)

    zero = jnp.zeros((nseg, ns), F32)
    er, ei = lax.fori_loop(0, n_pair, pass1, (zero, zero))
    pr, pi = ar_ref[...], ai_ref[...]
    for _ in range(seg.bit_length() - 1):
        pr, pi = pr * pr - pi * pi, 2.0 * pr * pi
    assert seg == 1 << (seg.bit_length() - 1)
    cr = [jnp.zeros((1, ns), F32)]
    ci = [jnp.zeros((1, ns), F32)]
    for s in range(nseg - 1):
        cr.append(er[s:s + 1] + pr * cr[s] - pi * ci[s])
        ci.append(ei[s:s + 1] + pr * ci[s] + pi * cr[s])
    x = (jnp.concatenate(cr, axis=0), jnp.concatenate(ci, axis=0))

    x = scan(xs0, 0, x, True)

    def pass2(p, x):
        row0 = p * R
        x = scan(xs1, row0, x, True)
        emit(xs0, row0, 2 * p)
        x = scan(xs0, row0 + R, x, True)
        emit(xs1, row0, 2 * p + 1)
        return x

    x = lax.fori_loop(0, n_pair - 1, pass2, x)
    last = (n_pair - 1) * R
    x = scan(xs1, last, x, True)
    emit(xs0, last, n_blk - 2)
    emit(xs1, last, n_blk - 1)
    xre_ref[...] = x[0][nseg - 1:nseg]
    xim_ref[...] = x[1][nseg - 1:nseg]
    for s in range(nseg):
        g5 = ys[pl.ds(s, seg, stride=nseg), :]
        g5_ref[s * seg:(s + 1) * seg, :] = g5
        g5b_ref[s * seg:(s + 1) * seg, :] = g5.astype(BF16)


def _s5_prompt(u_all, tables, d_skip, layer, batch, seq, n_ch):
    b_blk, c_blk, ar, ai = tables
    nblk, _, ns2 = b_blk.shape[1:]
    ns = ns2 // 2
    L = seq
    nbytes = (2 * L * LANES * 4 + 2 * L * LANES * 6 + 3 * L * ns2 * 4 + L * ns2 * 2
              + 8 * LANES * ns2 * 2)
    blkp = lambda r, c: pl.BlockSpec((None, None, r, c), lambda b, j: (layer, j, 0, 0))
    return pl.pallas_call(
        _s5_prompt_kernel,
        out_shape=(jax.ShapeDtypeStruct((batch * L, n_ch), F32),
                   jax.ShapeDtypeStruct((batch * L, n_ch), BF16),
                   jax.ShapeDtypeStruct((batch, 1, nblk * ns), F32),
                   jax.ShapeDtypeStruct((batch, 1, nblk * ns), F32)),
        grid=(batch, nblk),
        in_specs=[pl.BlockSpec((L, LANES), lambda b, j: (b, j)),
                  blkp(LANES, ns2), blkp(ns2, LANES), blkp(1, ns), blkp(1, ns),
                  pl.BlockSpec((None, 1, LANES), lambda b, j: (layer, 0, j))],
        out_specs=(pl.BlockSpec((L, LANES), lambda b, j: (b, j)),
                   pl.BlockSpec((L, LANES), lambda b, j: (b, j)),
                   pl.BlockSpec((None, 1, ns), lambda b, j: (b, 0, j)),
                   pl.BlockSpec((None, 1, ns), lambda b, j: (b, 0, j))),
        scratch_shapes=[pltpu.VMEM((L, LANES), F32),
                        pltpu.VMEM((L // 2, ns2), F32),
                        pltpu.VMEM((L // 2, ns2), F32),
                        pltpu.VMEM((L, LANES), F32)],
        compiler_params=pltpu.CompilerParams(
            dimension_semantics=("parallel", "parallel"),
            vmem_limit_bytes=_vmem_limit(nbytes)),
        name="s5_prompt",
    )(u_all, b_blk, c_blk, ar, ai, d_skip)


def _s5_sample_kernel(u_ref, bblk_ref, cblk_ref, ar_ref, ai_ref, d_ref, x0r_ref, x0i_ref,
                      g5_ref, g5b_ref, xre_ref, xim_ref):
    ns = ar_ref.shape[-1]
    u = u_ref[...]
    bu = jnp.dot(u.astype(BF16), bblk_ref[...], preferred_element_type=F32)
    ar, ai = ar_ref[...], ai_ref[...]
    x0r, x0i = x0r_ref[...], x0i_ref[...]
    xr = ar * x0r - ai * x0i + bu[:, :ns]
    xi = ar * x0i + ai * x0r + bu[:, ns:]
    xre_ref[...] = xr
    xim_ref[...] = xi
    x = jnp.concatenate([xr, xi], axis=1)
    y = jnp.dot(x.astype(BF16), cblk_ref[...], preferred_element_type=F32) + d_ref[...] * u
    g5 = _gelu_tanh(y)
    g5_ref[...] = g5
    g5b_ref[...] = g5.astype(BF16)


def _s5_sample(u_all, row0, tables, d_skip, x0_re, x0_im, layer, n_ch):
    b_blk, c_blk, ar, ai = tables
    nblk, _, ns2 = b_blk.shape[1:]
    ns = ns2 // 2
    nb = x0_re.shape[1]
    assert row0 % nb == 0
    r0 = row0 // nb
    blkp = lambda r, c: pl.BlockSpec((None, None, r, c), lambda j: (layer, j, 0, 0))
    nbytes = 16 * nb * ns2 * 4 + 8 * LANES * ns2 * 2
    return pl.pallas_call(
        _s5_sample_kernel,
        out_shape=(jax.ShapeDtypeStruct((nb, n_ch), F32),
                   jax.ShapeDtypeStruct((nb, n_ch), BF16),
                   jax.ShapeDtypeStruct((nb, nblk * ns), F32),
                   jax.ShapeDtypeStruct((nb, nblk * ns), F32)),
        grid=(nblk,),
        in_specs=[pl.BlockSpec((nb, LANES), lambda j: (r0, j)),
                  blkp(LANES, ns2), blkp(ns2, LANES), blkp(1, ns), blkp(1, ns),
                  pl.BlockSpec((None, 1, LANES), lambda j: (layer, 0, j)),
                  pl.BlockSpec((None, nb, ns), lambda j: (layer, 0, j)),
                  pl.BlockSpec((None, nb, ns), lambda j: (layer, 0, j))],
        out_specs=(pl.BlockSpec((nb, LANES), lambda j: (0, j)),
                   pl.BlockSpec((nb, LANES), lambda j: (0, j)),
                   pl.BlockSpec((nb, ns), lambda j: (0, j)),
                   pl.BlockSpec((nb, ns), lambda j: (0, j))),
        compiler_params=pltpu.CompilerParams(
            dimension_semantics=("parallel",), vmem_limit_bytes=_vmem_limit(nbytes)),
        name="s5_sample",
    )(u_all, b_blk, c_blk, ar, ai, d_skip, x0_re, x0_im)


def kernel(x_prompt, x_sample, state_dn_conv, state_dn_ssm, state_s5_re, state_s5_im, norm1, w_in, dn_conv_w, dn_a_log, dn_dt_bias, dn_norm_w, w_br_dn, s5_lam_re, s5_lam_im, s5_log_dt, s5_b_re, s5_b_im, s5_c_re, s5_c_im, s5_d, w_glu, w_br_s5, w_out, norm2, w_ffn_gate, w_ffn_up, w_ffn_down, norm_f):
    batch, seq, d_model = x_prompt.shape
    nb, dec_seq, _ = x_sample.shape
    assert dec_seq == 1
    depth, _, n_heads, dk, dv = state_dn_ssm.shape
    assert dk == LANES and dv == LANES and seq % GDN_CHUNK == 0 and seq % S5_SEGMENTS == 0
    qk_dim = n_heads * dk
    conv_ch = dn_conv_w.shape[2]
    assert conv_ch == 3 * qk_dim
    n_ch = s5_d.shape[1]
    n_groups, n_state = s5_lam_re.shape[1:]
    ffn = w_ffn_gate.shape[2]
    mp = batch * seq
    m = mp + nb
    z_end = 4 * qk_dim
    rest0 = z_end + 2 * n_heads
    assert w_in.shape[2] == rest0 + n_ch + 2 * d_model and 2 * n_heads <= LANES

    x = (x_prompt.reshape(mp, d_model), x_sample.reshape(nb, d_model))

    w_in_t = jnp.swapaxes(w_in, 1, 2)
    w_bg = jnp.pad(w_in_t[:, z_end:rest0, :], ((0, 0), (0, LANES - 2 * n_heads), (0, 0)))
    w_s5 = w_in_t[:, rest0:rest0 + n_ch, :]
    gate_tn = 512
    gate0 = rest0 + n_ch
    gate_base = gate0 - gate0 % gate_tn
    gate_shift = gate0 - gate_base
    gate_cols = -(-(gate_shift + 2 * d_model) // gate_tn) * gate_tn
    assert gate_shift < LANES and gate_base + gate_cols - gate_tn < w_in.shape[2]
    conv_hist = jnp.swapaxes(state_dn_conv, 1, 2)
    pad_heads = lambda a: jnp.pad(a, ((0, 0), (n_heads, LANES - 2 * n_heads)))[:, None, :]
    alog_pad = pad_heads(dn_a_log)
    dtb_pad = pad_heads(dn_dt_bias)
    x0_re = state_s5_re.reshape(depth, nb, n_groups * n_state)
    x0_im = state_s5_im.reshape(depth, nb, n_groups * n_state)

    tm = _pick_tile(m, 1664, 64)
    norm_w3 = dn_norm_w[:, None, :]
    d_skip3 = s5_d[:, None, :]
    tables = _s5_tables(s5_lam_re, s5_lam_im, s5_log_dt, s5_b_re, s5_b_im, s5_c_re, s5_c_im)
    outs = {k: [] for k in ("p_conv", "p_ssm", "p_re", "p_im", "s_conv", "s_re", "s_im")}
    s_ssm = None
    for l in range(depth):
        h = _rmsnorm(x, norm1[l][None, :], BF16)
        qkvz = _fused_matmul([h], [(0, w_in_t, l, 0, True)], [], _ep_identity, z_end, F32,
                             tm=tm, tn=512, name="in_qkvz")
        bg = _fused_matmul([h], [(0, w_bg, l, 0, True)], [], _ep_beta_decay(n_heads), LANES, F32,
                           tm=tm, tn=LANES, name="in_bg",
                           col_params=[(alog_pad, l), (dtb_pad, l)])
        s5u = _fused_matmul([h], [(0, w_s5, l, 0, True)], [], _ep_identity, n_ch, F32,
                            tm=tm, tn=512, name="in_s5")
        gates = _fused_matmul([h], [(0, w_in_t, l, gate_base, True)], [], _ep_identity, gate_cols, F32,
                              tm=tm, tn=gate_tn, name="in_gates",
                              valid_cols=gate_shift + 2 * d_model)

        o_p, ssm_p, *pc = _gdn_prompt(qkvz, bg, dn_conv_w, norm_w3, l, batch, seq, n_heads, dk)
        o_s, s_ssm, *sc = _gdn_sample(qkvz, bg, mp, conv_hist, dn_conv_w, norm_w3,
                                      state_dn_ssm, l, n_heads, dk, state_out=s_ssm)
        outs["p_conv"].append(jnp.concatenate(pc, axis=-1))
        outs["s_conv"].append(jnp.concatenate(sc, axis=-1))
        outs["p_ssm"].append(ssm_p)

        g5_p, g5b_p, re_p, im_p = _s5_prompt(s5u, tables, d_skip3, l, batch, seq, n_ch)
        g5_s, g5b_s, re_s, im_s = _s5_sample(s5u, mp, tables, d_skip3, x0_re, x0_im, l, n_ch)
        outs["p_re"].append(re_p.reshape(batch, n_groups, n_state))
        outs["p_im"].append(im_p.reshape(batch, n_groups, n_state))
        outs["s_re"].append(re_s.reshape(nb, n_groups, n_state))
        outs["s_im"].append(im_s.reshape(nb, n_groups, n_state))
        g5g = _fused_matmul([(g5b_p, g5b_s)], [(0, w_glu, l, 0)], [((g5_p, g5_s), 0)],
                            _ep_glu_self, n_ch, BF16, tm=tm, tn=512, name="s5_glu")

        merged = _fused_matmul([(o_p, o_s), g5g], [(0, w_br_dn, l, 0), (1, w_br_s5, l, 0)],
                               [(gates, gate_shift), (gates, gate_shift + d_model)],
                               _ep_gated_merge,
                               d_model, BF16, tm=tm, tn=512, name="branch_merge")
        x = _fused_matmul([merged], [(0, w_out, l, 0)], [(x, 0)], _ep_residual, d_model, F32,
                          tm=tm, tn=512, name="out_proj")

        hmid = _fused_matmul([x], [(0, w_ffn_gate, l, 0), (0, w_ffn_up, l, 0)], [], _ep_swiglu,
                             ffn, BF16, tm=tm, tn=256, name="ffn_up", norm_w=norm2[l][None, :])
        x = _matmul_residual_wstat(hmid, w_ffn_down, l, x, tm=_pick_tile(m, 832, 16), tn=512,
                                   name="ffn_down")

    y_p, y_s = _rmsnorm(x, norm_f[None, :], F32, split_rows=(mp, nb))
    st = lambda k: jnp.stack(outs[k])
    return (y_p.reshape(batch, seq, d_model), y_s.reshape(nb, 1, d_model),
            st("p_conv"), st("p_ssm"), st("p_re"), st("p_im"),
            jnp.swapaxes(st("s_conv"), 1, 2), s_ssm, st("s_re"), st("s_im"))
```

```python
import functools
import math

import jax
import jax.numpy as jnp
from jax import lax
from jax.experimental import pallas as pl
from jax.experimental.pallas import tpu as pltpu

F32 = jnp.float32
BF16 = jnp.bfloat16

NORM_EPS = 1e-6
L2_EPS = 1e-6
LANES = 128
SUBLANES = 8
VMEM_CAP_BYTES = 56 * 1024 * 1024
GDN_CHUNK = 128
S5_SEGMENTS = SUBLANES
GDN_HEADS_PER_STEP = 2
GDN_CHUNKS_PER_GROUP = 16
S5_BLOCK_STEPS = 32


def _vmem_limit(nbytes):
    return int(min(VMEM_CAP_BYTES, nbytes * 5 // 4 + (4 << 20)))


def _pick_tile(n, target, mult):
    best = None
    for t in range(mult, min(n, target) + 1, mult):
        if n % t == 0:
            best = t
    return best if best is not None else n


def _sigmoid(x):
    return 1.0 / (1.0 + jnp.exp(-x))


def _silu(x):
    return x * _sigmoid(x)


def _softplus(x):
    return jnp.maximum(x, 0.0) + jnp.log1p(jnp.exp(-jnp.abs(x)))


def _gelu_tanh(x):
    c = math.sqrt(2.0 / math.pi)
    return 0.5 * x * (1.0 + jnp.tanh(c * (x + 0.044715 * (x * x * x))))


def _bdot(a, b):
    return jnp.dot(a.astype(BF16), b.astype(BF16), preferred_element_type=F32)


def _bdot_nt(a, b):
    return lax.dot_general(a.astype(BF16), b.astype(BF16), (((1,), (1,)), ((), ())),
                           preferred_element_type=F32)


def _rows_of(op):
    return op[0].shape[0] + op[1].shape[0] if isinstance(op, tuple) else op.shape[0]


def _row_specs(op, tm, ncols, index_map):
    if not isinstance(op, tuple):
        return [pl.BlockSpec((tm, ncols), index_map)], [op], None
    p, s = op
    tail = p.shape[0] % tm
    assert tail + s.shape[0] == tm and tail % 16 == 0, (p.shape, s.shape, tm)

    def s_map(*idx):
        return (0,) + tuple(index_map(*idx)[1:])

    return ([pl.BlockSpec((tm, ncols), index_map), pl.BlockSpec((s.shape[0], ncols), s_map)],
            [p, s], tail)


def _load_rows(refs, tail, last):
    if tail is None or not last:
        return refs[0][...]
    return jnp.concatenate([refs[0][:tail, :], refs[1][...]], axis=0)


def _lane_window(parts, shift, width):
    x = parts[0] if len(parts) == 1 else jnp.concatenate(parts, axis=1)
    if shift:
        x = pltpu.roll(x, x.shape[1] - shift, 1)
    return x[:, :width]


def _on_row_tiles(i, n_tiles, any_split, body):
    if not any_split:
        body(False)
        return
    if n_tiles > 1:
        pl.when(i < n_tiles - 1)(lambda: body(False))
    pl.when(i == n_tiles - 1)(lambda: body(True))


def _rmsnorm_kernel(*refs, tail, n_tiles, out_tail):
    n_out = 1 if out_tail is None else 2
    x_refs, w_ref, o_refs = refs[:-1 - n_out], refs[-1 - n_out], refs[-n_out:]

    def body(last):
        x = _load_rows(x_refs, tail, last)
        y = x * lax.rsqrt(jnp.mean(x * x, axis=-1, keepdims=True) + NORM_EPS)
        y = (y * w_ref[...]).astype(o_refs[0].dtype)
        if out_tail is None:
            o_refs[0][...] = y
        elif not last:
            o_refs[0][...] = y
        else:
            o_refs[0][:out_tail, :] = y[:out_tail]
            o_refs[1][...] = y[out_tail:]

    _on_row_tiles(pl.program_id(0), n_tiles, tail is not None or out_tail is not None, body)


def _rmsnorm(x, w_row, out_dtype, split_rows=None):
    m = _rows_of(x)
    d = w_row.shape[1]
    tr = _pick_tile(m, 832, 64)
    specs, arrs, tail = _row_specs(x, tr, d, lambda i: (i, 0))
    nbytes = 2 * tr * d * 4 + 2 * tr * d * jnp.dtype(out_dtype).itemsize + 3 * tr * d * 4
    if split_rows is None:
        out_shape = jax.ShapeDtypeStruct((m, d), out_dtype)
        out_specs = pl.BlockSpec((tr, d), lambda i: (i, 0))
        out_tail = None
    else:
        mp, nb = split_rows
        out_tail = mp % tr
        assert mp + nb == m and out_tail + nb == tr
        out_shape = (jax.ShapeDtypeStruct((mp, d), out_dtype),
                     jax.ShapeDtypeStruct((nb, d), out_dtype))
        out_specs = (pl.BlockSpec((tr, d), lambda i: (i, 0)),
                     pl.BlockSpec((nb, d), lambda i: (0, 0)))
    return pl.pallas_call(
        functools.partial(_rmsnorm_kernel, tail=tail, n_tiles=m // tr, out_tail=out_tail),
        out_shape=out_shape,
        grid=(m // tr,),
        in_specs=specs + [pl.BlockSpec((1, d), lambda i: (0, 0))],
        out_specs=out_specs,
        compiler_params=pltpu.CompilerParams(
            dimension_semantics=("arbitrary",), vmem_limit_bytes=_vmem_limit(nbytes)),
        name="rmsnorm",
    )(*arrs, w_row)


def _mm_kernel(*refs, a_idx, w_transposed, a_groups, e_groups, n_tiles, epilogue, normed,
               zero_tail_from):
    pos = 0
    a_refs = []
    for n, _ in a_groups:
        a_refs.append(refs[pos:pos + n])
        pos += n
    w_refs = refs[pos:pos + len(a_idx)]
    pos += len(a_idx)
    e_refs = []
    for n, _, _ in e_groups:
        e_refs.append(refs[pos:pos + n])
        pos += n
    if normed:
        nw_ref, o_ref, h_scr = refs[pos:pos + 3]
    else:
        o_ref = refs[pos]
    any_split = any(g[1] is not None for g in a_groups + e_groups)

    def load_extra(r, tail, shift, last):
        if shift:
            return _lane_window([r[0][...], r[1][...]], shift, o_ref.shape[1])
        return _load_rows(r, tail, last)

    def body(last):
        if normed:
            @pl.when(pl.program_id(1) == 0)
            def _():
                x = _load_rows(a_refs[0], a_groups[0][1], last)
                y = x * lax.rsqrt(jnp.mean(x * x, axis=-1, keepdims=True) + NORM_EPS)
                h_scr[...] = (y * nw_ref[...]).astype(BF16)

            a_vals = [h_scr[...]]
        else:
            a_vals = [_load_rows(r, t, last) for r, (_, t) in zip(a_refs, a_groups)]
        parts = [(_bdot_nt if wt else _bdot)(a_vals[ai], w[...])
                 for ai, wt, w in zip(a_idx, w_transposed, w_refs)]
        e_vals = [load_extra(r, t, sh, last) for r, (_, t, sh) in zip(e_refs, e_groups)]
        o_ref[...] = epilogue(parts, e_vals).astype(o_ref.dtype)
        if zero_tail_from is not None:
            @pl.when(pl.program_id(1) == pl.num_programs(1) - 1)
            def _():
                o_ref[:, zero_tail_from:] = jnp.zeros(
                    (o_ref.shape[0], o_ref.shape[1] - zero_tail_from), o_ref.dtype)

    _on_row_tiles(pl.program_id(0), n_tiles, any_split, body)


def _fused_matmul(a_list, w_list, extras, epilogue, n_out, out_dtype, *, tm, tn, name,
                  col_params=(), norm_w=None, valid_cols=None):
    m = _rows_of(a_list[0])
    assert m % tm == 0 and n_out % tn == 0
    in_specs, args, a_groups, e_groups = [], [], [], []
    kdims = []
    for a in a_list:
        kd = (a[0] if isinstance(a, tuple) else a).shape[1]
        specs, arrs, tail = _row_specs(a, tm, kd, lambda i, j: (i, 0))
        assert _rows_of(a) == m
        in_specs += specs
        args += arrs
        a_groups.append((len(arrs), tail))
        kdims.append(kd)
    w_list = [tuple(e) + (False,) * (5 - len(e)) for e in w_list]
    for ai, w, layer, col0, transposed in w_list:
        assert col0 % tn == 0 and w.shape[2 if transposed else 1] == kdims[ai]
        if transposed:
            spec = pl.BlockSpec((None, tn, w.shape[2]),
                                lambda i, j, layer=layer, off=col0 // tn: (layer, j + off, 0))
        else:
            spec = pl.BlockSpec((None, w.shape[1], tn),
                                lambda i, j, layer=layer, off=col0 // tn: (layer, 0, j + off))
        in_specs.append(spec)
        args.append(w)
    for e, col0 in extras:
        shift = col0 % tn
        assert shift < LANES and _rows_of(e) == m
        specs, arrs, tail = _row_specs(e, tm, tn, lambda i, j, off=col0 // tn: (i, j + off))
        if shift:
            assert not isinstance(e, tuple) and tn % LANES == 0
            specs.append(pl.BlockSpec(
                (tm, LANES), lambda i, j, off=col0 // tn: (i, (j + off + 1) * (tn // LANES))))
            arrs.append(e)
        in_specs += specs
        args += arrs
        e_groups.append((len(arrs), tail, shift))
    for p, layer in col_params:
        in_specs.append(pl.BlockSpec((None, 1, tn), lambda i, j, layer=layer: (layer, 0, j)))
        args.append(p)
        e_groups.append((1, None, 0))
    normed = norm_w is not None
    scratch = []
    if normed:
        assert len(a_list) == 1
        in_specs.append(pl.BlockSpec((1, kdims[0]), lambda i, j: (0, 0)))
        args.append(norm_w)
        scratch.append(pltpu.VMEM((tm, kdims[0]), BF16))
    osz = jnp.dtype(out_dtype).itemsize
    nbytes = (sum(3 * tm * kd * (6 if normed else 2) for kd in kdims)
              + sum(kdims[ai] * tn * (2 * w.dtype.itemsize + 2) for ai, w, _, _, _ in w_list)
              + sum(3 * tm * tn * 4 for _ in extras)
              + 2 * tm * tn * osz + (2 + len(w_list)) * tm * tn * 4)
    kern = functools.partial(_mm_kernel, a_idx=tuple(e[0] for e in w_list),
                             w_transposed=tuple(e[4] for e in w_list), a_groups=tuple(a_groups), e_groups=tuple(e_groups),
                             n_tiles=m // tm, epilogue=epilogue, normed=normed,
                             zero_tail_from=None if valid_cols is None else valid_cols % tn)
    assert valid_cols is None or n_out - tn < valid_cols < n_out
    return pl.pallas_call(
        kern,
        out_shape=jax.ShapeDtypeStruct((m, n_out), out_dtype),
        grid=(m // tm, n_out // tn),
        in_specs=in_specs,
        out_specs=pl.BlockSpec((tm, tn), lambda i, j: (i, j)),
        scratch_shapes=scratch,
        compiler_params=pltpu.CompilerParams(
            dimension_semantics=("parallel", "arbitrary" if normed else "parallel"),
            vmem_limit_bytes=_vmem_limit(nbytes)),
        name=name,
    )(*args)


def _mm_residual_wstat_kernel(a_ref, w_ref, x_ref, o_ref, wb):
    @pl.when(pl.program_id(1) == 0)
    def _():
        wb[...] = w_ref[...].astype(BF16)

    o_ref[...] = x_ref[...] + jnp.dot(a_ref[...], wb[...], preferred_element_type=F32)


def _matmul_residual_wstat(a, w, layer, x, *, tm, tn, name):
    m, kdim = a.shape
    n_out = w.shape[2]
    assert m % tm == 0 and n_out % tn == 0 and w.shape[1] == kdim
    nbytes = 2 * tm * kdim * 2 + kdim * tn * (w.dtype.itemsize + 2) + 6 * tm * tn * 4
    return pl.pallas_call(
        _mm_residual_wstat_kernel,
        out_shape=jax.ShapeDtypeStruct((m, n_out), F32),
        grid=(n_out // tn, m // tm),
        in_specs=[pl.BlockSpec((tm, kdim), lambda j, i: (i, 0)),
                  pl.BlockSpec((None, kdim, tn), lambda j, i: (layer, 0, j),
                               pipeline_mode=pl.Buffered(1)),
                  pl.BlockSpec((tm, tn), lambda j, i: (i, j))],
        out_specs=pl.BlockSpec((tm, tn), lambda j, i: (i, j)),
        scratch_shapes=[pltpu.VMEM((kdim, tn), BF16)],
        compiler_params=pltpu.CompilerParams(
            dimension_semantics=("parallel", "arbitrary"),
            vmem_limit_bytes=_vmem_limit(nbytes)),
        name=name,
    )(a, w, x)


def _ep_identity(accs, extras):
    return accs[0]


def _ep_residual(accs, extras):
    return extras[0] + accs[0]


def _ep_swiglu(accs, extras):
    return _silu(accs[0]) * accs[1]


def _ep_glu_self(accs, extras):
    g5 = extras[0]
    return g5 * _sigmoid(accs[0])


def _ep_gated_merge(accs, extras):
    return _sigmoid(extras[0]) * accs[0] + _sigmoid(extras[1]) * accs[1]


def _ep_beta_decay(n_heads):
    def ep(accs, extras):
        acc = accs[0]
        alog_row, dtb_row = extras
        lane = lax.broadcasted_iota(jnp.int32, acc.shape, 1)
        return jnp.where(lane < n_heads, _sigmoid(acc),
                         -jnp.exp(alog_row) * _softplus(acc + dtb_row))
    return ep


def _head_columns(bg, head, n_heads):
    lane = lax.broadcasted_iota(jnp.int32, bg.shape, 1)
    beta = jnp.sum(jnp.where(lane == head, bg, 0.0), axis=-1, keepdims=True)
    g = jnp.sum(jnp.where(lane == head + n_heads, bg, 0.0), axis=-1, keepdims=True)
    return beta, g


def _l2norm_rows(x):
    return x * lax.rsqrt(jnp.sum(x * x, axis=-1, keepdims=True) + L2_EPS)


def _sum_rows(x):
    acc = x[0:SUBLANES]
    for i in range(1, x.shape[0] // SUBLANES):
        acc = acc + x[i * SUBLANES:(i + 1) * SUBLANES]
    shift = SUBLANES // 2
    while shift:
        acc = acc + pltpu.roll(acc, shift, 0)
        shift //= 2
    return acc[0:1]


def _gated_out_norm(o, z, nw_row):
    y = o * lax.rsqrt(jnp.mean(o * o, axis=-1, keepdims=True) + NORM_EPS)
    return y * nw_row * _silu(z)


def _gdn_prompt_kernel(q_ref, k_ref, v_ref, z_ref, bg_ref, cwq_ref, cwk_ref, cwv_ref, nw_ref,
                       o_ref, s_ref, pcq_ref, pck_ref, pcv_ref,
                       qn, kn, vn, gb, bb, us, ws, qks, qds, kdt, gl, osc, *, n_heads):
    hb, L, dk = qn.shape
    n_hist = pcq_ref.shape[0]
    for x_ref, pc_ref in ((q_ref, pcq_ref), (k_ref, pck_ref), (v_ref, pcv_ref)):
        pc_ref[...] = x_ref[L - n_hist:L, :]
    C = GDN_CHUNK
    n_chunks = L // C
    H8 = range(hb)

    row8 = lax.broadcasted_iota(jnp.int32, (SUBLANES, dk), 0)

    def conv_silu(x_ref, cw_ref, cols):
        cw = cw_ref[:, cols]
        n_taps = cw.shape[0]
        assert n_taps - 1 <= SUBLANES
        tap = lambda j: cw[n_taps - 1 - j:n_taps - j, :]
        head8 = x_ref[0:SUBLANES, cols]
        lo = head8 * tap(0)
        hi = x_ref[SUBLANES:L, cols] * tap(0)
        for j in range(1, n_taps):
            lo = lo + jnp.where(row8 >= j, pltpu.roll(head8, j, 0), 0.0) * tap(j)
            hi = hi + x_ref[pl.ds(SUBLANES - j, L - SUBLANES), cols] * tap(j)
        return _silu(jnp.concatenate([lo, hi], axis=0))

    bg = bg_ref[...]
    for hh in H8:
        cols = slice(hh * dk, (hh + 1) * dk)
        qn[hh] = _l2norm_rows(conv_silu(q_ref, cwq_ref, cols)) * (dk ** -0.5)
        kn[hh] = _l2norm_rows(conv_silu(k_ref, cwk_ref, cols))
        vn[hh] = conv_silu(v_ref, cwv_ref, cols)
        beta, g = _head_columns(bg, pl.program_id(1) * hb + hh, n_heads)
        bb[hh] = jnp.broadcast_to(beta, (L, dk))
        gb[hh] = jnp.broadcast_to(g, (L, dk))

    ri = lax.broadcasted_iota(jnp.int32, (C, C), 0)
    ci = lax.broadcasted_iota(jnp.int32, (C, C), 1)
    causal = ri >= ci
    strict = ri > ci
    tri_incl = jnp.where(causal, 1.0, 0.0).astype(F32)
    eye = jnp.where(ri == ci, 1.0, 0.0).astype(F32)
    level_masks = []
    n = 1
    while n < C:
        sh = n.bit_length() - 1
        same_2n = (ri >> (sh + 1)) == (ci >> (sh + 1))
        diff_n = (ri >> sh) != (ci >> sh)
        level_masks.append(jnp.where(same_2n & diff_n & strict, 1.0, 0.0).astype(F32))
        n *= 2

    group = math.gcd(n_chunks, GDN_CHUNKS_PER_GROUP)

    def intra_group(hh, i):
        G = range(group)
        rows = [pl.ds(pl.multiple_of((i * group + j) * C, C), C) for j in G]
        k = [kn[hh, r, :] for r in rows]
        bet = [bb[hh, r, :] for r in rows]
        gcb = [jnp.dot(tri_incl, gb[hh, r, :], precision=lax.Precision.HIGHEST,
                       preferred_element_type=F32) for r in rows]
        gamma = [jnp.where(causal, jnp.exp(jnp.minimum(g - g.T, 0.0)), 0.0) for g in gcb]
        kb = [k[j] * bet[j] for j in G]
        a_mat = [jnp.where(strict, _bdot_nt(kb[j], k[j]) * gamma[j], 0.0) for j in G]
        q = [qn[hh, r, :] for r in rows]
        for j in G:
            qks[hh, rows[j], :] = _bdot_nt(q[j], k[j]) * gamma[j]
        t = [eye - a * level_masks[0] for a in a_mat]
        for m in level_masks[1:]:
            x = [_bdot(a_mat[j] * m, t[j]) for j in G]
            t = [t[j] - _bdot(t[j], x[j]) for j in G]
        eg = [jnp.exp(g) for g in gcb]
        for j in G:
            us[hh, rows[j], :] = _bdot(t[j], vn[hh, rows[j], :] * bet[j])
        for j in G:
            ws[hh, rows[j], :] = _bdot(t[j], kb[j] * eg[j])
        for j in G:
            qds[hh, rows[j], :] = q[j] * eg[j]
            g_last = gcb[j][C - 1:C, :]
            kdt[hh, rows[j], :] = (k[j] * jnp.exp(g_last - gcb[j])).T
            gl[hh, pl.ds(pl.multiple_of((i * group + j) * SUBLANES, SUBLANES), SUBLANES), :] = (
                jnp.broadcast_to(jnp.exp(g_last), (SUBLANES, dk)))

    for hh in H8:
        if n_chunks == group:
            intra_group(hh, 0)
        else:
            def body(i, carry, hh=hh):
                intra_group(hh, i)
                return carry
            lax.fori_loop(0, n_chunks // group, body, 0)

    def inter(c, states):
        rows = pl.ds(pl.multiple_of(c * C, C), C)
        ws_s = [_bdot(ws[hh, rows, :], states[hh]) for hh in H8]
        qd_s = [_bdot(qds[hh, rows, :], states[hh]) for hh in H8]
        v_new = [us[hh, rows, :] - ws_s[hh] for hh in H8]
        for hh in H8:
            osc[hh, rows, :] = qd_s[hh] + _bdot(qks[hh, rows, :], v_new[hh])
        decay = [gl[hh, pl.ds(pl.multiple_of(c * SUBLANES, SUBLANES), 1), :] for hh in H8]
        return tuple(states[hh] * decay[hh] + _bdot(kdt[hh, rows, :], v_new[hh]) for hh in H8)

    s_fin = lax.fori_loop(0, n_chunks, inter, tuple(jnp.zeros((dk, dk), F32) for _ in H8))
    for hh in H8:
        cols = slice(hh * dk, (hh + 1) * dk)
        s_ref[hh] = s_fin[hh]
        o_ref[:, cols] = _gated_out_norm(osc[hh], z_ref[:, cols], nw_ref[...]).astype(o_ref.dtype)


def _gdn_prompt(qkvz, bg, conv_w, norm_w, layer, batch, seq, n_heads, dk):
    H = n_heads
    L = seq
    hb = GDN_HEADS_PER_STEP if H % GDN_HEADS_PER_STEP == 0 else 1
    hg = H // hb
    n_hist = conv_w.shape[1] - 1

    def col(sec):
        return pl.BlockSpec((L, hb * dk), lambda b, h, sec=sec: (b, h + sec * hg))

    def cw(sec):
        return pl.BlockSpec((None, conv_w.shape[1], hb * dk),
                            lambda b, h, sec=sec: (layer, 0, h + sec * hg))

    scr = lambda: pltpu.VMEM((hb, L, dk), F32)
    nbytes = hb * (2 * 4 * L * dk * 4 + 11 * L * dk * 4 + 2 * L * dk * 2) + 2 * L * LANES * 4 \
        + 60 * GDN_CHUNK * GDN_CHUNK * 4
    return pl.pallas_call(
        functools.partial(_gdn_prompt_kernel, n_heads=H),
        out_shape=(jax.ShapeDtypeStruct((batch * L, H * dk), BF16),
                   jax.ShapeDtypeStruct((batch, H, dk, dk), F32))
        + (jax.ShapeDtypeStruct((batch, n_hist, H * dk), F32),) * 3,
        grid=(batch, hg),
        in_specs=[col(0), col(1), col(2), col(3),
                  pl.BlockSpec((L, LANES), lambda b, h: (b, 0)),
                  cw(0), cw(1), cw(2),
                  pl.BlockSpec((None, 1, dk), lambda b, h: (layer, 0, 0))],
        out_specs=(pl.BlockSpec((L, hb * dk), lambda b, h: (b, h)),
                   pl.BlockSpec((None, hb, dk, dk), lambda b, h: (b, h, 0, 0)))
        + (pl.BlockSpec((None, n_hist, hb * dk), lambda b, h: (b, 0, h)),) * 3,
        scratch_shapes=[scr(), scr(), scr(), scr(), scr(),
                        scr(), scr(), scr(), scr(), scr(),
                        pltpu.VMEM((hb, L // GDN_CHUNK * SUBLANES, dk), F32),
                        scr()],
        compiler_params=pltpu.CompilerParams(
            dimension_semantics=("parallel", "parallel"),
            vmem_limit_bytes=_vmem_limit(nbytes)),
        name="gdn_prompt",
    )(qkvz, qkvz, qkvz, qkvz, bg, conv_w, conv_w, conv_w, norm_w)


def _gdn_sample_kernel(q_ref, k_ref, v_ref, z_ref, bg_ref, bq_ref, bk_ref, bv_ref,
                       cwq_ref, cwk_ref, cwv_ref, nw_ref, s_in_ref, *rest, n_heads, stack_layer):
    o_ref, s_out_ref, cq_ref, ck_ref, cv_ref, osc = rest[-6:]
    nb, dk = q_ref.shape
    head = pl.program_id(0)

    def conv_silu(x_ref, buf_ref, new_ref, cw_ref):
        cw = cw_ref[...]
        n_hist = buf_ref.shape[0]
        x = x_ref[...]
        y = x * cw[n_hist:n_hist + 1, :]
        for i in range(n_hist):
            row = buf_ref[i]
            y = y + row * cw[i:i + 1, :]
            if i > 0:
                new_ref[i - 1] = row
        new_ref[n_hist - 1] = x
        return _silu(y)

    q = _l2norm_rows(conv_silu(q_ref, bq_ref, cq_ref, cwq_ref)) * (dk ** -0.5)
    k = _l2norm_rows(conv_silu(k_ref, bk_ref, ck_ref, cwk_ref))
    v = conv_silu(v_ref, bv_ref, cv_ref, cwv_ref)
    beta, g = _head_columns(bg_ref[...], head, n_heads)
    decay = jnp.exp(g)
    kt = jnp.concatenate([k, jnp.zeros((LANES - nb, dk), F32)], axis=0).T if nb < LANES else k.T
    qt = jnp.concatenate([q, jnp.zeros((LANES - nb, dk), F32)], axis=0).T if nb < LANES else q.T
    for b in range(nb):
        s = s_in_ref[b] * decay[b:b + 1, :]
        kcol = kt[:, b:b + 1]
        v_new = (v[b:b + 1, :] - _sum_rows(s * kcol)) * beta[b:b + 1, :]
        s = s + kcol * v_new
        if stack_layer is None:
            s_out_ref[b] = s
        else:
            for d in range(s_out_ref.shape[0]):
                s_out_ref[d, b] = s if d == stack_layer else jnp.zeros_like(s)
        osc[b:b + 1, :] = _sum_rows(s * qt[:, b:b + 1])
    o_ref[...] = _gated_out_norm(osc[...], z_ref[...], nw_ref[...]).astype(o_ref.dtype)


def _gdn_sample(qkvz, bg, row0, conv_buf, conv_w, norm_w, state, layer, n_heads, dk,
                state_out=None):
    H = n_heads
    nb_total = state.shape[1]
    nb = 16
    assert nb_total % nb == 0 and row0 % nb == 0
    r0 = row0 // nb
    n_hist = conv_buf.shape[1]

    def col(off):
        return pl.BlockSpec((nb, dk), lambda h, i, off=off: (i + r0, h + off))

    def buf(off):
        return pl.BlockSpec((None, n_hist, nb, dk), lambda h, i, off=off: (layer, 0, i, h + off))

    def cw(off):
        return pl.BlockSpec((None, conv_w.shape[1], dk), lambda h, i, off=off: (layer, 0, h + off))

    in_specs = [col(0), col(H), col(2 * H), col(3 * H),
                pl.BlockSpec((nb, LANES), lambda h, i: (i + r0, 0)),
                buf(0), buf(H), buf(2 * H),
                cw(0), cw(H), cw(2 * H),
                pl.BlockSpec((None, 1, dk), lambda h, i: (layer, 0, 0)),
                pl.BlockSpec((None, nb, None, dk, dk), lambda h, i: (layer, i, h, 0, 0))]
    args = [qkvz, qkvz, qkvz, qkvz, bg, conv_buf, conv_buf, conv_buf,
            conv_w, conv_w, conv_w, norm_w, state]
    aliases = {}
    if state_out is not None:
        in_specs.append(pl.BlockSpec(memory_space=pl.ANY))
        args.append(state_out)
        aliases = {len(args) - 1: 1}
    depth = state.shape[0]
    if state_out is None:
        stack_spec = pl.BlockSpec((depth, nb, None, dk, dk), lambda h, i: (0, i, h, 0, 0))
    else:
        stack_spec = pl.BlockSpec((None, nb, None, dk, dk), lambda h, i: (layer, i, h, 0, 0))
    nbytes = (4 + 2 * depth) * nb * dk * dk * 4 + 64 * nb * dk * 4 + 64 * dk * dk * 4
    return pl.pallas_call(
        functools.partial(_gdn_sample_kernel, n_heads=H,
                          stack_layer=layer if state_out is None else None),
        out_shape=(jax.ShapeDtypeStruct((nb_total, H * dk), BF16),
                   jax.ShapeDtypeStruct(state.shape, F32))
        + (jax.ShapeDtypeStruct((n_hist, nb_total, H * dk), F32),) * 3,
        grid=(H, nb_total // nb),
        in_specs=in_specs,
        out_specs=(pl.BlockSpec((nb, dk), lambda h, i: (i, h)), stack_spec)
        + (pl.BlockSpec((n_hist, nb, dk), lambda h, i: (0, i, h)),) * 3,
        scratch_shapes=[pltpu.VMEM((nb, dk), F32)],
        input_output_aliases=aliases,
        compiler_params=pltpu.CompilerParams(
            dimension_semantics=("parallel", "parallel"),
            vmem_limit_bytes=_vmem_limit(nbytes)),
        name="gdn_sample",
    )(*args)


def _s5_tables(lam_re, lam_im, log_dt, b_re, b_im, c_re, c_im):
    D, G, P = lam_re.shape
    gc = b_re.shape[-1]
    gpb = LANES // gc
    nblk = G // gpb
    dt = jnp.exp(log_dt)[..., None]
    mag = jnp.exp(lam_re * dt)
    ar = mag * jnp.cos(lam_im * dt)
    ai = mag * jnp.sin(lam_im * dt)
    nr = ar - 1.0
    den = lam_re * lam_re + lam_im * lam_im
    fr = (nr * lam_re + ai * lam_im) / den
    fi = (ai * lam_re - nr * lam_im) / den
    bbar_re = fr[..., None] * b_re - fi[..., None] * b_im
    bbar_im = fr[..., None] * b_im + fi[..., None] * b_re
    rg = lax.broadcasted_iota(jnp.int32, (gpb * gc, gpb * P), 0) // gc
    cg = lax.broadcasted_iota(jnp.int32, (gpb * gc, gpb * P), 1) // P
    diag = rg == cg

    def bmat(bb):
        t = jnp.swapaxes(bb.reshape(D, nblk, gpb * P, gc), 2, 3)
        return jnp.where(diag, jnp.tile(t, (1, 1, gpb, 1)), 0.0)

    def cmat(cc):
        t = jnp.swapaxes(cc.reshape(D, nblk, gpb, gc, P), 3, 4).reshape(D, nblk, gpb * P, gc)
        return jnp.where(diag.T, jnp.tile(t, (1, 1, 1, gpb)), 0.0)

    b_blk = jnp.concatenate([bmat(bbar_re), bmat(bbar_im)], axis=3).astype(BF16)
    c_blk = jnp.concatenate([cmat(c_re), -cmat(c_im)], axis=2).astype(BF16)
    return (b_blk, c_blk, ar.reshape(D, nblk, 1, gpb * P), ai.reshape(D, nblk, 1, gpb * P))


def _s5_prompt_kernel(u_ref, bblk_ref, cblk_ref, ar_ref, ai_ref, d_ref,
                      g5_ref, g5b_ref, xre_ref, xim_ref, up, xs0, xs1, ys):
    L = u_ref.shape[0]
    ns = ar_ref.shape[-1]
    nseg = S5_SEGMENTS
    seg = L // nseg
    R = xs0.shape[0] // max(xs0.shape[0] // (S5_BLOCK_STEPS * nseg), 1)
    steps = R // nseg
    n_pair = xs0.shape[0] // R
    n_blk = 2 * n_pair
    assert n_blk * R == L

    bmat = bblk_ref[...]
    cmat = cblk_ref[...]
    ar = jnp.broadcast_to(ar_ref[...], (nseg, ns))
    ai = jnp.broadcast_to(ai_ref[...], (nseg, ns))

    def block(row0):
        return pl.ds(pl.multiple_of(row0, R), R)

    def project(buf, blk, row0):
        u = jnp.concatenate([u_ref[pl.ds(blk * steps + t, nseg, stride=seg), :]
                             for t in range(steps)], axis=0)
        up[block(blk * R), :] = u
        buf[block(row0), :] = jnp.dot(u.astype(BF16), bmat, preferred_element_type=F32)

    def scan(buf, row0, x, store):
        for t in range(steps):
            rows = pl.ds(pl.multiple_of(row0 + t * nseg, nseg), nseg)
            r = buf[rows, :]
            xr, xi = x
            x = (ar * xr - ai * xi + r[:, :ns], ar * xi + ai * xr + r[:, ns:])
            if store:
                buf[rows, :] = jnp.concatenate(x, axis=1)
        return x

    def emit(buf, row0, blk):
        y = jnp.dot(buf[block(row0), :].astype(BF16), cmat, preferred_element_type=F32)
        ys[block(blk * R), :] = _gelu_tanh(y + d_ref[...] * up[block(blk * R), :])

    project(xs0, 0, 0)

    def pass1(p, e):
        row0 = p * R
        project(xs1, 2 * p + 1, row0)
        e = scan(xs0, row0, e, False)
        nxt = jnp.minimum(p + 1, n_pair - 1)
        project(xs0, 2 * nxt, nxt * R)
        return scan(xs1, row0, e, False)

    zero = jnp.zeros((nseg, ns), F32)
    er, ei = lax.fori_loop(0, n_pair, pass1, (zero, zero))
    pr, pi = ar_ref[...], ai_ref[...]
    for _ in range(seg.bit_length() - 1):
        pr, pi = pr * pr - pi * pi, 2.0 * pr * pi
    assert seg == 1 << (seg.bit_length() - 1)
    cr = [jnp.zeros((1, ns), F32)]
    ci = [jnp.zeros((1, ns), F32)]
    for s in range(nseg - 1):
        cr.append(er[s:s + 1] + pr * cr[s] - pi * ci[s])
        ci.append(ei[s:s + 1] + pr * ci[s] + pi * cr[s])
    x = (jnp.concatenate(cr, axis=0), jnp.concatenate(ci, axis=0))

    x = scan(xs0, 0, x, True)

    def pass2(p, x):
        row0 = p * R
        x = scan(xs1, row0, x, True)
        emit(xs0, row0, 2 * p)
        x = scan(xs0, row0 + R, x, True)
        emit(xs1, row0, 2 * p + 1)
        return x

    x = lax.fori_loop(0, n_pair - 1, pass2, x)
    last = (n_pair - 1) * R
    x = scan(xs1, last, x, True)
    emit(xs0, last, n_blk - 2)
    emit(xs1, last, n_blk - 1)
    xre_ref[...] = x[0][nseg - 1:nseg]
    xim_ref[...] = x[1][nseg - 1:nseg]
    for s in range(nseg):
        g5 = ys[pl.ds(s, seg, stride=nseg), :]
        g5_ref[s * seg:(s + 1) * seg, :] = g5
        g5b_ref[s * seg:(s + 1) * seg, :] = g5.astype(BF16)


def _s5_prompt(u_all, tables, d_skip, layer, batch, seq, n_ch):
    b_blk, c_blk, ar, ai = tables
    nblk, _, ns2 = b_blk.shape[1:]
    ns = ns2 // 2
    L = seq
    nbytes = (2 * L * LANES * 4 + 2 * L * LANES * 6 + 3 * L * ns2 * 4 + L * ns2 * 2
              + 8 * LANES * ns2 * 2)
    blkp = lambda r, c: pl.BlockSpec((None, None, r, c), lambda b, j: (layer, j, 0, 0))
    return pl.pallas_call(
        _s5_prompt_kernel,
        out_shape=(jax.ShapeDtypeStruct((batch * L, n_ch), F32),
                   jax.ShapeDtypeStruct((batch * L, n_ch), BF16),
                   jax.ShapeDtypeStruct((batch, 1, nblk * ns), F32),
                   jax.ShapeDtypeStruct((batch, 1, nblk * ns), F32)),
        grid=(batch, nblk),
        in_specs=[pl.BlockSpec((L, LANES), lambda b, j: (b, j)),
                  blkp(LANES, ns2), blkp(ns2, LANES), blkp(1, ns), blkp(1, ns),
                  pl.BlockSpec((None, 1, LANES), lambda b, j: (layer, 0, j))],
        out_specs=(pl.BlockSpec((L, LANES), lambda b, j: (b, j)),
                   pl.BlockSpec((L, LANES), lambda b, j: (b, j)),
                   pl.BlockSpec((None, 1, ns), lambda b, j: (b, 0, j)),
                   pl.BlockSpec((None, 1, ns), lambda b, j: (b, 0, j))),
        scratch_shapes=[pltpu.VMEM((L, LANES), F32),
                        pltpu.VMEM((L // 2, ns2), F32),
                        pltpu.VMEM((L // 2, ns2), F32),
                        pltpu.VMEM((L, LANES), F32)],
        compiler_params=pltpu.CompilerParams(
            dimension_semantics=("parallel", "parallel"),
            vmem_limit_bytes=_vmem_limit(nbytes)),
        name="s5_prompt",
    )(u_all, b_blk, c_blk, ar, ai, d_skip)


def _s5_sample_kernel(u_ref, bblk_ref, cblk_ref, ar_ref, ai_ref, d_ref, x0r_ref, x0i_ref,
                      g5_ref, g5b_ref, xre_ref, xim_ref):
    ns = ar_ref.shape[-1]
    u = u_ref[...]
    bu = jnp.dot(u.astype(BF16), bblk_ref[...], preferred_element_type=F32)
    ar, ai = ar_ref[...], ai_ref[...]
    x0r, x0i = x0r_ref[...], x0i_ref[...]
    xr = ar * x0r - ai * x0i + bu[:, :ns]
    xi = ar * x0i + ai * x0r + bu[:, ns:]
    xre_ref[...] = xr
    xim_ref[...] = xi
    x = jnp.concatenate([xr, xi], axis=1)
    y = jnp.dot(x.astype(BF16), cblk_ref[...], preferred_element_type=F32) + d_ref[...] * u
    g5 = _gelu_tanh(y)
    g5_ref[...] = g5
    g5b_ref[...] = g5.astype(BF16)


def _s5_sample(u_all, row0, tables, d_skip, x0_re, x0_im, layer, n_ch):
    b_blk, c_blk, ar, ai = tables
    nblk, _, ns2 = b_blk.shape[1:]
    ns = ns2 // 2
    nb = x0_re.shape[1]
    assert row0 % nb == 0
    r0 = row0 // nb
    blkp = lambda r, c: pl.BlockSpec((None, None, r, c), lambda j: (layer, j, 0, 0))
    nbytes = 16 * nb * ns2 * 4 + 8 * LANES * ns2 * 2
    return pl.pallas_call(
        _s5_sample_kernel,
        out_shape=(jax.ShapeDtypeStruct((nb, n_ch), F32),
                   jax.ShapeDtypeStruct((nb, n_ch), BF16),
                   jax.ShapeDtypeStruct((nb, nblk * ns), F32),
                   jax.ShapeDtypeStruct((nb, nblk * ns), F32)),
        grid=(nblk,),
        in_specs=[pl.BlockSpec((nb, LANES), lambda j: (r0, j)),
                  blkp(LANES, ns2), blkp(ns2, LANES), blkp(1, ns), blkp(1, ns),
                  pl.BlockSpec((None, 1, LANES), lambda j: (layer, 0, j)),
                  pl.BlockSpec((None, nb, ns), lambda j: (layer, 0, j)),
                  pl.BlockSpec((None, nb, ns), lambda j: (layer, 0, j))],
        out_specs=(pl.BlockSpec((nb, LANES), lambda j: (0, j)),
                   pl.BlockSpec((nb, LANES), lambda j: (0, j)),
                   pl.BlockSpec((nb, ns), lambda j: (0, j)),
                   pl.BlockSpec((nb, ns), lambda j: (0, j))),
        compiler_params=pltpu.CompilerParams(
            dimension_semantics=("parallel",), vmem_limit_bytes=_vmem_limit(nbytes)),
        name="s5_sample",
    )(u_all, b_blk, c_blk, ar, ai, d_skip, x0_re, x0_im)


def kernel(x_prompt, x_sample, state_dn_conv, state_dn_ssm, state_s5_re, state_s5_im, norm1, w_in, dn_conv_w, dn_a_log, dn_dt_bias, dn_norm_w, w_br_dn, s5_lam_re, s5_lam_im, s5_log_dt, s5_b_re, s5_b_im, s5_c_re, s5_c_im, s5_d, w_glu, w_br_s5, w_out, norm2, w_ffn_gate, w_ffn_up, w_ffn_down, norm_f):
    batch, seq, d_model = x_prompt.shape
    nb, dec_seq, _ = x_sample.shape
    assert dec_seq == 1
    depth, _, n_heads, dk, dv = state_dn_ssm.shape
    assert dk == LANES and dv == LANES and seq % GDN_CHUNK == 0 and seq % S5_SEGMENTS == 0
    qk_dim = n_heads * dk
    conv_ch = dn_conv_w.shape[2]
    assert conv_ch == 3 * qk_dim
    n_ch = s5_d.shape[1]
    n_groups, n_state = s5_lam_re.shape[1:]
    ffn = w_ffn_gate.shape[2]
    mp = batch * seq
    m = mp + nb
    z_end = 4 * qk_dim
    rest0 = z_end + 2 * n_heads
    assert w_in.shape[2] == rest0 + n_ch + 2 * d_model and 2 * n_heads <= LANES

    x = (x_prompt.reshape(mp, d_model), x_sample.reshape(nb, d_model))

    w_in_t = jnp.swapaxes(w_in, 1, 2)
    w_bg = jnp.pad(w_in_t[:, z_end:rest0, :], ((0, 0), (0, LANES - 2 * n_heads), (0, 0)))
    w_s5 = w_in_t[:, rest0:rest0 + n_ch, :]
    gate_tn = 512
    gate0 = rest0 + n_ch
    gate_base = gate0 - gate0 % gate_tn
    gate_shift = gate0 - gate_base
    gate_cols = -(-(gate_shift + 2 * d_model) // gate_tn) * gate_tn
    assert gate_shift < LANES and gate_base + gate_cols - gate_tn < w_in.shape[2]
    conv_hist = jnp.swapaxes(state_dn_conv, 1, 2)
    pad_heads = lambda a: jnp.pad(a, ((0, 0), (n_heads, LANES - 2 * n_heads)))[:, None, :]
    alog_pad = pad_heads(dn_a_log)
    dtb_pad = pad_heads(dn_dt_bias)
    x0_re = state_s5_re.reshape(depth, nb, n_groups * n_state)
    x0_im = state_s5_im.reshape(depth, nb, n_groups * n_state)

    tm = _pick_tile(m, 1664, 64)
    norm_w3 = dn_norm_w[:, None, :]
    d_skip3 = s5_d[:, None, :]
    tables = _s5_tables(s5_lam_re, s5_lam_im, s5_log_dt, s5_b_re, s5_b_im, s5_c_re, s5_c_im)
    outs = {k: [] for k in ("p_conv", "p_ssm", "p_re", "p_im", "s_conv", "s_re", "s_im")}
    s_ssm = None
    for l in range(depth):
        h = _rmsnorm(x, norm1[l][None, :], BF16)
        qkvz = _fused_matmul([h], [(0, w_in_t, l, 0, True)], [], _ep_identity, z_end, F32,
                             tm=tm, tn=512, name="in_qkvz")
        bg = _fused_matmul([h], [(0, w_bg, l, 0, True)], [], _ep_beta_decay(n_heads), LANES, F32,
                           tm=tm, tn=LANES, name="in_bg",
                           col_params=[(alog_pad, l), (dtb_pad, l)])
        s5u = _fused_matmul([h], [(0, w_s5, l, 0, True)], [], _ep_identity, n_ch, F32,
                            tm=tm, tn=512, name="in_s5")
        gates = _fused_matmul([h], [(0, w_in_t, l, gate_base, True)], [], _ep_identity, gate_cols, F32,
                              tm=tm, tn=gate_tn, name="in_gates",
                              valid_cols=gate_shift + 2 * d_model)

        o_p, ssm_p, *pc = _gdn_prompt(qkvz, bg, dn_conv_w, norm_w3, l, batch, seq, n_heads, dk)
        o_s, s_ssm, *sc = _gdn_sample(qkvz, bg, mp, conv_hist, dn_conv_w, norm_w3,
                                      state_dn_ssm, l, n_heads, dk, state_out=s_ssm)
        outs["p_conv"].append(jnp.concatenate(pc, axis=-1))
        outs["s_conv"].append(jnp.concatenate(sc, axis=-1))
        outs["p_ssm"].append(ssm_p)

        g5_p, g5b_p, re_p, im_p = _s5_prompt(s5u, tables, d_skip3, l, batch, seq, n_ch)
        g5_s, g5b_s, re_s, im_s = _s5_sample(s5u, mp, tables, d_skip3, x0_re, x0_im, l, n_ch)
        outs["p_re"].append(re_p.reshape(batch, n_groups, n_state))
        outs["p_im"].append(im_p.reshape(batch, n_groups, n_state))
        outs["s_re"].append(re_s.reshape(nb, n_groups, n_state))
        outs["s_im"].append(im_s.reshape(nb, n_groups, n_state))
        g5g = _fused_matmul([(g5b_p, g5b_s)], [(0, w_glu, l, 0)], [((g5_p, g5_s), 0)],
                            _ep_glu_self, n_ch, BF16, tm=tm, tn=512, name="s5_glu")

        merged = _fused_matmul([(o_p, o_s), g5g], [(0, w_br_dn, l, 0), (1, w_br_s5, l, 0)],
                               [(gates, gate_shift), (gates, gate_shift + d_model)],
                               _ep_gated_merge,
                               d_model, BF16, tm=tm, tn=512, name="branch_merge")
        x = _fused_matmul([merged], [(0, w_out, l, 0)], [(x, 0)], _ep_residual, d_model, F32,
                          tm=tm, tn=512, name="out_proj")

        hmid = _fused_matmul([x], [(0, w_ffn_gate, l, 0), (0, w_ffn_up, l, 0)], [], _ep_swiglu,
                             ffn, BF16, tm=tm, tn=256, name="ffn_up", norm_w=norm2[l][None, :])
        x = _matmul_residual_wstat(hmid, w_ffn_down, l, x, tm=_pick_tile(m, 832, 16), tn=512,
                                   name="ffn_down")

    y_p, y_s = _rmsnorm(x, norm_f[None, :], F32, split_rows=(mp, nb))
    st = lambda k: jnp.stack(outs[k])
    return (y_p.reshape(batch, seq, d_model), y_s.reshape(nb, 1, d_model),
            st("p_conv"), st("p_ssm"), st("p_re"), st("p_im"),
            jnp.swapaxes(st("s_conv"), 1, 2), s_ssm, st("s_re"), st("s_im"))
```

```python
import functools
import math

import jax
import jax.numpy as jnp
from jax import lax
from jax.experimental import pallas as pl
from jax.experimental.pallas import tpu as pltpu

F32 = jnp.float32
BF16 = jnp.bfloat16

NORM_EPS = 1e-6
L2_EPS = 1e-6
LANES = 128
SUBLANES = 8
VMEM_CAP_BYTES = 56 * 1024 * 1024
GDN_CHUNK = 128
S5_SEGMENTS = SUBLANES
GDN_HEADS_PER_STEP = 2
GDN_CHUNKS_PER_GROUP = 16
S5_BLOCK_STEPS = 32


def _vmem_limit(nbytes):
    return int(min(VMEM_CAP_BYTES, nbytes * 5 // 4 + (4 << 20)))


def _pick_tile(n, target, mult):
    best = None
    for t in range(mult, min(n, target) + 1, mult):
        if n % t == 0:
            best = t
    return best if best is not None else n


def _sigmoid(x):
    return 1.0 / (1.0 + jnp.exp(-x))


def _silu(x):
    return x * _sigmoid(x)


def _softplus(x):
    return jnp.maximum(x, 0.0) + jnp.log1p(jnp.exp(-jnp.abs(x)))


def _gelu_tanh(x):
    c = math.sqrt(2.0 / math.pi)
    return 0.5 * x * (1.0 + jnp.tanh(c * (x + 0.044715 * (x * x * x))))


def _bdot(a, b):
    return jnp.dot(a.astype(BF16), b.astype(BF16), preferred_element_type=F32)


def _bdot_nt(a, b):
    return lax.dot_general(a.astype(BF16), b.astype(BF16), (((1,), (1,)), ((), ())),
                           preferred_element_type=F32)


def _rows_of(op):
    return op[0].shape[0] + op[1].shape[0] if isinstance(op, tuple) else op.shape[0]


def _row_specs(op, tm, ncols, index_map):
    if not isinstance(op, tuple):
        return [pl.BlockSpec((tm, ncols), index_map)], [op], None
    p, s = op
    tail = p.shape[0] % tm
    assert tail + s.shape[0] == tm and tail % 16 == 0, (p.shape, s.shape, tm)

    def s_map(*idx):
        return (0,) + tuple(index_map(*idx)[1:])

    return ([pl.BlockSpec((tm, ncols), index_map), pl.BlockSpec((s.shape[0], ncols), s_map)],
            [p, s], tail)


def _load_rows(refs, tail, last):
    if tail is None or not last:
        return refs[0][...]
    return jnp.concatenate([refs[0][:tail, :], refs[1][...]], axis=0)


def _lane_window(parts, shift, width):
    x = parts[0] if len(parts) == 1 else jnp.concatenate(parts, axis=1)
    if shift:
        x = pltpu.roll(x, x.shape[1] - shift, 1)
    return x[:, :width]


def _on_row_tiles(i, n_tiles, any_split, body):
    if not any_split:
        body(False)
        return
    if n_tiles > 1:
        pl.when(i < n_tiles - 1)(lambda: body(False))
    pl.when(i == n_tiles - 1)(lambda: body(True))


def _rmsnorm_kernel(*refs, tail, n_tiles, out_tail):
    n_out = 1 if out_tail is None else 2
    x_refs, w_ref, o_refs = refs[:-1 - n_out], refs[-1 - n_out], refs[-n_out:]

    def body(last):
        x = _load_rows(x_refs, tail, last)
        y = x * lax.rsqrt(jnp.mean(x * x, axis=-1, keepdims=True) + NORM_EPS)
        y = (y * w_ref[...]).astype(o_refs[0].dtype)
        if out_tail is None:
            o_refs[0][...] = y
        elif not last:
            o_refs[0][...] = y
        else:
            o_refs[0][:out_tail, :] = y[:out_tail]
            o_refs[1][...] = y[out_tail:]

    _on_row_tiles(pl.program_id(0), n_tiles, tail is not None or out_tail is not None, body)


def _rmsnorm(x, w_row, out_dtype, split_rows=None):
    m = _rows_of(x)
    d = w_row.shape[1]
    tr = _pick_tile(m, 832, 64)
    specs, arrs, tail = _row_specs(x, tr, d, lambda i: (i, 0))
    nbytes = 2 * tr * d * 4 + 2 * tr * d * jnp.dtype(out_dtype).itemsize + 3 * tr * d * 4
    if split_rows is None:
        out_shape = jax.ShapeDtypeStruct((m, d), out_dtype)
        out_specs = pl.BlockSpec((tr, d), lambda i: (i, 0))
        out_tail = None
    else:
        mp, nb = split_rows
        out_tail = mp % tr
        assert mp + nb == m and out_tail + nb == tr
        out_shape = (jax.ShapeDtypeStruct((mp, d), out_dtype),
                     jax.ShapeDtypeStruct((nb, d), out_dtype))
        out_specs = (pl.BlockSpec((tr, d), lambda i: (i, 0)),
                     pl.BlockSpec((nb, d), lambda i: (0, 0)))
    return pl.pallas_call(
        functools.partial(_rmsnorm_kernel, tail=tail, n_tiles=m // tr, out_tail=out_tail),
        out_shape=out_shape,
        grid=(m // tr,),
        in_specs=specs + [pl.BlockSpec((1, d), lambda i: (0, 0))],
        out_specs=out_specs,
        compiler_params=pltpu.CompilerParams(
            dimension_semantics=("arbitrary",), vmem_limit_bytes=_vmem_limit(nbytes)),
        name="rmsnorm",
    )(*arrs, w_row)


def _mm_kernel(*refs, a_idx, w_transposed, a_groups, e_groups, n_tiles, epilogue, normed,
               zero_tail_from):
    pos = 0
    a_refs = []
    for n, _ in a_groups:
        a_refs.append(refs[pos:pos + n])
        pos += n
    w_refs = refs[pos:pos + len(a_idx)]
    pos += len(a_idx)
    e_refs = []
    for n, _, _ in e_groups:
        e_refs.append(refs[pos:pos + n])
        pos += n
    if normed:
        nw_ref, o_ref, h_scr = refs[pos:pos + 3]
    else:
        o_ref = refs[pos]
    any_split = any(g[1] is not None for g in a_groups + e_groups)

    def load_extra(r, tail, shift, last):
        if shift:
            return _lane_window([r[0][...], r[1][...]], shift, o_ref.shape[1])
        return _load_rows(r, tail, last)

    def body(last):
        if normed:
            @pl.when(pl.program_id(1) == 0)
            def _():
                x = _load_rows(a_refs[0], a_groups[0][1], last)
                y = x * lax.rsqrt(jnp.mean(x * x, axis=-1, keepdims=True) + NORM_EPS)
                h_scr[...] = (y * nw_ref[...]).astype(BF16)

            a_vals = [h_scr[...]]
        else:
            a_vals = [_load_rows(r, t, last) for r, (_, t) in zip(a_refs, a_groups)]
        parts = [(_bdot_nt if wt else _bdot)(a_vals[ai], w[...])
                 for ai, wt, w in zip(a_idx, w_transposed, w_refs)]
        e_vals = [load_extra(r, t, sh, last) for r, (_, t, sh) in zip(e_refs, e_groups)]
        o_ref[...] = epilogue(parts, e_vals).astype(o_ref.dtype)
        if zero_tail_from is not None:
            @pl.when(pl.program_id(1) == pl.num_programs(1) - 1)
            def _():
                o_ref[:, zero_tail_from:] = jnp.zeros(
                    (o_ref.shape[0], o_ref.shape[1] - zero_tail_from), o_ref.dtype)

    _on_row_tiles(pl.program_id(0), n_tiles, any_split, body)


def _fused_matmul(a_list, w_list, extras, epilogue, n_out, out_dtype, *, tm, tn, name,
                  col_params=(), norm_w=None, valid_cols=None):
    m = _rows_of(a_list[0])
    assert m % tm == 0 and n_out % tn == 0
    in_specs, args, a_groups, e_groups = [], [], [], []
    kdims = []
    for a in a_list:
        kd = (a[0] if isinstance(a, tuple) else a).shape[1]
        specs, arrs, tail = _row_specs(a, tm, kd, lambda i, j: (i, 0))
        assert _rows_of(a) == m
        in_specs += specs
        args += arrs
        a_groups.append((len(arrs), tail))
        kdims.append(kd)
    w_list = [tuple(e) + (False,) * (5 - len(e)) for e in w_list]
    for ai, w, layer, col0, transposed in w_list:
        assert col0 % tn == 0 and w.shape[2 if transposed else 1] == kdims[ai]
        if transposed:
            spec = pl.BlockSpec((None, tn, w.shape[2]),
                                lambda i, j, layer=layer, off=col0 // tn: (layer, j + off, 0))
        else:
            spec = pl.BlockSpec((None, w.shape[1], tn),
                                lambda i, j, layer=layer, off=col0 // tn: (layer, 0, j + off))
        in_specs.append(spec)
        args.append(w)
    for e, col0 in extras:
        shift = col0 % tn
        assert shift < LANES and _rows_of(e) == m
        specs, arrs, tail = _row_specs(e, tm, tn, lambda i, j, off=col0 // tn: (i, j + off))
        if shift:
            assert not isinstance(e, tuple) and tn % LANES == 0
            specs.append(pl.BlockSpec(
                (tm, LANES), lambda i, j, off=col0 // tn: (i, (j + off + 1) * (tn // LANES))))
            arrs.append(e)
        in_specs += specs
        args += arrs
        e_groups.append((len(arrs), tail, shift))
    for p, layer in col_params:
        in_specs.append(pl.BlockSpec((None, 1, tn), lambda i, j, layer=layer: (layer, 0, j)))
        args.append(p)
        e_groups.append((1, None, 0))
    normed = norm_w is not None
    scratch = []
    if normed:
        assert len(a_list) == 1
        in_specs.append(pl.BlockSpec((1, kdims[0]), lambda i, j: (0, 0)))
        args.append(norm_w)
        scratch.append(pltpu.VMEM((tm, kdims[0]), BF16))
    osz = jnp.dtype(out_dtype).itemsize
    nbytes = (sum(3 * tm * kd * (6 if normed else 2) for kd in kdims)
              + sum(kdims[ai] * tn * (2 * w.dtype.itemsize + 2) for ai, w, _, _, _ in w_list)
              + sum(3 * tm * tn * 4 for _ in extras)
              + 2 * tm * tn * osz + (2 + len(w_list)) * tm * tn * 4)
    kern = functools.partial(_mm_kernel, a_idx=tuple(e[0] for e in w_list),
                             w_transposed=tuple(e[4] for e in w_list), a_groups=tuple(a_groups), e_groups=tuple(e_groups),
                             n_tiles=m // tm, epilogue=epilogue, normed=normed,
                             zero_tail_from=None if valid_cols is None else valid_cols % tn)
    assert valid_cols is None or n_out - tn < valid_cols < n_out
    return pl.pallas_call(
        kern,
        out_shape=jax.ShapeDtypeStruct((m, n_out), out_dtype),
        grid=(m // tm, n_out // tn),
        in_specs=in_specs,
        out_specs=pl.BlockSpec((tm, tn), lambda i, j: (i, j)),
        scratch_shapes=scratch,
        compiler_params=pltpu.CompilerParams(
            dimension_semantics=("parallel", "arbitrary" if normed else "parallel"),
            vmem_limit_bytes=_vmem_limit(nbytes)),
        name=name,
    )(*args)


def _mm_residual_wstat_kernel(a_ref, w_ref, x_ref, o_ref, wb):
    @pl.when(pl.program_id(1) == 0)
    def _():
        wb[...] = w_ref[...].astype(BF16)

    o_ref[...] = x_ref[...] + jnp.dot(a_ref[...], wb[...], preferred_element_type=F32)


def _matmul_residual_wstat(a, w, layer, x, *, tm, tn, name):
    m, kdim = a.shape
    n_out = w.shape[2]
    assert m % tm == 0 and n_out % tn == 0 and w.shape[1] == kdim
    nbytes = 2 * tm * kdim * 2 + kdim * tn * (w.dtype.itemsize + 2) + 6 * tm * tn * 4
    return pl.pallas_call(
        _mm_residual_wstat_kernel,
        out_shape=jax.ShapeDtypeStruct((m, n_out), F32),
        grid=(n_out // tn, m // tm),
        in_specs=[pl.BlockSpec((tm, kdim), lambda j, i: (i, 0)),
                  pl.BlockSpec((None, kdim, tn), lambda j, i: (layer, 0, j),
                               pipeline_mode=pl.Buffered(1)),
                  pl.BlockSpec((tm, tn), lambda j, i: (i, j))],
        out_specs=pl.BlockSpec((tm, tn), lambda j, i: (i, j)),
        scratch_shapes=[pltpu.VMEM((kdim, tn), BF16)],
        compiler_params=pltpu.CompilerParams(
            dimension_semantics=("parallel", "arbitrary"),
            vmem_limit_bytes=_vmem_limit(nbytes)),
        name=name,
    )(a, w, x)


def _ep_identity(accs, extras):
    return accs[0]


def _ep_residual(accs, extras):
    return extras[0] + accs[0]


def _ep_swiglu(accs, extras):
    return _silu(accs[0]) * accs[1]


def _ep_glu_self(accs, extras):
    g5 = extras[0]
    return g5 * _sigmoid(accs[0])


def _ep_gated_merge(accs, extras):
    return _sigmoid(extras[0]) * accs[0] + _sigmoid(extras[1]) * accs[1]


def _ep_beta_decay(n_heads):
    def ep(accs, extras):
        acc = accs[0]
        alog_row, dtb_row = extras
        lane = lax.broadcasted_iota(jnp.int32, acc.shape, 1)
        return jnp.where(lane < n_heads, _sigmoid(acc),
                         -jnp.exp(alog_row) * _softplus(acc + dtb_row))
    return ep


def _head_columns(bg, head, n_heads):
    lane = lax.broadcasted_iota(jnp.int32, bg.shape, 1)
    beta = jnp.sum(jnp.where(lane == head, bg, 0.0), axis=-1, keepdims=True)
    g = jnp.sum(jnp.where(lane == head + n_heads, bg, 0.0), axis=-1, keepdims=True)
    return beta, g


def _l2norm_rows(x):
    return x * lax.rsqrt(jnp.sum(x * x, axis=-1, keepdims=True) + L2_EPS)


def _sum_rows(x):
    acc = x[0:SUBLANES]
    for i in range(1, x.shape[0] // SUBLANES):
        acc = acc + x[i * SUBLANES:(i + 1) * SUBLANES]
    shift = SUBLANES // 2
    while shift:
        acc = acc + pltpu.roll(acc, shift, 0)
        shift //= 2
    return acc[0:1]


def _gated_out_norm(o, z, nw_row):
    y = o * lax.rsqrt(jnp.mean(o * o, axis=-1, keepdims=True) + NORM_EPS)
    return y * nw_row * _silu(z)


def _gdn_prompt_kernel(q_ref, k_ref, v_ref, z_ref, bg_ref, cwq_ref, cwk_ref, cwv_ref, nw_ref,
                       o_ref, s_ref, pcq_ref, pck_ref, pcv_ref,
                       qn, kn, vn, gb, bb, us, ws, qks, qds, kdt, gl, osc, *, n_heads):
    hb, L, dk = qn.shape
    n_hist = pcq_ref.shape[0]
    for x_ref, pc_ref in ((q_ref, pcq_ref), (k_ref, pck_ref), (v_ref, pcv_ref)):
        pc_ref[...] = x_ref[L - n_hist:L, :]
    C = GDN_CHUNK
    n_chunks = L // C
    H8 = range(hb)

    row8 = lax.broadcasted_iota(jnp.int32, (SUBLANES, dk), 0)

    def conv_silu(x_ref, cw_ref, cols):
        cw = cw_ref[:, cols]
        n_taps = cw.shape[0]
        assert n_taps - 1 <= SUBLANES
        tap = lambda j: cw[n_taps - 1 - j:n_taps - j, :]
        head8 = x_ref[0:SUBLANES, cols]
        lo = head8 * tap(0)
        hi = x_ref[SUBLANES:L, cols] * tap(0)
        for j in range(1, n_taps):
            lo = lo + jnp.where(row8 >= j, pltpu.roll(head8, j, 0), 0.0) * tap(j)
            hi = hi + x_ref[pl.ds(SUBLANES - j, L - SUBLANES), cols] * tap(j)
        return _silu(jnp.concatenate([lo, hi], axis=0))

    bg = bg_ref[...]
    for hh in H8:
        cols = slice(hh * dk, (hh + 1) * dk)
        qn[hh] = _l2norm_rows(conv_silu(q_ref, cwq_ref, cols)) * (dk ** -0.5)
        kn[hh] = _l2norm_rows(conv_silu(k_ref, cwk_ref, cols))
        vn[hh] = conv_silu(v_ref, cwv_ref, cols)
        beta, g = _head_columns(bg, pl.program_id(1) * hb + hh, n_heads)
        bb[hh] = jnp.broadcast_to(beta, (L, dk))
        gb[hh] = jnp.broadcast_to(g, (L, dk))

    ri = lax.broadcasted_iota(jnp.int32, (C, C), 0)
    ci = lax.broadcasted_iota(jnp.int32, (C, C), 1)
    causal = ri >= ci
    strict = ri > ci
    tri_incl = jnp.where(causal, 1.0, 0.0).astype(F32)
    eye = jnp.where(ri == ci, 1.0, 0.0).astype(F32)
    level_masks = []
    n = 1
    while n < C:
        sh = n.bit_length() - 1
        same_2n = (ri >> (sh + 1)) == (ci >> (sh + 1))
        diff_n = (ri >> sh) != (ci >> sh)
        level_masks.append(jnp.where(same_2n & diff_n & strict, 1.0, 0.0).astype(F32))
        n *= 2

    group = math.gcd(n_chunks, GDN_CHUNKS_PER_GROUP)

    def intra_group(hh, i):
        G = range(group)
        rows = [pl.ds(pl.multiple_of((i * group + j) * C, C), C) for j in G]
        k = [kn[hh, r, :] for r in rows]
        bet = [bb[hh, r, :] for r in rows]
        gcb = [jnp.dot(tri_incl, gb[hh, r, :], precision=lax.Precision.HIGHEST,
                       preferred_element_type=F32) for r in rows]
        gamma = [jnp.where(causal, jnp.exp(jnp.minimum(g - g.T, 0.0)), 0.0) for g in gcb]
        kb = [k[j] * bet[j] for j in G]
        a_mat = [jnp.where(strict, _bdot_nt(kb[j], k[j]) * gamma[j], 0.0) for j in G]
        q = [qn[hh, r, :] for r in rows]
        for j in G:
            qks[hh, rows[j], :] = _bdot_nt(q[j], k[j]) * gamma[j]
        t = [eye - a * level_masks[0] for a in a_mat]
        for m in level_masks[1:]:
            x = [_bdot(a_mat[j] * m, t[j]) for j in G]
            t = [t[j] - _bdot(t[j], x[j]) for j in G]
        eg = [jnp.exp(g) for g in gcb]
        for j in G:
            us[hh, rows[j], :] = _bdot(t[j], vn[hh, rows[j], :] * bet[j])
        for j in G:
            ws[hh, rows[j], :] = _bdot(t[j], kb[j] * eg[j])
        for j in G:
            qds[hh, rows[j], :] = q[j] * eg[j]
            g_last = gcb[j][C - 1:C, :]
            kdt[hh, rows[j], :] = (k[j] * jnp.exp(g_last - gcb[j])).T
            gl[hh, pl.ds(pl.multiple_of((i * group + j) * SUBLANES, SUBLANES), SUBLANES), :] = (
                jnp.broadcast_to(jnp.exp(g_last), (SUBLANES, dk)))

    for hh in H8:
        if n_chunks == group:
            intra_group(hh, 0)
        else:
            def body(i, carry, hh=hh):
                intra_group(hh, i)
                return carry
            lax.fori_loop(0, n_chunks // group, body, 0)

    def inter(c, states):
        rows = pl.ds(pl.multiple_of(c * C, C), C)
        ws_s = [_bdot(ws[hh, rows, :], states[hh]) for hh in H8]
        qd_s = [_bdot(qds[hh, rows, :], states[hh]) for hh in H8]
        v_new = [us[hh, rows, :] - ws_s[hh] for hh in H8]
        for hh in H8:
            osc[hh, rows, :] = qd_s[hh] + _bdot(qks[hh, rows, :], v_new[hh])
        decay = [gl[hh, pl.ds(pl.multiple_of(c * SUBLANES, SUBLANES), 1), :] for hh in H8]
        return tuple(states[hh] * decay[hh] + _bdot(kdt[hh, rows, :], v_new[hh]) for hh in H8)

    s_fin = lax.fori_loop(0, n_chunks, inter, tuple(jnp.zeros((dk, dk), F32) for _ in H8))
    for hh in H8:
        cols = slice(hh * dk, (hh + 1) * dk)
        s_ref[hh] = s_fin[hh]
        o_ref[:, cols] = _gated_out_norm(osc[hh], z_ref[:, cols], nw_ref[...]).astype(o_ref.dtype)


def _gdn_prompt(qkvz, bg, conv_w, norm_w, layer, batch, seq, n_heads, dk):
    H = n_heads
    L = seq
    hb = GDN_HEADS_PER_STEP if H % GDN_HEADS_PER_STEP == 0 else 1
    hg = H // hb
    n_hist = conv_w.shape[1] - 1

    def col(sec):
        return pl.BlockSpec((L, hb * dk), lambda b, h, sec=sec: (b, h + sec * hg))

    def cw(sec):
        return pl.BlockSpec((None, conv_w.shape[1], hb * dk),
                            lambda b, h, sec=sec: (layer, 0, h + sec * hg))

    scr = lambda: pltpu.VMEM((hb, L, dk), F32)
    nbytes = hb * (2 * 4 * L * dk * 4 + 11 * L * dk * 4 + 2 * L * dk * 2) + 2 * L * LANES * 4 \
        + 60 * GDN_CHUNK * GDN_CHUNK * 4
    return pl.pallas_call(
        functools.partial(_gdn_prompt_kernel, n_heads=H),
        out_shape=(jax.ShapeDtypeStruct((batch * L, H * dk), BF16),
                   jax.ShapeDtypeStruct((batch, H, dk, dk), F32))
        + (jax.ShapeDtypeStruct((batch, n_hist, H * dk), F32),) * 3,
        grid=(batch, hg),
        in_specs=[col(0), col(1), col(2), col(3),
                  pl.BlockSpec((L, LANES), lambda b, h: (b, 0)),
                  cw(0), cw(1), cw(2),
                  pl.BlockSpec((None, 1, dk), lambda b, h: (layer, 0, 0))],
        out_specs=(pl.BlockSpec((L, hb * dk), lambda b, h: (b, h)),
                   pl.BlockSpec((None, hb, dk, dk), lambda b, h: (b, h, 0, 0)))
        + (pl.BlockSpec((None, n_hist, hb * dk), lambda b, h: (b, 0, h)),) * 3,
        scratch_shapes=[scr(), scr(), scr(), scr(), scr(),
                        scr(), scr(), scr(), scr(), scr(),
                        pltpu.VMEM((hb, L // GDN_CHUNK * SUBLANES, dk), F32),
                        scr()],
        compiler_params=pltpu.CompilerParams(
            dimension_semantics=("parallel", "parallel"),
            vmem_limit_bytes=_vmem_limit(nbytes)),
        name="gdn_prompt",
    )(qkvz, qkvz, qkvz, qkvz, bg, conv_w, conv_w, conv_w, norm_w)


def _gdn_sample_kernel(q_ref, k_ref, v_ref, z_ref, bg_ref, bq_ref, bk_ref, bv_ref,
                       cwq_ref, cwk_ref, cwv_ref, nw_ref, s_in_ref, *rest, n_heads, stack_layer):
    o_ref, s_out_ref, cq_ref, ck_ref, cv_ref, osc = rest[-6:]
    nb, dk = q_ref.shape
    head = pl.program_id(0)

    def conv_silu(x_ref, buf_ref, new_ref, cw_ref):
        cw = cw_ref[...]
        n_hist = buf_ref.shape[0]
        x = x_ref[...]
        y = x * cw[n_hist:n_hist + 1, :]
        for i in range(n_hist):
            row = buf_ref[i]
            y = y + row * cw[i:i + 1, :]
            if i > 0:
                new_ref[i - 1] = row
        new_ref[n_hist - 1] = x
        return _silu(y)

    q = _l2norm_rows(conv_silu(q_ref, bq_ref, cq_ref, cwq_ref)) * (dk ** -0.5)
    k = _l2norm_rows(conv_silu(k_ref, bk_ref, ck_ref, cwk_ref))
    v = conv_silu(v_ref, bv_ref, cv_ref, cwv_ref)
    beta, g = _head_columns(bg_ref[...], head, n_heads)
    decay = jnp.exp(g)
    kt = jnp.concatenate([k, jnp.zeros((LANES - nb, dk), F32)], axis=0).T if nb < LANES else k.T
    qt = jnp.concatenate([q, jnp.zeros((LANES - nb, dk), F32)], axis=0).T if nb < LANES else q.T
    for b in range(nb):
        s = s_in_ref[b] * decay[b:b + 1, :]
        kcol = kt[:, b:b + 1]
        v_new = (v[b:b + 1, :] - _sum_rows(s * kcol)) * beta[b:b + 1, :]
        s = s + kcol * v_new
        if stack_layer is None:
            s_out_ref[b] = s
        else:
            for d in range(s_out_ref.shape[0]):
                s_out_ref[d, b] = s if d == stack_layer else jnp.zeros_like(s)
        osc[b:b + 1, :] = _sum_rows(s * qt[:, b:b + 1])
    o_ref[...] = _gated_out_norm(osc[...], z_ref[...], nw_ref[...]).astype(o_ref.dtype)


def _gdn_sample(qkvz, bg, row0, conv_buf, conv_w, norm_w, state, layer, n_heads, dk,
                state_out=None):
    H = n_heads
    nb_total = state.shape[1]
    nb = 32 if nb_total % 32 == 0 else 16
    assert nb_total % nb == 0 and row0 % nb == 0
    r0 = row0 // nb
    n_hist = conv_buf.shape[1]

    def col(off):
        return pl.BlockSpec((nb, dk), lambda h, i, off=off: (i + r0, h + off))

    def buf(off):
        return pl.BlockSpec((None, n_hist, nb, dk), lambda h, i, off=off: (layer, 0, i, h + off))

    def cw(off):
        return pl.BlockSpec((None, conv_w.shape[1], dk), lambda h, i, off=off: (layer, 0, h + off))

    in_specs = [col(0), col(H), col(2 * H), col(3 * H),
                pl.BlockSpec((nb, LANES), lambda h, i: (i + r0, 0)),
                buf(0), buf(H), buf(2 * H),
                cw(0), cw(H), cw(2 * H),
                pl.BlockSpec((None, 1, dk), lambda h, i: (layer, 0, 0)),
                pl.BlockSpec((None, nb, None, dk, dk), lambda h, i: (layer, i, h, 0, 0))]
    args = [qkvz, qkvz, qkvz, qkvz, bg, conv_buf, conv_buf, conv_buf,
            conv_w, conv_w, conv_w, norm_w, state]
    aliases = {}
    if state_out is not None:
        in_specs.append(pl.BlockSpec(memory_space=pl.ANY))
        args.append(state_out)
        aliases = {len(args) - 1: 1}
    depth = state.shape[0]
    if state_out is None:
        stack_spec = pl.BlockSpec((depth, nb, None, dk, dk), lambda h, i: (0, i, h, 0, 0))
    else:
        stack_spec = pl.BlockSpec((None, nb, None, dk, dk), lambda h, i: (layer, i, h, 0, 0))
    nbytes = (4 + 2 * depth) * nb * dk * dk * 4 + 64 * nb * dk * 4 + 64 * dk * dk * 4
    return pl.pallas_call(
        functools.partial(_gdn_sample_kernel, n_heads=H,
                          stack_layer=layer if state_out is None else None),
        out_shape=(jax.ShapeDtypeStruct((nb_total, H * dk), BF16),
                   jax.ShapeDtypeStruct(state.shape, F32))
        + (jax.ShapeDtypeStruct((n_hist, nb_total, H * dk), F32),) * 3,
        grid=(H, nb_total // nb),
        in_specs=in_specs,
        out_specs=(pl.BlockSpec((nb, dk), lambda h, i: (i, h)), stack_spec)
        + (pl.BlockSpec((n_hist, nb, dk), lambda h, i: (0, i, h)),) * 3,
        scratch_shapes=[pltpu.VMEM((nb, dk), F32)],
        input_output_aliases=aliases,
        compiler_params=pltpu.CompilerParams(
            dimension_semantics=("parallel", "parallel"),
            vmem_limit_bytes=_vmem_limit(nbytes)),
        name="gdn_sample",
    )(*args)


def _s5_tables(lam_re, lam_im, log_dt, b_re, b_im, c_re, c_im):
    D, G, P = lam_re.shape
    gc = b_re.shape[-1]
    gpb = LANES // gc
    nblk = G // gpb
    dt = jnp.exp(log_dt)[..., None]
    mag = jnp.exp(lam_re * dt)
    ar = mag * jnp.cos(lam_im * dt)
    ai = mag * jnp.sin(lam_im * dt)
    nr = ar - 1.0
    den = lam_re * lam_re + lam_im * lam_im
    fr = (nr * lam_re + ai * lam_im) / den
    fi = (ai * lam_re - nr * lam_im) / den
    bbar_re = fr[..., None] * b_re - fi[..., None] * b_im
    bbar_im = fr[..., None] * b_im + fi[..., None] * b_re
    rg = lax.broadcasted_iota(jnp.int32, (gpb * gc, gpb * P), 0) // gc
    cg = lax.broadcasted_iota(jnp.int32, (gpb * gc, gpb * P), 1) // P
    diag = rg == cg

    def bmat(bb):
        t = jnp.swapaxes(bb.reshape(D, nblk, gpb * P, gc), 2, 3)
        return jnp.where(diag, jnp.tile(t, (1, 1, gpb, 1)), 0.0)

    def cmat(cc):
        t = jnp.swapaxes(cc.reshape(D, nblk, gpb, gc, P), 3, 4).reshape(D, nblk, gpb * P, gc)
        return jnp.where(diag.T, jnp.tile(t, (1, 1, 1, gpb)), 0.0)

    b_blk = jnp.concatenate([bmat(bbar_re), bmat(bbar_im)], axis=3).astype(BF16)
    c_blk = jnp.concatenate([cmat(c_re), -cmat(c_im)], axis=2).astype(BF16)
    return (b_blk, c_blk, ar.reshape(D, nblk, 1, gpb * P), ai.reshape(D, nblk, 1, gpb * P))


def _s5_prompt_kernel(u_ref, bblk_ref, cblk_ref, ar_ref, ai_ref, d_ref,
                      g5_ref, g5b_ref, xre_ref, xim_ref, up, xs0, xs1, ys):
    L = u_ref.shape[0]
    ns = ar_ref.shape[-1]
    nseg = S5_SEGMENTS
    seg = L // nseg
    R = xs0.shape[0] // max(xs0.shape[0] // (S5_BLOCK_STEPS * nseg), 1)
    steps = R // nseg
    n_pair = xs0.shape[0] // R
    n_blk = 2 * n_pair
    assert n_blk * R == L

    bmat = bblk_ref[...]
    cmat = cblk_ref[...]
    ar = jnp.broadcast_to(ar_ref[...], (nseg, ns))
    ai = jnp.broadcast_to(ai_ref[...], (nseg, ns))

    def block(row0):
        return pl.ds(pl.multiple_of(row0, R), R)

    def project(buf, blk, row0):
        u = jnp.concatenate([u_ref[pl.ds(blk * steps + t, nseg, stride=seg), :]
                             for t in range(steps)], axis=0)
        up[block(blk * R), :] = u
        buf[block(row0), :] = jnp.dot(u.astype(BF16), bmat, preferred_element_type=F32)

    def scan(buf, row0, x, store):
        for t in range(steps):
            rows = pl.ds(pl.multiple_of(row0 + t * nseg, nseg), nseg)
            r = buf[rows, :]
            xr, xi = x
            x = (ar * xr - ai * xi + r[:, :ns], ar * xi + ai * xr + r[:, ns:])
            if store:
                buf[rows, :] = jnp.concatenate(x, axis=1)
        return x

    def emit(buf, row0, blk):
        y = jnp.dot(buf[block(row0), :].astype(BF16), cmat, preferred_element_type=F32)
        ys[block(blk * R), :] = _gelu_tanh(y + d_ref[...] * up[block(blk * R), :])

    project(xs0, 0, 0)

    def pass1(p, e):
        row0 = p * R
        project(xs1, 2 * p + 1, row0)
        e = scan(xs0, row0, e, False)
        nxt = jnp.minimum(p + 1, n_pair - 1)
        project(xs0, 2 * nxt, nxt * R)
        return scan(xs1, row0, e, False)

    zero = jnp.zeros((nseg, ns), F32)
    er, ei = lax.fori_loop(0, n_pair, pass1, (zero, zero))
    pr, pi = ar_ref[...], ai_ref[...]
    for _ in range(seg.bit_length() - 1):
        pr, pi = pr * pr - pi * pi, 2.0 * pr * pi
    assert seg == 1 << (seg.bit_length() - 1)
    cr = [jnp.zeros((1, ns), F32)]
    ci = [jnp.zeros((1, ns), F32)]
    for s in range(nseg - 1):
        cr.append(er[s:s + 1] + pr * cr[s] - pi * ci[s])
        ci.append(ei[s:s + 1] + pr * ci[s] + pi * cr[s])
    x = (jnp.concatenate(cr, axis=0), jnp.concatenate(ci, axis=0))

    x = scan(xs0, 0, x, True)

    def pass2(p, x):
        row0 = p * R
        x = scan(xs1, row0, x, True)
        emit(xs0, row0, 2 * p)
        x = scan(xs0, row0 + R, x, True)
        emit(xs1, row0, 2 * p + 1)
        return x

    x = lax.fori_loop(0, n_pair - 1, pass2, x)
    last = (n_pair - 1) * R
    x = scan(xs1, last, x, True)
    emit(xs0, last, n_blk - 2)
    emit(xs1, last, n_blk - 1)
    xre_ref[...] = x[0][nseg - 1:nseg]
    xim_ref[...] = x[1][nseg - 1:nseg]
    for s in range(nseg):
        g5 = ys[pl.ds(s, seg, stride=nseg), :]
        g5_ref[s * seg:(s + 1) * seg, :] = g5
        g5b_ref[s * seg:(s + 1) * seg, :] = g5.astype(BF16)


def _s5_prompt(u_all, tables, d_skip, layer, batch, seq, n_ch):
    b_blk, c_blk, ar, ai = tables
    nblk, _, ns2 = b_blk.shape[1:]
    ns = ns2 // 2
    L = seq
    nbytes = (2 * L * LANES * 4 + 2 * L * LANES * 6 + 3 * L * ns2 * 4 + L * ns2 * 2
              + 8 * LANES * ns2 * 2)
    blkp = lambda r, c: pl.BlockSpec((None, None, r, c), lambda b, j: (layer, j, 0, 0))
    return pl.pallas_call(
        _s5_prompt_kernel,
        out_shape=(jax.ShapeDtypeStruct((batch * L, n_ch), F32),
                   jax.ShapeDtypeStruct((batch * L, n_ch), BF16),
                   jax.ShapeDtypeStruct((batch, 1, nblk * ns), F32),
                   jax.ShapeDtypeStruct((batch, 1, nblk * ns), F32)),
        grid=(batch, nblk),
        in_specs=[pl.BlockSpec((L, LANES), lambda b, j: (b, j)),
                  blkp(LANES, ns2), blkp(ns2, LANES), blkp(1, ns), blkp(1, ns),
                  pl.BlockSpec((None, 1, LANES), lambda b, j: (layer, 0, j))],
        out_specs=(pl.BlockSpec((L, LANES), lambda b, j: (b, j)),
                   pl.BlockSpec((L, LANES), lambda b, j: (b, j)),
                   pl.BlockSpec((None, 1, ns), lambda b, j: (b, 0, j)),
                   pl.BlockSpec((None, 1, ns), lambda b, j: (b, 0, j))),
        scratch_shapes=[pltpu.VMEM((L, LANES), F32),
                        pltpu.VMEM((L // 2, ns2), F32),
                        pltpu.VMEM((L // 2, ns2), F32),
                        pltpu.VMEM((L, LANES), F32)],
        compiler_params=pltpu.CompilerParams(
            dimension_semantics=("parallel", "parallel"),
            vmem_limit_bytes=_vmem_limit(nbytes)),
        name="s5_prompt",
    )(u_all, b_blk, c_blk, ar, ai, d_skip)


def _s5_sample_kernel(u_ref, bblk_ref, cblk_ref, ar_ref, ai_ref, d_ref, x0r_ref, x0i_ref,
                      g5_ref, g5b_ref, xre_ref, xim_ref):
    ns = ar_ref.shape[-1]
    u = u_ref[...]
    bu = jnp.dot(u.astype(BF16), bblk_ref[...], preferred_element_type=F32)
    ar, ai = ar_ref[...], ai_ref[...]
    x0r, x0i = x0r_ref[...], x0i_ref[...]
    xr = ar * x0r - ai * x0i + bu[:, :ns]
    xi = ar * x0i + ai * x0r + bu[:, ns:]
    xre_ref[...] = xr
    xim_ref[...] = xi
    x = jnp.concatenate([xr, xi], axis=1)
    y = jnp.dot(x.astype(BF16), cblk_ref[...], preferred_element_type=F32) + d_ref[...] * u
    g5 = _gelu_tanh(y)
    g5_ref[...] = g5
    g5b_ref[...] = g5.astype(BF16)


def _s5_sample(u_all, row0, tables, d_skip, x0_re, x0_im, layer, n_ch):
    b_blk, c_blk, ar, ai = tables
    nblk, _, ns2 = b_blk.shape[1:]
    ns = ns2 // 2
    nb = x0_re.shape[1]
    assert row0 % nb == 0
    r0 = row0 // nb
    blkp = lambda r, c: pl.BlockSpec((None, None, r, c), lambda j: (layer, j, 0, 0))
    nbytes = 16 * nb * ns2 * 4 + 8 * LANES * ns2 * 2
    return pl.pallas_call(
        _s5_sample_kernel,
        out_shape=(jax.ShapeDtypeStruct((nb, n_ch), F32),
                   jax.ShapeDtypeStruct((nb, n_ch), BF16),
                   jax.ShapeDtypeStruct((nb, nblk * ns), F32),
                   jax.ShapeDtypeStruct((nb, nblk * ns), F32)),
        grid=(nblk,),
        in_specs=[pl.BlockSpec((nb, LANES), lambda j: (r0, j)),
                  blkp(LANES, ns2), blkp(ns2, LANES), blkp(1, ns), blkp(1, ns),
                  pl.BlockSpec((None, 1, LANES), lambda j: (layer, 0, j)),
                  pl.BlockSpec((None, nb, ns), lambda j: (layer, 0, j)),
                  pl.BlockSpec((None, nb, ns), lambda j: (layer, 0, j))],
        out_specs=(pl.BlockSpec((nb, LANES), lambda j: (0, j)),
                   pl.BlockSpec((nb, LANES), lambda j: (0, j)),
                   pl.BlockSpec((nb, ns), lambda j: (0, j)),
                   pl.BlockSpec((nb, ns), lambda j: (0, j))),
        compiler_params=pltpu.CompilerParams(
            dimension_semantics=("parallel",), vmem_limit_bytes=_vmem_limit(nbytes)),
        name="s5_sample",
    )(u_all, b_blk, c_blk, ar, ai, d_skip, x0_re, x0_im)


def kernel(x_prompt, x_sample, state_dn_conv, state_dn_ssm, state_s5_re, state_s5_im, norm1, w_in, dn_conv_w, dn_a_log, dn_dt_bias, dn_norm_w, w_br_dn, s5_lam_re, s5_lam_im, s5_log_dt, s5_b_re, s5_b_im, s5_c_re, s5_c_im, s5_d, w_glu, w_br_s5, w_out, norm2, w_ffn_gate, w_ffn_up, w_ffn_down, norm_f):
    batch, seq, d_model = x_prompt.shape
    nb, dec_seq, _ = x_sample.shape
    assert dec_seq == 1
    depth, _, n_heads, dk, dv = state_dn_ssm.shape
    assert dk == LANES and dv == LANES and seq % GDN_CHUNK == 0 and seq % S5_SEGMENTS == 0
    qk_dim = n_heads * dk
    conv_ch = dn_conv_w.shape[2]
    assert conv_ch == 3 * qk_dim
    n_ch = s5_d.shape[1]
    n_groups, n_state = s5_lam_re.shape[1:]
    ffn = w_ffn_gate.shape[2]
    mp = batch * seq
    m = mp + nb
    z_end = 4 * qk_dim
    rest0 = z_end + 2 * n_heads
    assert w_in.shape[2] == rest0 + n_ch + 2 * d_model and 2 * n_heads <= LANES

    x = (x_prompt.reshape(mp, d_model), x_sample.reshape(nb, d_model))

    w_in_t = jnp.swapaxes(w_in, 1, 2)
    w_bg = jnp.pad(w_in_t[:, z_end:rest0, :], ((0, 0), (0, LANES - 2 * n_heads), (0, 0)))
    w_s5 = w_in_t[:, rest0:rest0 + n_ch, :]
    gate_tn = 512
    gate0 = rest0 + n_ch
    gate_base = gate0 - gate0 % gate_tn
    gate_shift = gate0 - gate_base
    gate_cols = -(-(gate_shift + 2 * d_model) // gate_tn) * gate_tn
    assert gate_shift < LANES and gate_base + gate_cols - gate_tn < w_in.shape[2]
    conv_hist = jnp.swapaxes(state_dn_conv, 1, 2)
    pad_heads = lambda a: jnp.pad(a, ((0, 0), (n_heads, LANES - 2 * n_heads)))[:, None, :]
    alog_pad = pad_heads(dn_a_log)
    dtb_pad = pad_heads(dn_dt_bias)
    x0_re = state_s5_re.reshape(depth, nb, n_groups * n_state)
    x0_im = state_s5_im.reshape(depth, nb, n_groups * n_state)

    tm = _pick_tile(m, 1664, 64)
    norm_w3 = dn_norm_w[:, None, :]
    d_skip3 = s5_d[:, None, :]
    tables = _s5_tables(s5_lam_re, s5_lam_im, s5_log_dt, s5_b_re, s5_b_im, s5_c_re, s5_c_im)
    outs = {k: [] for k in ("p_conv", "p_ssm", "p_re", "p_im", "s_conv", "s_re", "s_im")}
    s_ssm = None
    for l in range(depth):
        h = _rmsnorm(x, norm1[l][None, :], BF16)
        qkvz = _fused_matmul([h], [(0, w_in_t, l, 0, True)], [], _ep_identity, z_end, F32,
                             tm=tm, tn=512, name="in_qkvz")
        bg = _fused_matmul([h], [(0, w_bg, l, 0, True)], [], _ep_beta_decay(n_heads), LANES, F32,
                           tm=tm, tn=LANES, name="in_bg",
                           col_params=[(alog_pad, l), (dtb_pad, l)])
        s5u = _fused_matmul([h], [(0, w_s5, l, 0, True)], [], _ep_identity, n_ch, F32,
                            tm=tm, tn=512, name="in_s5")
        gates = _fused_matmul([h], [(0, w_in_t, l, gate_base, True)], [], _ep_identity, gate_cols, F32,
                              tm=tm, tn=gate_tn, name="in_gates",
                              valid_cols=gate_shift + 2 * d_model)

        o_p, ssm_p, *pc = _gdn_prompt(qkvz, bg, dn_conv_w, norm_w3, l, batch, seq, n_heads, dk)
        o_s, s_ssm, *sc = _gdn_sample(qkvz, bg, mp, conv_hist, dn_conv_w, norm_w3,
                                      state_dn_ssm, l, n_heads, dk, state_out=s_ssm)
        outs["p_conv"].append(jnp.concatenate(pc, axis=-1))
        outs["s_conv"].append(jnp.concatenate(sc, axis=-1))
        outs["p_ssm"].append(ssm_p)

        g5_p, g5b_p, re_p, im_p = _s5_prompt(s5u, tables, d_skip3, l, batch, seq, n_ch)
        g5_s, g5b_s, re_s, im_s = _s5_sample(s5u, mp, tables, d_skip3, x0_re, x0_im, l, n_ch)
        outs["p_re"].append(re_p.reshape(batch, n_groups, n_state))
        outs["p_im"].append(im_p.reshape(batch, n_groups, n_state))
        outs["s_re"].append(re_s.reshape(nb, n_groups, n_state))
        outs["s_im"].append(im_s.reshape(nb, n_groups, n_state))
        g5g = _fused_matmul([(g5b_p, g5b_s)], [(0, w_glu, l, 0)], [((g5_p, g5_s), 0)],
                            _ep_glu_self, n_ch, BF16, tm=tm, tn=512, name="s5_glu")

        merged = _fused_matmul([(o_p, o_s), g5g], [(0, w_br_dn, l, 0), (1, w_br_s5, l, 0)],
                               [(gates, gate_shift), (gates, gate_shift + d_model)],
                               _ep_gated_merge,
                               d_model, BF16, tm=tm, tn=512, name="branch_merge")
        x = _fused_matmul([merged], [(0, w_out, l, 0)], [(x, 0)], _ep_residual, d_model, F32,
                          tm=tm, tn=512, name="out_proj")

        hmid = _fused_matmul([x], [(0, w_ffn_gate, l, 0), (0, w_ffn_up, l, 0)], [], _ep_swiglu,
                             ffn, BF16, tm=tm, tn=256, name="ffn_up", norm_w=norm2[l][None, :])
        x = _matmul_residual_wstat(hmid, w_ffn_down, l, x, tm=_pick_tile(m, 832, 16), tn=512,
                                   name="ffn_down")

    y_p, y_s = _rmsnorm(x, norm_f[None, :], F32, split_rows=(mp, nb))
    st = lambda k: jnp.stack(outs[k])
    return (y_p.reshape(batch, seq, d_model), y_s.reshape(nb, 1, d_model),
            st("p_conv"), st("p_ssm"), st("p_re"), st("p_im"),
            jnp.swapaxes(st("s_conv"), 1, 2), s_ssm, st("s_re"), st("s_im"))
```
